```python
import math
import jax
import jax.numpy as jnp
from jax import lax
import numpy as np

D_MODEL = 1024
BATCH = 8
SEQ = 2048
DEPTH = 4

GRID_W = 64
CTX_LEN = 256
HEAD_DIM = 64
NORM_EPS = 1e-6

SSD_HEADS = 16
SSD_HEAD_DIM = 64
SSD_INNER = SSD_HEADS * SSD_HEAD_DIM
SSD_GROUPS = 2
SSD_STATE = 128
SSD_CONV = 5
SSD_CHUNK = 128
SSD_XBC = SSD_INNER + 2 * SSD_GROUPS * SSD_STATE

SWA_Q_HEADS = 8
SWA_KV_HEADS = 2
SWA_WINDOW = 128
SWA_BLOCK = 128
SWA_Q = SWA_Q_HEADS * HEAD_DIM
SWA_KV = SWA_KV_HEADS * HEAD_DIM
ROPE_THETA = 10000.0

NA_HEADS = 8
NA_WIN_R = 8
NA_WIN_C = 16
NA_DIM = NA_HEADS * HEAD_DIM

N_BRANCH = 3
IN_SIZES = (SSD_INNER, SSD_XBC, SSD_HEADS, SSD_HEADS, SWA_Q, SWA_KV, SWA_KV, NA_DIM, NA_DIM, NA_DIM, N_BRANCH * D_MODEL)
IN_COLS = SSD_INNER + SSD_XBC + 2 * SSD_HEADS + SWA_Q + 2 * SWA_KV + 3 * NA_DIM + N_BRANCH * D_MODEL

N_EXPERTS = 32
TOP_K = 4
D_EXPERT = D_MODEL
SWIGLU_LIMIT = 7.0
SWIGLU_ALPHA = 1.702
MOE_BLOCK = 128

kernel_name = 'hybrid_ssd_swa_natten_moe_diffusion_trunk'


def rmsnorm(x, g):
    xf = x.astype(jnp.float32)
    y = xf * lax.rsqrt(jnp.mean(xf * xf, axis=-1, keepdims=True) + NORM_EPS)
    return (y * g.astype(jnp.float32)).astype(x.dtype)


def split_cols(t):
    idx = np.cumsum(np.array(IN_SIZES))[:-1].tolist()
    return jnp.split(t, idx, axis=-1)


def to_heads(t, n):
    return t.reshape(t.shape[0], t.shape[1], n, HEAD_DIM)


def rope_axis(u, pos):
    d = u.shape[-1]
    inv = ROPE_THETA ** (-jnp.arange(0, d, 2, dtype=jnp.float32) / d)
    ang = pos.astype(jnp.float32)[:, None] * inv[None, :]
    cos = jnp.cos(ang)[None, :, None, :]
    sin = jnp.sin(ang)[None, :, None, :]
    uf = u.astype(jnp.float32)
    u1, u2 = uf[..., : d // 2], uf[..., d // 2:]
    return jnp.concatenate([u1 * cos - u2 * sin, u2 * cos + u1 * sin], axis=-1).astype(u.dtype)


def rope_2d(t, row, col):
    half = t.shape[-1] // 2
    return jnp.concatenate([rope_axis(t[..., :half], row), rope_axis(t[..., half:], col)], axis=-1)


def dwconv_centred(u, w, b):
    pad = w.shape[0] // 2
    y = lax.conv_general_dilated(u, w[:, None, :].astype(u.dtype), window_strides=(1,), padding=[(pad, pad)],
                                 dimension_numbers=('NWC', 'WIO', 'NWC'), feature_group_count=u.shape[-1])
    return y + b


def segsum(a):
    t = a.shape[-1]
    cs = jnp.cumsum(a, axis=-1)
    d = cs[..., :, None] - cs[..., None, :]
    mask = jnp.tril(jnp.ones((t, t), dtype=bool))
    return jnp.where(mask, d, -jnp.inf)


def ssd_scan(x, dt, a_coef, bm, cm, h0):
    b, l, h, p = x.shape
    g, n = bm.shape[2], bm.shape[3]
    hg = h // g
    q = SSD_CHUNK
    nc = l // q
    f32 = jnp.float32
    xf = (x.astype(f32) * dt[..., None]).reshape(b, nc, q, g, hg, p)
    a = jnp.moveaxis((dt * a_coef.astype(f32)).reshape(b, nc, q, g, hg), 2, -1)
    a_cs = jnp.cumsum(a, axis=-1)
    bc = bm.astype(f32).reshape(b, nc, q, g, n)
    cc = cm.astype(f32).reshape(b, nc, q, g, n)
    lmat = jnp.exp(segsum(a))
    cb = jnp.einsum('bclgn,bcsgn->bcgls', cc, bc)
    y_diag = jnp.einsum('bcgjls,bcsgjp->bclgjp', cb[:, :, :, None] * lmat, xf)
    decay_states = jnp.exp(a_cs[..., -1:] - a_cs)
    states = jnp.einsum('bcsgn,bcgjs,bcsgjp->bcgjpn', bc, decay_states, xf)
    states = jnp.concatenate([h0.astype(f32).reshape(b, 1, g, hg, p, n), states], axis=1)
    chunk_a = jnp.pad(a_cs[..., -1], ((0, 0), (1, 0), (0, 0), (0, 0)))
    decay_chunk = jnp.exp(segsum(jnp.moveaxis(chunk_a, 1, -1)))
    new_states = jnp.einsum('bgjzc,bcgjpn->bzgjpn', decay_chunk, states)
    prev_states, final = new_states[:, :-1], new_states[:, -1]
    y_off = jnp.einsum('bclgn,bcgjpn,bcgjl->bclgjp', cc, prev_states, jnp.exp(a_cs))
    y = (y_diag + y_off).reshape(b, l, h, p)
    return y, final.reshape(b, h, p, n)


def ssd_branch(z, xbc, dtf_raw, dtb_raw, p, h0_f, h0_b):
    bsz, l, _ = z.shape
    f32 = jnp.float32
    xbc = jax.nn.silu(dwconv_centred(xbc, p['ssd_conv_w'], p['ssd_conv_b']))
    xs = xbc[..., :SSD_INNER].reshape(bsz, l, SSD_HEADS, SSD_HEAD_DIM)
    bm = xbc[..., SSD_INNER:SSD_INNER + SSD_GROUPS * SSD_STATE].reshape(bsz, l, SSD_GROUPS, SSD_STATE)
    cm = xbc[..., SSD_INNER + SSD_GROUPS * SSD_STATE:].reshape(bsz, l, SSD_GROUPS, SSD_STATE)
    a_log = p['ssd_a_log'].astype(f32)
    dt_bias = p['ssd_dt_bias'].astype(f32)
    dt_f = jax.nn.softplus(dtf_raw.astype(f32) + dt_bias[0])
    dt_b = jax.nn.softplus(dtb_raw.astype(f32) + dt_bias[1])
    y_f, hf = ssd_scan(xs, dt_f, -jnp.exp(a_log[0]), bm, cm, h0_f)
    rev = lambda t: jnp.flip(t, axis=1)
    y_b, hb = ssd_scan(rev(xs), rev(dt_b), -jnp.exp(a_log[1]), rev(bm), rev(cm), h0_b)
    y = y_f + rev(y_b) + p['ssd_d'].astype(f32)[:, None] * xs.astype(f32)
    y = y.reshape(bsz, l, SSD_INNER) * jax.nn.silu(z.astype(f32))
    return rmsnorm(y, p['ssd_norm_g']).astype(z.dtype), hf, hb


def swa_latent(q, k, v, kc, vc, sink):
    bsz, s, hq, d = q.shape
    hkv = k.shape[2]
    g = hq // hkv
    blk = SWA_BLOCK
    nb = s // blk
    lc = kc.shape[1]
    scale = d ** -0.5
    qb = q.reshape(bsz, nb, blk, hkv, g, d)

    def band(t):
        tp = jnp.pad(t, ((0, 0), (blk, blk), (0, 0), (0, 0))).reshape(bsz, nb + 2, blk, hkv, d)
        return jnp.concatenate([tp[:, :-2], tp[:, 1:-1], tp[:, 2:]], axis=2)

    kb, vb = band(k), band(v)
    s_loc = jnp.einsum('bnqhgd,bnjhd->bnhgqj', qb, kb).astype(jnp.float32) * scale
    qi = jnp.arange(blk)[None, :, None]
    kj = jnp.arange(3 * blk)[None, None, :]
    nidx = jnp.arange(nb)[:, None, None]
    kpos = nidx * blk - blk + kj
    qpos = nidx * blk + qi
    mask = (jnp.abs(kpos - qpos) <= SWA_WINDOW) & (kpos >= 0) & (kpos < s)
    s_loc = jnp.where(mask[None, :, None, None], s_loc, -jnp.inf)
    s_ctx = jnp.einsum('bnqhgd,bchd->bnhgqc', qb, kc).astype(jnp.float32) * scale
    s_sink = jnp.broadcast_to(sink.astype(jnp.float32).reshape(hkv, g)[None, None, :, :, None, None],
                              s_loc.shape[:-1] + (1,))
    prob = jax.nn.softmax(jnp.concatenate([s_loc, s_ctx, s_sink], axis=-1), axis=-1)
    p_loc = prob[..., : 3 * blk].astype(v.dtype)
    p_ctx = prob[..., 3 * blk: 3 * blk + lc].astype(v.dtype)
    o = jnp.einsum('bnhgqj,bnjhd->bnqhgd', p_loc, vb) + jnp.einsum('bnhgqc,bchd->bnqhgd', p_ctx, vc)
    return o.reshape(bsz, s, hq * d)


def na_latent(q, k, v, kc, vc, rpb):
    bsz, s, h, d = q.shape
    rows = s // GRID_W
    wr = min(NA_WIN_R, rows)
    wc = NA_WIN_C
    scale = d ** -0.5
    r = jnp.arange(rows)
    cidx = jnp.arange(GRID_W)
    r0 = jnp.clip(r - wr // 2, 0, rows - wr)
    c0 = jnp.clip(cidx - wc // 2, 0, GRID_W - wc)
    krow = r0[:, None] + jnp.arange(wr)[None, :]
    q4 = q.reshape(bsz, rows, GRID_W, h, d)
    k4 = k.reshape(bsz, rows, GRID_W, h, d)[:, krow]
    v4 = v.reshape(bsz, rows, GRID_W, h, d)[:, krow]
    sc = jnp.einsum('brqhd,brikhd->bhrqik', q4, k4).astype(jnp.float32) * scale
    roff = krow - r[:, None] + (NA_WIN_R - 1)
    coff = jnp.clip(cidx[None, :] - cidx[:, None] + (NA_WIN_C - 1), 0, 2 * NA_WIN_C - 2)
    cvalid = (cidx[None, :] >= c0[:, None]) & (cidx[None, :] < c0[:, None] + wc)
    bias = rpb.astype(jnp.float32)[:, roff[:, None, :, None], coff[None, :, None, :]]
    sc = jnp.where(cvalid[:, None, :], sc + bias[None], -jnp.inf)
    s_ctx = jnp.einsum('brqhd,bchd->bhrqc', q4, kc).astype(jnp.float32) * scale
    nk = wr * GRID_W
    prob = jax.nn.softmax(jnp.concatenate([sc.reshape(bsz, h, rows, GRID_W, nk), s_ctx], axis=-1), axis=-1)
    p_loc = prob[..., :nk].reshape(bsz, h, rows, GRID_W, wr, GRID_W).astype(v.dtype)
    p_ctx = prob[..., nk:].astype(v.dtype)
    o = jnp.einsum('bhrqik,brikhd->brqhd', p_loc, v4) + jnp.einsum('bhrqc,bchd->brqhd', p_ctx, vc)
    return o.reshape(bsz, s, h * d)


def ctx_attention(qc, kc, vc, sink=None):
    bsz, lc, hq, d = qc.shape
    hkv = kc.shape[2]
    g = hq // hkv
    qg = qc.reshape(bsz, lc, hkv, g, d)
    s = jnp.einsum('bqhgd,bkhd->bhgqk', qg, kc).astype(jnp.float32) * d ** -0.5
    if sink is None:
        prob = jax.nn.softmax(s, axis=-1)
    else:
        s_sink = jnp.broadcast_to(sink.astype(jnp.float32).reshape(hkv, g)[None, :, :, None, None], s.shape[:-1] + (1,))
        prob = jax.nn.softmax(jnp.concatenate([s, s_sink], axis=-1), axis=-1)[..., :-1]
    o = jnp.einsum('bhgqk,bkhd->bqhgd', prob.astype(vc.dtype), vc)
    return o.reshape(bsz, lc, hq * d)


def merge(o_ssd, o_swa, o_na, g, p):
    gates = jax.nn.sigmoid((g + p['b_gate']).astype(jnp.float32)).astype(o_ssd.dtype)
    ga, gb, gn = jnp.split(gates, N_BRANCH, axis=-1)
    m = ga * (o_ssd @ p['w_br_ssd']) + gb * (o_swa @ p['w_br_swa']) + gn * (o_na @ p['w_br_na'])
    return m @ p['w_out']


def moe(hid, p):
    t, d = hid.shape
    logits = (hid @ p['router_w']).astype(jnp.float32) + p['router_b'].astype(jnp.float32)
    top_v, top_i = lax.top_k(logits, TOP_K)
    gates = jax.nn.softmax(top_v, axis=-1)
    flat_e = top_i.reshape(-1)
    flat_tok = jnp.repeat(jnp.arange(t, dtype=jnp.int32), TOP_K)
    flat_w = gates.reshape(-1)
    order = jnp.argsort(flat_e)
    se = flat_e[order]
    counts = jnp.bincount(flat_e, length=N_EXPERTS)
    starts = jnp.cumsum(counts) - counts
    padded = (counts + MOE_BLOCK - 1) // MOE_BLOCK * MOE_BLOCK
    pends = jnp.cumsum(padded)
    pstarts = pends - padded
    slot = pstarts[se] + (jnp.arange(t * TOP_K) - starts[se])
    n_blocks = (t * TOP_K + MOE_BLOCK - 1) // MOE_BLOCK + N_EXPERTS
    n_slots = n_blocks * MOE_BLOCK
    slot_tok = jnp.zeros((n_slots,), jnp.int32).at[slot].set(flat_tok[order])
    slot_w = jnp.zeros((n_slots,), jnp.float32).at[slot].set(flat_w[order])
    block_e = jnp.clip(jnp.searchsorted(pends, jnp.arange(n_blocks) * MOE_BLOCK, side='right'), 0, N_EXPERTS - 1)
    xs = hid[slot_tok].reshape(n_blocks, MOE_BLOCK, d)
    w1, b1, w2, b2 = p['moe_w1'], p['moe_b1'], p['moe_w2'], p['moe_b2']

    def expert_block(args):
        xb, e = args
        gu = xb @ w1[e] + b1[e]
        gate = jnp.minimum(gu[:, :D_EXPERT], SWIGLU_LIMIT)
        up = jnp.clip(gu[:, D_EXPERT:], -SWIGLU_LIMIT, SWIGLU_LIMIT)
        act = (up + 1) * gate * jax.nn.sigmoid(SWIGLU_ALPHA * gate)
        return act @ w2[e] + b2[e]

    ys = lax.map(expert_block, (xs, block_e)).reshape(n_slots, d)
    ys = ys * slot_w[:, None].astype(ys.dtype)
    return jax.ops.segment_sum(ys, slot_tok, num_segments=t)


def layer(xl, xc, sc_l, sc_c, row, col, p, last):
    bsz, s, d = xl.shape
    lc = xc.shape[1]
    mod_l = (sc_l @ p['w_ada'] + p['b_ada'])[:, None, :]
    mod_c = sc_c @ p['w_ada'] + p['b_ada']
    sh1_l, s1_l, g1_l, sh2_l, s2_l, g2_l = jnp.split(mod_l, 6, axis=-1)
    sh1_c, s1_c, g1_c, sh2_c, s2_c, g2_c = jnp.split(mod_c, 6, axis=-1)

    hl = rmsnorm(xl, p['norm1_g']) * (1 + s1_l) + sh1_l
    hc = rmsnorm(xc, p['norm1_g']) * (1 + s1_c) + sh1_c
    zl, xbcl, dtfl, dtbl, qsl, ksl, vsl, qnl, knl, vnl, gl = split_cols(hl @ p['w_in'])
    zc, xbcc, dtfc, dtbc, qsc, ksc, vsc, qnc, knc, vnc, gc = split_cols(hc @ p['w_in'])

    h0 = jnp.zeros((bsz, SSD_HEADS, SSD_HEAD_DIM, SSD_STATE), jnp.float32)
    ssd_c, hf, hb = ssd_branch(zc, xbcc, dtfc, dtbc, p, h0, h0)
    ssd_l, _, _ = ssd_branch(zl, xbcl, dtfl, dtbl, p, hf, hb)

    ks_c, vs_c = to_heads(ksc, SWA_KV_HEADS), to_heads(vsc, SWA_KV_HEADS)
    swa_l = swa_latent(rope_2d(to_heads(qsl, SWA_Q_HEADS), row, col), rope_2d(to_heads(ksl, SWA_KV_HEADS), row, col),
                       to_heads(vsl, SWA_KV_HEADS), ks_c, vs_c, p['swa_sink'])

    kn_c, vn_c = to_heads(knc, NA_HEADS), to_heads(vnc, NA_HEADS)
    na_l = na_latent(to_heads(qnl, NA_HEADS), to_heads(knl, NA_HEADS), to_heads(vnl, NA_HEADS), kn_c, vn_c, p['na_rpb'])

    xl = xl + g1_l * merge(ssd_l, swa_l, na_l, gl, p)
    if not last:
        swa_c = ctx_attention(to_heads(qsc, SWA_Q_HEADS), ks_c, vs_c, p['swa_sink'])
        na_c = ctx_attention(to_heads(qnc, NA_HEADS), kn_c, vn_c)
        xc = xc + g1_c * merge(ssd_c, swa_c, na_c, gc, p)

    fl = rmsnorm(xl, p['norm2_g']) * (1 + s2_l) + sh2_l
    if last:
        yl = moe(fl.reshape(bsz * s, d), p).reshape(bsz, s, d)
    else:
        fc = rmsnorm(xc, p['norm2_g']) * (1 + s2_c) + sh2_c
        y = moe(jnp.concatenate([fc.reshape(bsz * lc, d), fl.reshape(bsz * s, d)], axis=0), p)
        xc = xc + g2_c * y[: bsz * lc].reshape(bsz, lc, d)
        yl = y[bsz * lc:].reshape(bsz, s, d)
    xl = xl + g2_l * yl
    return xl, xc


def setup_inputs(seed: int = 0) -> dict:
    key = jax.random.key(seed)
    keys = jax.random.split(key, 32)
    ctr = [0]

    def nxt():
        ctr[0] += 1
        return keys[ctr[0] - 1]

    def nrm(shape, scale):
        return jax.random.normal(nxt(), shape, jnp.float32) * scale

    L, D = DEPTH, D_MODEL
    x = nrm((BATCH, SEQ, D), 1.0)
    c = nrm((BATCH, D), 1.0)
    ctx = nrm((BATCH, CTX_LEN, D), 1.0)
    c_ctx = nrm((D,), 1.0)
    norm1_g = 1.0 + nrm((L, D), 0.05)
    norm2_g = 1.0 + nrm((L, D), 0.05)
    w_ada = nrm((L, D, 6 * D), 0.5 * D ** -0.5)
    b_ada = nrm((L, 6 * D), 0.01)
    w_in = nrm((L, D, IN_COLS), D ** -0.5)
    b_gate = nrm((L, N_BRANCH * D), 0.01)
    ssd_conv_w = nrm((L, SSD_CONV, SSD_XBC), SSD_CONV ** -0.5)
    ssd_conv_b = nrm((L, SSD_XBC), 0.01)
    ssd_a_log = jnp.log(jax.random.uniform(nxt(), (L, 2, SSD_HEADS), jnp.float32, 1.0, 16.0))
    dt0 = jnp.exp(jax.random.uniform(nxt(), (L, 2, SSD_HEADS), jnp.float32, math.log(1e-3), math.log(1e-1)))
    ssd_dt_bias = dt0 + jnp.log(-jnp.expm1(-dt0))
    ssd_d = 1.0 + nrm((L, SSD_HEADS), 0.1)
    ssd_norm_g = 1.0 + nrm((L, SSD_INNER), 0.05)
    swa_sink = nrm((L, SWA_Q_HEADS), 0.5)
    na_rpb = nrm((L, NA_HEADS, 2 * NA_WIN_R - 1, 2 * NA_WIN_C - 1), 0.1)
    w_br_ssd = nrm((L, SSD_INNER, D), SSD_INNER ** -0.5)
    w_br_swa = nrm((L, SWA_Q, D), SWA_Q ** -0.5)
    w_br_na = nrm((L, NA_DIM, D), NA_DIM ** -0.5)
    w_out = nrm((L, D, D), D ** -0.5)
    router_w = nrm((L, D, N_EXPERTS), D ** -0.5)
    router_b = nrm((L, N_EXPERTS), 0.01)
    moe_w1 = nrm((L, N_EXPERTS, D, 2 * D_EXPERT), D ** -0.5)
    moe_b1 = nrm((L, N_EXPERTS, 2 * D_EXPERT), 0.01)
    moe_w2 = nrm((L, N_EXPERTS, D_EXPERT, D), D_EXPERT ** -0.5)
    moe_b2 = nrm((L, N_EXPERTS, D), 0.01)
    final_norm_g = 1.0 + nrm((D,), 0.05)
    return {'x': x, 'c': c, 'ctx': ctx, 'c_ctx': c_ctx, 'norm1_g': norm1_g, 'norm2_g': norm2_g,
            'w_ada': w_ada, 'b_ada': b_ada, 'w_in': w_in, 'b_gate': b_gate,
            'ssd_conv_w': ssd_conv_w, 'ssd_conv_b': ssd_conv_b, 'ssd_a_log': ssd_a_log,
            'ssd_dt_bias': ssd_dt_bias, 'ssd_d': ssd_d, 'ssd_norm_g': ssd_norm_g,
            'swa_sink': swa_sink, 'na_rpb': na_rpb, 'w_br_ssd': w_br_ssd, 'w_br_swa': w_br_swa,
            'w_br_na': w_br_na, 'w_out': w_out, 'router_w': router_w, 'router_b': router_b,
            'moe_w1': moe_w1, 'moe_b1': moe_b1, 'moe_w2': moe_w2, 'moe_b2': moe_b2,
            'final_norm_g': final_norm_g}


def reference(x, c, ctx, c_ctx, norm1_g, norm2_g, w_ada, b_ada, w_in, b_gate, ssd_conv_w, ssd_conv_b,
              ssd_a_log, ssd_dt_bias, ssd_d, ssd_norm_g, swa_sink, na_rpb, w_br_ssd, w_br_swa, w_br_na,
              w_out, router_w, router_b, moe_w1, moe_b1, moe_w2, moe_b2, final_norm_g):
    s = x.shape[1]
    pos = jnp.arange(s)
    row, col = pos // GRID_W, pos % GRID_W
    sc_l = jax.nn.silu(c)
    sc_c = jax.nn.silu(c_ctx)
    xl, xc = x, ctx
    for i in range(DEPTH):
        p = {'norm1_g': norm1_g[i], 'norm2_g': norm2_g[i], 'w_ada': w_ada[i], 'b_ada': b_ada[i],
             'w_in': w_in[i], 'b_gate': b_gate[i], 'ssd_conv_w': ssd_conv_w[i], 'ssd_conv_b': ssd_conv_b[i],
             'ssd_a_log': ssd_a_log[i], 'ssd_dt_bias': ssd_dt_bias[i], 'ssd_d': ssd_d[i],
             'ssd_norm_g': ssd_norm_g[i], 'swa_sink': swa_sink[i], 'na_rpb': na_rpb[i],
             'w_br_ssd': w_br_ssd[i], 'w_br_swa': w_br_swa[i], 'w_br_na': w_br_na[i], 'w_out': w_out[i],
             'router_w': router_w[i], 'router_b': router_b[i], 'moe_w1': moe_w1[i], 'moe_b1': moe_b1[i],
             'moe_w2': moe_w2[i], 'moe_b2': moe_b2[i]}
        xl, xc = layer(xl, xc, sc_l, sc_c, row, col, p, i == DEPTH - 1)
    return rmsnorm(xl, final_norm_g)
```

```python
import functools
import math

import jax
import jax.numpy as jnp
import numpy as np
from jax import lax
from jax.experimental import pallas as pl
from jax.experimental.pallas import tpu as pltpu

F32 = jnp.float32
BF16 = jnp.bfloat16

D_MODEL = 1024
GRID_W = 64
CTX_LEN = 256
HEAD_DIM = 64
NORM_EPS = 1e-6

SSD_HEADS = 16
SSD_HEAD_DIM = 64
SSD_INNER = SSD_HEADS * SSD_HEAD_DIM
SSD_GROUPS = 2
SSD_STATE = 128
SSD_CHUNK = 128

SWA_Q_HEADS = 8
SWA_KV_HEADS = 2
SWA_WINDOW = 128
SWA_BLOCK = 128
SWA_Q = SWA_Q_HEADS * HEAD_DIM
SWA_KV = SWA_KV_HEADS * HEAD_DIM
ROPE_THETA = 10000.0

NA_HEADS = 8
NA_WIN_R = 8
NA_WIN_C = 16
NA_DIM = NA_HEADS * HEAD_DIM

N_BRANCH = 3
N_EXPERTS = 32
TOP_K = 4
SWIGLU_LIMIT = 7.0
SWIGLU_ALPHA = 1.702

COL_GATE = 0
COL_Z = 3 * D_MODEL
COL_XS = COL_Z + SSD_INNER
COL_BC = COL_XS + SSD_INNER
COL_QS = COL_BC + 512
COL_QN = COL_QS + SWA_Q
COL_KN = COL_QN + NA_DIM
COL_VN = COL_KN + NA_DIM
COL_KS = COL_VN + NA_DIM
COL_VS = COL_KS + SWA_KV
COL_DT = COL_VS + SWA_KV
IN_COLS_PACKED = COL_DT + 128

MOE_BLOCK_ROWS = 256
VMEM_LIMIT = 56 * 1024 * 1024


def _cparams(sem):
    return pltpu.CompilerParams(dimension_semantics=sem, vmem_limit_bytes=VMEM_LIMIT)


def _ada_kernel(c_ref, w_ref, b_ref, o_ref):
    c = c_ref[...]
    s = (c * jax.nn.sigmoid(c)).astype(BF16)
    o_ref[...] = jnp.dot(s, w_ref[...].astype(BF16), preferred_element_type=F32) + b_ref[...]


def ada_modulation(rows, w_ada, b_ada):
    r, d = rows.shape
    n = w_ada.shape[1]
    tn = 1024
    return pl.pallas_call(
        _ada_kernel,
        out_shape=jax.ShapeDtypeStruct((r, n), F32),
        grid=(n // tn,),
        in_specs=[pl.BlockSpec((r, d), lambda j: (0, 0)),
                  pl.BlockSpec((d, tn), lambda j: (0, j)),
                  pl.BlockSpec((1, tn), lambda j: (0, j))],
        out_specs=pl.BlockSpec((r, tn), lambda j: (0, j)),
        compiler_params=_cparams(("parallel",)),
        name="ada_modulation",
    )(rows, w_ada, b_ada.reshape(1, n))


def _row_mod(mod_l_ref, mod_c_ref, idx, is_ctx):
    return jnp.where(is_ctx, mod_c_ref[0, idx:idx + 1, :], mod_l_ref[0, idx:idx + 1, :])


def _rms_mod(x, g, scale, shift):
    y = x * lax.rsqrt(jnp.mean(x * x, axis=-1, keepdims=True) + NORM_EPS)
    return (y * g) * (1.0 + scale) + shift


def _norm_proj_kernel(x_ref, ml_ref, mc_ref, g_ref, w_ref, o_ref, h_ref, *, tm, n_ctx):
    i = pl.program_id(1)
    j = pl.program_id(2)

    @pl.when(j == 0)
    def _():
        rows = i * tm + lax.broadcasted_iota(jnp.int32, (tm, 1), 0)
        is_ctx = rows < n_ctx
        h = _rms_mod(x_ref[0], g_ref[...], _row_mod(ml_ref, mc_ref, 1, is_ctx), _row_mod(ml_ref, mc_ref, 0, is_ctx))
        h_ref[...] = h.astype(BF16)

    o_ref[0] = jnp.dot(h_ref[...], w_ref[...], preferred_element_type=F32)


def norm_proj(x, mod, g, w_packed, n_ctx):
    bsz, l, d = x.shape
    n = w_packed.shape[1]
    tm, tn = 768, 1152
    return pl.pallas_call(
        functools.partial(_norm_proj_kernel, tm=tm, n_ctx=n_ctx),
        out_shape=jax.ShapeDtypeStruct((bsz, l, n), F32),
        grid=(bsz, l // tm, n // tn),
        in_specs=[pl.BlockSpec((1, tm, d), lambda b, i, j: (b, i, 0)),
                  pl.BlockSpec((1, 6, d), lambda b, i, j: (b, 0, 0)),
                  pl.BlockSpec((1, 6, d), lambda b, i, j: (bsz, 0, 0)),
                  pl.BlockSpec((1, d), lambda b, i, j: (0, 0)),
                  pl.BlockSpec((d, tn), lambda b, i, j: (0, j))],
        out_specs=pl.BlockSpec((1, tm, tn), lambda b, i, j: (b, i, j)),
        scratch_shapes=[pltpu.VMEM((tm, d), BF16)],
        compiler_params=_cparams(("parallel", "parallel", "arbitrary")),
        name="norm_proj",
    )(x, mod, mod, g.reshape(1, d), w_packed)


def _merge_kernel(x_ref, ga_ref, gb_ref, gn_ref, ossd_ref, oswa_ref, ona_ref, bg_ref, wssd_ref, wswa_ref, wna_ref,
                  wout_ref, ml_ref, mc_ref, o_ref, *, tm, n_ctx):
    i = pl.program_id(1)
    rows = i * tm + lax.broadcasted_iota(jnp.int32, (tm, 1), 0)
    is_ctx = rows < n_ctx

    def branch(gate_ref, k, o_br_ref, w_ref):
        gate = jax.nn.sigmoid(gate_ref[0] + bg_ref[k:k + 1, :])
        return gate * jnp.dot(o_br_ref[0].astype(BF16), w_ref[...], preferred_element_type=F32)

    m = branch(ga_ref, 0, ossd_ref, wssd_ref) + branch(gb_ref, 1, oswa_ref, wswa_ref) + branch(gn_ref, 2, ona_ref, wna_ref)
    y = jnp.dot(m.astype(BF16), wout_ref[...], preferred_element_type=F32)
    o_ref[0] = x_ref[0] + _row_mod(ml_ref, mc_ref, 2, is_ctx) * y


def merge(x, proj, o_ssd, o_swa, o_na, b_gate, w_ssd, w_swa, w_na, w_out, mod, n_ctx):
    bsz, l, d = x.shape
    tm = 256
    gate_blk = COL_GATE // d
    row = lambda w: pl.BlockSpec((1, tm, w), lambda b, i: (b, i, 0))
    full = lambda a: pl.BlockSpec(a.shape, lambda b, i: (0,) * a.ndim)
    return pl.pallas_call(
        functools.partial(_merge_kernel, tm=tm, n_ctx=n_ctx),
        out_shape=jax.ShapeDtypeStruct((bsz, l, d), F32),
        grid=(bsz, l // tm),
        in_specs=[row(d),
                  pl.BlockSpec((1, tm, d), lambda b, i: (b, i, gate_blk)),
                  pl.BlockSpec((1, tm, d), lambda b, i: (b, i, gate_blk + 1)),
                  pl.BlockSpec((1, tm, d), lambda b, i: (b, i, gate_blk + 2)),
                  row(o_ssd.shape[-1]), row(o_swa.shape[-1]), row(o_na.shape[-1]),
                  full(b_gate), full(w_ssd), full(w_swa), full(w_na), full(w_out),
                  pl.BlockSpec((1, 6, d), lambda b, i: (b, 0, 0)),
                  pl.BlockSpec((1, 6, d), lambda b, i: (bsz, 0, 0))],
        out_specs=row(d),
        compiler_params=_cparams(("parallel", "parallel")),
        name="merge",
    )(x, proj, proj, proj, o_ssd, o_swa, o_na, b_gate, w_ssd, w_swa, w_na, w_out, mod, mod)


def _router_kernel(x_ref, ml_ref, mc_ref, g_ref, rw_ref, rb_ref, hid_ref, idx_ref, gate_ref, *, tm, n_ctx, row_off):
    i = pl.program_id(1)
    rows = row_off + i * tm + lax.broadcasted_iota(jnp.int32, (tm, 1), 0)
    is_ctx = rows < n_ctx
    h = _rms_mod(x_ref[0], g_ref[...], _row_mod(ml_ref, mc_ref, 4, is_ctx), _row_mod(ml_ref, mc_ref, 3, is_ctx))
    hb = h.astype(BF16)
    hid_ref[0] = hb
    logits = jnp.dot(hb, rw_ref[...], preferred_element_type=F32) + rb_ref[...]
    lane = lax.broadcasted_iota(jnp.int32, logits.shape, 1)
    vals, idxs = [], []
    for _ in range(TOP_K):
        m = jnp.max(logits, axis=-1, keepdims=True)
        sel = jnp.min(jnp.where(logits == m, lane, N_EXPERTS), axis=-1, keepdims=True)
        vals.append(m)
        idxs.append(sel)
        logits = jnp.where(lane == sel, -jnp.inf, logits)
    ex = [jnp.exp(v - vals[0]) for v in vals]
    denom = ex[0] + ex[1] + ex[2] + ex[3]
    col = lax.broadcasted_iota(jnp.int32, (tm, TOP_K), 1)
    idx_out = jnp.zeros((tm, TOP_K), jnp.int32)
    gate_out = jnp.zeros((tm, TOP_K), F32)
    for k in range(TOP_K):
        idx_out = jnp.where(col == k, idxs[k], idx_out)
        gate_out = jnp.where(col == k, ex[k] / denom, gate_out)
    idx_ref[0] = idx_out
    gate_ref[0] = gate_out


def moe_router(x, mod, g, router_w, router_b, n_ctx, row_off):
    bsz, l, d = x.shape
    tm = 256
    nrow = (l - row_off) // tm
    off_blk = row_off // tm
    return pl.pallas_call(
        functools.partial(_router_kernel, tm=tm, n_ctx=n_ctx, row_off=row_off),
        out_shape=(jax.ShapeDtypeStruct((bsz, l - row_off, d), BF16),
                   jax.ShapeDtypeStruct((bsz, l - row_off, TOP_K), jnp.int32),
                   jax.ShapeDtypeStruct((bsz, l - row_off, TOP_K), F32)),
        grid=(bsz, nrow),
        in_specs=[pl.BlockSpec((1, tm, d), lambda b, i: (b, i + off_blk, 0)),
                  pl.BlockSpec((1, 6, d), lambda b, i: (b, 0, 0)),
                  pl.BlockSpec((1, 6, d), lambda b, i: (bsz, 0, 0)),
                  pl.BlockSpec((1, d), lambda b, i: (0, 0)),
                  pl.BlockSpec((d, N_EXPERTS), lambda b, i: (0, 0)),
                  pl.BlockSpec((1, N_EXPERTS), lambda b, i: (0, 0))],
        out_specs=(pl.BlockSpec((1, tm, d), lambda b, i: (b, i, 0)),
                   pl.BlockSpec((1, tm, TOP_K), lambda b, i: (b, i, 0)),
                   pl.BlockSpec((1, tm, TOP_K), lambda b, i: (b, i, 0))),
        compiler_params=_cparams(("parallel", "parallel")),
        name="moe_router",
    )(x, mod, mod, g.reshape(1, d), router_w.astype(BF16), router_b.reshape(1, N_EXPERTS))


def _expert_kernel(be_ref, nu_ref, xs_ref, sw_ref, w1_ref, b1_ref, w2_ref, b2_ref, o_ref):
    i = pl.program_id(0)

    @pl.when(i < nu_ref[0])
    def _():
        de = w2_ref.shape[1]
        gu = jnp.dot(xs_ref[...], w1_ref[0], preferred_element_type=F32) + b1_ref[0]
        gate = jnp.minimum(gu[:, :de], SWIGLU_LIMIT)
        up = jnp.clip(gu[:, de:], -SWIGLU_LIMIT, SWIGLU_LIMIT)
        act = (up + 1.0) * gate * jax.nn.sigmoid(SWIGLU_ALPHA * gate)
        y = jnp.dot(act.astype(BF16), w2_ref[0], preferred_element_type=F32) + b2_ref[0]
        o_ref[...] = y * sw_ref[...]

    @pl.when(i >= nu_ref[0])
    def _():
        o_ref[...] = jnp.zeros_like(o_ref)


def moe_experts(xs, slot_w, block_e, n_used, w1, b1, w2, b2):
    n_slots, d = xs.shape
    bm = MOE_BLOCK_ROWS
    n_exp, _, two_de = w1.shape
    de = two_de // 2
    return pl.pallas_call(
        _expert_kernel,
        out_shape=jax.ShapeDtypeStruct((n_slots, d), F32),
        grid_spec=pltpu.PrefetchScalarGridSpec(
            num_scalar_prefetch=2,
            grid=(n_slots // bm,),
            in_specs=[pl.BlockSpec((bm, d), lambda i, be, nu: (i, 0)),
                      pl.BlockSpec((bm, 1), lambda i, be, nu: (i, 0)),
                      pl.BlockSpec((1, d, two_de), lambda i, be, nu: (be[i], 0, 0)),
                      pl.BlockSpec((1, 1, two_de), lambda i, be, nu: (be[i], 0, 0)),
                      pl.BlockSpec((1, de, d), lambda i, be, nu: (be[i], 0, 0)),
                      pl.BlockSpec((1, 1, d), lambda i, be, nu: (be[i], 0, 0))],
            out_specs=pl.BlockSpec((bm, d), lambda i, be, nu: (i, 0))),
        compiler_params=_cparams(("arbitrary",)),
        name="moe_experts",
    )(block_e, n_used, xs, slot_w.reshape(n_slots, 1), w1, b1.reshape(n_exp, 1, two_de), w2, b2.reshape(n_exp, 1, d))


def _combine_kernel(x_ref, y_ref, ml_ref, mc_ref, o_ref, *, tm, n_ctx, row_off):
    i = pl.program_id(1)
    rows = row_off + i * tm + lax.broadcasted_iota(jnp.int32, (tm, 1), 0)
    is_ctx = rows < n_ctx
    y = y_ref[0, :, 0, :] + y_ref[0, :, 1, :] + y_ref[0, :, 2, :] + y_ref[0, :, 3, :]
    o_ref[0] = x_ref[0] + _row_mod(ml_ref, mc_ref, 5, is_ctx) * y


def moe_combine(x, yk, mod, n_ctx, row_off):
    bsz, l, d = x.shape
    tm = 128
    nrow = (l - row_off) // tm
    off_blk = row_off // tm
    return pl.pallas_call(
        functools.partial(_combine_kernel, tm=tm, n_ctx=n_ctx, row_off=row_off),
        out_shape=jax.ShapeDtypeStruct((bsz, l - row_off, d), F32),
        grid=(bsz, nrow),
        in_specs=[pl.BlockSpec((1, tm, d), lambda b, i: (b, i + off_blk, 0)),
                  pl.BlockSpec((1, tm, TOP_K, d), lambda b, i: (b, i, 0, 0)),
                  pl.BlockSpec((1, 6, d), lambda b, i: (b, 0, 0)),
                  pl.BlockSpec((1, 6, d), lambda b, i: (bsz, 0, 0))],
        out_specs=pl.BlockSpec((1, tm, d), lambda b, i: (b, i, 0)),
        compiler_params=_cparams(("parallel", "parallel")),
        name="moe_combine",
    )(x, yk, mod, mod)


def moe_layer(x, mod, g, router_w, router_b, w1, b1, w2, b2, n_ctx, row_off):
    bsz, l, d = x.shape
    hid, top_i, gates = moe_router(x, mod, g, router_w, router_b, n_ctx, row_off)
    t = bsz * (l - row_off)
    bm = MOE_BLOCK_ROWS
    flat_e = top_i.reshape(-1)
    flat_w = gates.reshape(-1)
    order = jnp.argsort(flat_e)
    se = flat_e[order]
    counts = jnp.bincount(flat_e, length=N_EXPERTS)
    starts = jnp.cumsum(counts) - counts
    padded = (counts + bm - 1) // bm * bm
    pends = jnp.cumsum(padded)
    pstarts = pends - padded
    slot = (pstarts[se] + (jnp.arange(t * TOP_K) - starts[se])).astype(jnp.int32)
    n_blocks = (t * TOP_K + bm - 1) // bm + N_EXPERTS
    n_slots = n_blocks * bm
    slot_tok = jnp.zeros((n_slots,), jnp.int32).at[slot].set((order // TOP_K).astype(jnp.int32))
    slot_w = jnp.zeros((n_slots,), F32).at[slot].set(flat_w[order])
    inv_slot = jnp.zeros((t * TOP_K,), jnp.int32).at[order].set(slot)
    block_e = jnp.clip(jnp.searchsorted(pends, jnp.arange(n_blocks) * bm, side='right'), 0, N_EXPERTS - 1).astype(jnp.int32)
    n_used = (pends[-1] // bm).astype(jnp.int32).reshape(1)
    xs = hid.reshape(t, d)[slot_tok]
    ys = moe_experts(xs, slot_w, block_e, n_used, w1.astype(BF16), b1, w2.astype(BF16), b2)
    yk = ys[inv_slot].reshape(bsz, l - row_off, TOP_K, d)
    return moe_combine(x, yk, mod, n_ctx, row_off)


def _final_norm_kernel(x_ref, g_ref, o_ref):
    x = x_ref[0]
    o_ref[0] = (x * lax.rsqrt(jnp.mean(x * x, axis=-1, keepdims=True) + NORM_EPS)) * g_ref[...]


def final_norm(x, g):
    bsz, l, d = x.shape
    tm = 512
    return pl.pallas_call(
        _final_norm_kernel,
        out_shape=jax.ShapeDtypeStruct((bsz, l, d), F32),
        grid=(bsz, l // tm),
        in_specs=[pl.BlockSpec((1, tm, d), lambda b, i: (b, i, 0)), pl.BlockSpec((1, d), lambda b, i: (0, 0))],
        out_specs=pl.BlockSpec((1, tm, d), lambda b, i: (b, i, 0)),
        compiler_params=_cparams(("parallel", "parallel")),
        name="final_norm",
    )(x, g.reshape(1, d))


def _rmsnorm(x, g):
    y = x * lax.rsqrt(jnp.mean(x * x, axis=-1, keepdims=True) + NORM_EPS)
    return y * g


def _to_heads(t, n):
    return t.reshape(t.shape[0], t.shape[1], n, HEAD_DIM)


def _rope_axis(u, pos):
    d = u.shape[-1]
    inv = ROPE_THETA ** (-jnp.arange(0, d, 2, dtype=F32) / d)
    ang = pos.astype(F32)[:, None] * inv[None, :]
    cos = jnp.cos(ang)[None, :, None, :]
    sin = jnp.sin(ang)[None, :, None, :]
    u1, u2 = u[..., : d // 2], u[..., d // 2:]
    return jnp.concatenate([u1 * cos - u2 * sin, u2 * cos + u1 * sin], axis=-1)


def _rope_2d(t, row, col):
    half = t.shape[-1] // 2
    return jnp.concatenate([_rope_axis(t[..., :half], row), _rope_axis(t[..., half:], col)], axis=-1)


def _dwconv_centred(u, w, b):
    pad = w.shape[0] // 2
    y = lax.conv_general_dilated(u, w[:, None, :], window_strides=(1,), padding=[(pad, pad)],
                                 dimension_numbers=('NWC', 'WIO', 'NWC'), feature_group_count=u.shape[-1])
    return y + b


def _segsum(a):
    t = a.shape[-1]
    cs = jnp.cumsum(a, axis=-1)
    d = cs[..., :, None] - cs[..., None, :]
    mask = jnp.tril(jnp.ones((t, t), dtype=bool))
    return jnp.where(mask, d, -jnp.inf)


def _ssd_scan(x, dt, a_coef, bm, cm, h0):
    b, l, h, p = x.shape
    g, n = bm.shape[2], bm.shape[3]
    hg = h // g
    q = SSD_CHUNK
    nc = l // q
    xf = (x * dt[..., None]).reshape(b, nc, q, g, hg, p)
    a = jnp.moveaxis((dt * a_coef).reshape(b, nc, q, g, hg), 2, -1)
    a_cs = jnp.cumsum(a, axis=-1)
    bc = bm.reshape(b, nc, q, g, n)
    cc = cm.reshape(b, nc, q, g, n)
    lmat = jnp.exp(_segsum(a))
    cb = jnp.einsum('bclgn,bcsgn->bcgls', cc, bc)
    y_diag = jnp.einsum('bcgjls,bcsgjp->bclgjp', cb[:, :, :, None] * lmat, xf)
    decay_states = jnp.exp(a_cs[..., -1:] - a_cs)
    states = jnp.einsum('bcsgn,bcgjs,bcsgjp->bcgjpn', bc, decay_states, xf)
    states = jnp.concatenate([h0.reshape(b, 1, g, hg, p, n), states], axis=1)
    chunk_a = jnp.pad(a_cs[..., -1], ((0, 0), (1, 0), (0, 0), (0, 0)))
    decay_chunk = jnp.exp(_segsum(jnp.moveaxis(chunk_a, 1, -1)))
    new_states = jnp.einsum('bgjzc,bcgjpn->bzgjpn', decay_chunk, states)
    prev_states, final = new_states[:, :-1], new_states[:, -1]
    y_off = jnp.einsum('bclgn,bcgjpn,bcgjl->bclgjp', cc, prev_states, jnp.exp(a_cs))
    y = (y_diag + y_off).reshape(b, l, h, p)
    return y, final.reshape(b, h, p, n)


def _ssd_branch(z, xbc, dtf_raw, dtb_raw, p, h0_f, h0_b):
    bsz, l, _ = z.shape
    xbc = jax.nn.silu(_dwconv_centred(xbc, p['ssd_conv_w'], p['ssd_conv_b']))
    xs = xbc[..., :SSD_INNER].reshape(bsz, l, SSD_HEADS, SSD_HEAD_DIM)
    bm = xbc[..., SSD_INNER:SSD_INNER + SSD_GROUPS * SSD_STATE].reshape(bsz, l, SSD_GROUPS, SSD_STATE)
    cm = xbc[..., SSD_INNER + SSD_GROUPS * SSD_STATE:].reshape(bsz, l, SSD_GROUPS, SSD_STATE)
    a_log = p['ssd_a_log']
    dt_bias = p['ssd_dt_bias']
    dt_f = jax.nn.softplus(dtf_raw + dt_bias[0])
    dt_b = jax.nn.softplus(dtb_raw + dt_bias[1])
    y_f, hf = _ssd_scan(xs, dt_f, -jnp.exp(a_log[0]), bm, cm, h0_f)
    rev = lambda t: jnp.flip(t, axis=1)
    y_b, hb = _ssd_scan(rev(xs), rev(dt_b), -jnp.exp(a_log[1]), rev(bm), rev(cm), h0_b)
    y = y_f + rev(y_b) + p['ssd_d'][:, None] * xs
    y = y.reshape(bsz, l, SSD_INNER) * jax.nn.silu(z)
    return _rmsnorm(y, p['ssd_norm_g']), hf, hb


def _swa_latent(q, k, v, kc, vc, sink):
    bsz, s, hq, d = q.shape
    hkv = k.shape[2]
    g = hq // hkv
    blk = SWA_BLOCK
    nb = s // blk
    lc = kc.shape[1]
    scale = d ** -0.5
    qb = q.reshape(bsz, nb, blk, hkv, g, d)

    def band(t):
        tp = jnp.pad(t, ((0, 0), (blk, blk), (0, 0), (0, 0))).reshape(bsz, nb + 2, blk, hkv, d)
        return jnp.concatenate([tp[:, :-2], tp[:, 1:-1], tp[:, 2:]], axis=2)

    kb, vb = band(k), band(v)
    s_loc = jnp.einsum('bnqhgd,bnjhd->bnhgqj', qb, kb) * scale
    qi = jnp.arange(blk)[None, :, None]
    kj = jnp.arange(3 * blk)[None, None, :]
    nidx = jnp.arange(nb)[:, None, None]
    kpos = nidx * blk - blk + kj
    qpos = nidx * blk + qi
    mask = (jnp.abs(kpos - qpos) <= SWA_WINDOW) & (kpos >= 0) & (kpos < s)
    s_loc = jnp.where(mask[None, :, None, None], s_loc, -jnp.inf)
    s_ctx = jnp.einsum('bnqhgd,bchd->bnhgqc', qb, kc) * scale
    s_sink = jnp.broadcast_to(sink.reshape(hkv, g)[None, None, :, :, None, None], s_loc.shape[:-1] + (1,))
    prob = jax.nn.softmax(jnp.concatenate([s_loc, s_ctx, s_sink], axis=-1), axis=-1)
    p_loc = prob[..., : 3 * blk]
    p_ctx = prob[..., 3 * blk: 3 * blk + lc]
    o = jnp.einsum('bnhgqj,bnjhd->bnqhgd', p_loc, vb) + jnp.einsum('bnhgqc,bchd->bnqhgd', p_ctx, vc)
    return o.reshape(bsz, s, hq * d)


def _na_latent(q, k, v, kc, vc, rpb):
    bsz, s, h, d = q.shape
    rows = s // GRID_W
    wr = min(NA_WIN_R, rows)
    wc = NA_WIN_C
    scale = d ** -0.5
    r = jnp.arange(rows)
    cidx = jnp.arange(GRID_W)
    r0 = jnp.clip(r - wr // 2, 0, rows - wr)
    c0 = jnp.clip(cidx - wc // 2, 0, GRID_W - wc)
    krow = r0[:, None] + jnp.arange(wr)[None, :]
    q4 = q.reshape(bsz, rows, GRID_W, h, d)
    k4 = k.reshape(bsz, rows, GRID_W, h, d)[:, krow]
    v4 = v.reshape(bsz, rows, GRID_W, h, d)[:, krow]
    sc = jnp.einsum('brqhd,brikhd->bhrqik', q4, k4) * scale
    roff = krow - r[:, None] + (NA_WIN_R - 1)
    coff = jnp.clip(cidx[None, :] - cidx[:, None] + (NA_WIN_C - 1), 0, 2 * NA_WIN_C - 2)
    cvalid = (cidx[None, :] >= c0[:, None]) & (cidx[None, :] < c0[:, None] + wc)
    bias = rpb[:, roff[:, None, :, None], coff[None, :, None, :]]
    sc = jnp.where(cvalid[:, None, :], sc + bias[None], -jnp.inf)
    s_ctx = jnp.einsum('brqhd,bchd->bhrqc', q4, kc) * scale
    nk = wr * GRID_W
    prob = jax.nn.softmax(jnp.concatenate([sc.reshape(bsz, h, rows, GRID_W, nk), s_ctx], axis=-1), axis=-1)
    p_loc = prob[..., :nk].reshape(bsz, h, rows, GRID_W, wr, GRID_W)
    p_ctx = prob[..., nk:]
    o = jnp.einsum('bhrqik,brikhd->brqhd', p_loc, v4) + jnp.einsum('bhrqc,bchd->brqhd', p_ctx, vc)
    return o.reshape(bsz, s, h * d)


def _ctx_attention(qc, kc, vc, sink=None):
    bsz, lc, hq, d = qc.shape
    hkv = kc.shape[2]
    g = hq // hkv
    qg = qc.reshape(bsz, lc, hkv, g, d)
    s = jnp.einsum('bqhgd,bkhd->bhgqk', qg, kc) * d ** -0.5
    if sink is None:
        prob = jax.nn.softmax(s, axis=-1)
    else:
        s_sink = jnp.broadcast_to(sink.reshape(hkv, g)[None, :, :, None, None], s.shape[:-1] + (1,))
        prob = jax.nn.softmax(jnp.concatenate([s, s_sink], axis=-1), axis=-1)[..., :-1]
    o = jnp.einsum('bhgqk,bkhd->bqhgd', prob, vc)
    return o.reshape(bsz, lc, hq * d)


def _mixers(proj, p, row, col, last):
    lc = CTX_LEN
    seg = lambda off, w: proj[..., off:off + w]
    z, xs_, bc = seg(COL_Z, SSD_INNER), seg(COL_XS, SSD_INNER), seg(COL_BC, 512)
    xbc = jnp.concatenate([xs_, bc], axis=-1)
    dtf, dtb = seg(COL_DT, SSD_HEADS), seg(COL_DT + SSD_HEADS, SSD_HEADS)
    qs, ks, vs = seg(COL_QS, SWA_Q), seg(COL_KS, SWA_KV), seg(COL_VS, SWA_KV)
    qn, kn, vn = seg(COL_QN, NA_DIM), seg(COL_KN, NA_DIM), seg(COL_VN, NA_DIM)
    c_ = lambda t: t[:, :lc]
    l_ = lambda t: t[:, lc:]
    bsz = proj.shape[0]
    h0 = jnp.zeros((bsz, SSD_HEADS, SSD_HEAD_DIM, SSD_STATE), F32)
    ssd_c, hf, hb = _ssd_branch(c_(z), c_(xbc), c_(dtf), c_(dtb), p, h0, h0)
    ssd_l, _, _ = _ssd_branch(l_(z), l_(xbc), l_(dtf), l_(dtb), p, hf, hb)
    ks_c, vs_c = _to_heads(c_(ks), SWA_KV_HEADS), _to_heads(c_(vs), SWA_KV_HEADS)
    swa_l = _swa_latent(_rope_2d(_to_heads(l_(qs), SWA_Q_HEADS), row, col),
                        _rope_2d(_to_heads(l_(ks), SWA_KV_HEADS), row, col),
                        _to_heads(l_(vs), SWA_KV_HEADS), ks_c, vs_c, p['swa_sink'])
    kn_c, vn_c = _to_heads(c_(kn), NA_HEADS), _to_heads(c_(vn), NA_HEADS)
    na_l = _na_latent(_to_heads(l_(qn), NA_HEADS), _to_heads(l_(kn), NA_HEADS), _to_heads(l_(vn), NA_HEADS),
                      kn_c, vn_c, p['na_rpb'])
    if last:
        swa_c = jnp.zeros((bsz, lc, SWA_Q), F32)
        na_c = jnp.zeros((bsz, lc, NA_DIM), F32)
    else:
        swa_c = _ctx_attention(_to_heads(c_(qs), SWA_Q_HEADS), ks_c, vs_c, p['swa_sink'])
        na_c = _ctx_attention(_to_heads(c_(qn), NA_HEADS), kn_c, vn_c)
    cat = lambda a, b: jnp.concatenate([a, b], axis=1)
    return cat(ssd_c, ssd_l), cat(swa_c, swa_l), cat(na_c, na_l)


def _pack_w_in(w):
    o = 0
    segs = {}
    for name, width in (('z', SSD_INNER), ('xs', SSD_INNER), ('bc', 512), ('dt', 2 * SSD_HEADS), ('qs', SWA_Q),
                        ('ks', SWA_KV), ('vs', SWA_KV), ('qn', NA_DIM), ('kn', NA_DIM), ('vn', NA_DIM),
                        ('g', N_BRANCH * D_MODEL)):
        segs[name] = w[:, o:o + width]
        o += width
    pad = jnp.zeros((w.shape[0], 128 - 2 * SSD_HEADS), w.dtype)
    order = ('g', 'z', 'xs', 'bc', 'qs', 'qn', 'kn', 'vn', 'ks', 'vs', 'dt')
    return jnp.concatenate([segs[k] for k in order] + [pad], axis=1).astype(BF16)


def kernel(x, c, ctx, c_ctx, norm1_g, norm2_g, w_ada, b_ada, w_in, b_gate, ssd_conv_w, ssd_conv_b, ssd_a_log,
           ssd_dt_bias, ssd_d, ssd_norm_g, swa_sink, na_rpb, w_br_ssd, w_br_swa, w_br_na, w_out, router_w, router_b,
           moe_w1, moe_b1, moe_w2, moe_b2, final_norm_g):
    bsz, s, d = x.shape
    lc = ctx.shape[1]
    depth = w_in.shape[0]
    pos = jnp.arange(s)
    row, col = pos // GRID_W, pos % GRID_W
    n_mod_rows = 16
    cond = jnp.concatenate([c, c_ctx[None, :], jnp.zeros((n_mod_rows - bsz - 1, d), F32)], axis=0)
    xs = jnp.concatenate([ctx, x], axis=1)
    for i in range(depth):
        last = i == depth - 1
        p = {'ssd_conv_w': ssd_conv_w[i], 'ssd_conv_b': ssd_conv_b[i], 'ssd_a_log': ssd_a_log[i],
             'ssd_dt_bias': ssd_dt_bias[i], 'ssd_d': ssd_d[i], 'ssd_norm_g': ssd_norm_g[i],
             'swa_sink': swa_sink[i], 'na_rpb': na_rpb[i]}
        mod = ada_modulation(cond, w_ada[i], b_ada[i]).reshape(n_mod_rows, 6, d)
        proj = norm_proj(xs, mod, norm1_g[i], _pack_w_in(w_in[i]), lc)
        o_ssd, o_swa, o_na = _mixers(proj, p, row, col, last)
        xs = merge(xs, proj, o_ssd, o_swa, o_na, b_gate[i].reshape(N_BRANCH, d), w_br_ssd[i].astype(BF16),
                   w_br_swa[i].astype(BF16), w_br_na[i].astype(BF16), w_out[i].astype(BF16), mod, lc)
        xs = moe_layer(xs, mod, norm2_g[i], router_w[i], router_b[i], moe_w1[i], moe_b1[i], moe_w2[i], moe_b2[i],
                       lc, lc if last else 0)
    return final_norm(xs, final_norm_g)
```

```python
import functools
import math

import jax
import jax.numpy as jnp
import numpy as np
from jax import lax
from jax.experimental import pallas as pl
from jax.experimental.pallas import tpu as pltpu

F32 = jnp.float32
BF16 = jnp.bfloat16

D_MODEL = 1024
GRID_W = 64
CTX_LEN = 256
HEAD_DIM = 64
NORM_EPS = 1e-6

SSD_HEADS = 16
SSD_HEAD_DIM = 64
SSD_INNER = SSD_HEADS * SSD_HEAD_DIM
SSD_GROUPS = 2
SSD_STATE = 128
SSD_CHUNK = 128

SWA_Q_HEADS = 8
SWA_KV_HEADS = 2
SWA_WINDOW = 128
SWA_BLOCK = 128
SWA_Q = SWA_Q_HEADS * HEAD_DIM
SWA_KV = SWA_KV_HEADS * HEAD_DIM
ROPE_THETA = 10000.0

NA_HEADS = 8
NA_WIN_R = 8
NA_WIN_C = 16
NA_DIM = NA_HEADS * HEAD_DIM

N_BRANCH = 3
N_EXPERTS = 32
TOP_K = 4
SWIGLU_LIMIT = 7.0
SWIGLU_ALPHA = 1.702

COL_GATE = 0
COL_Z = 3 * D_MODEL
COL_XS = COL_Z + SSD_INNER
COL_BC = COL_XS + SSD_INNER
COL_QS = COL_BC + 512
COL_QN = COL_QS + SWA_Q
COL_KN = COL_QN + NA_DIM
COL_VN = COL_KN + NA_DIM
COL_KS = COL_VN + NA_DIM
COL_VS = COL_KS + SWA_KV
COL_DT = COL_VS + SWA_KV
IN_COLS_PACKED = COL_DT + 128

MOE_BLOCK_ROWS = 256
VMEM_LIMIT = 56 * 1024 * 1024


def _cparams(sem):
    return pltpu.CompilerParams(dimension_semantics=sem, vmem_limit_bytes=VMEM_LIMIT)


def _ada_kernel(c_ref, w_ref, b_ref, o_ref):
    c = c_ref[...]
    s = (c * jax.nn.sigmoid(c)).astype(BF16)
    o_ref[...] = jnp.dot(s, w_ref[...].astype(BF16), preferred_element_type=F32) + b_ref[...]


def ada_modulation(rows, w_ada, b_ada):
    r, d = rows.shape
    n = w_ada.shape[1]
    tn = 1024
    return pl.pallas_call(
        _ada_kernel,
        out_shape=jax.ShapeDtypeStruct((r, n), F32),
        grid=(n // tn,),
        in_specs=[pl.BlockSpec((r, d), lambda j: (0, 0)),
                  pl.BlockSpec((d, tn), lambda j: (0, j)),
                  pl.BlockSpec((1, tn), lambda j: (0, j))],
        out_specs=pl.BlockSpec((r, tn), lambda j: (0, j)),
        compiler_params=_cparams(("parallel",)),
        name="ada_modulation",
    )(rows, w_ada, b_ada.reshape(1, n))


def _row_mod(mod_l_ref, mod_c_ref, idx, is_ctx):
    return jnp.where(is_ctx, mod_c_ref[0, idx:idx + 1, :], mod_l_ref[0, idx:idx + 1, :])


def _rms_mod(x, g, scale, shift):
    y = x * lax.rsqrt(jnp.mean(x * x, axis=-1, keepdims=True) + NORM_EPS)
    return (y * g) * (1.0 + scale) + shift


def _norm_proj_kernel(x_ref, ml_ref, mc_ref, g_ref, w_ref, o_ref, h_ref, *, tm, n_ctx):
    i = pl.program_id(1)
    j = pl.program_id(2)

    @pl.when(j == 0)
    def _():
        rows = i * tm + lax.broadcasted_iota(jnp.int32, (tm, 1), 0)
        is_ctx = rows < n_ctx
        h = _rms_mod(x_ref[0], g_ref[...], _row_mod(ml_ref, mc_ref, 1, is_ctx), _row_mod(ml_ref, mc_ref, 0, is_ctx))
        h_ref[...] = h.astype(BF16)

    o_ref[0] = jnp.dot(h_ref[...], w_ref[...], preferred_element_type=F32)


def norm_proj(x, mod, g, w_packed, n_ctx):
    bsz, l, d = x.shape
    n = w_packed.shape[1]
    tm, tn = 768, 1152
    return pl.pallas_call(
        functools.partial(_norm_proj_kernel, tm=tm, n_ctx=n_ctx),
        out_shape=jax.ShapeDtypeStruct((bsz, l, n), F32),
        grid=(bsz, l // tm, n // tn),
        in_specs=[pl.BlockSpec((1, tm, d), lambda b, i, j: (b, i, 0)),
                  pl.BlockSpec((1, 6, d), lambda b, i, j: (b, 0, 0)),
                  pl.BlockSpec((1, 6, d), lambda b, i, j: (bsz, 0, 0)),
                  pl.BlockSpec((1, d), lambda b, i, j: (0, 0)),
                  pl.BlockSpec((d, tn), lambda b, i, j: (0, j))],
        out_specs=pl.BlockSpec((1, tm, tn), lambda b, i, j: (b, i, j)),
        scratch_shapes=[pltpu.VMEM((tm, d), BF16)],
        compiler_params=_cparams(("parallel", "parallel", "arbitrary")),
        name="norm_proj",
    )(x, mod, mod, g.reshape(1, d), w_packed)


def _merge_kernel(x_ref, ga_ref, gb_ref, gn_ref, ossd_ref, oswa_ref, ona_ref, bg_ref, wssd_ref, wswa_ref, wna_ref,
                  wout_ref, ml_ref, mc_ref, o_ref, *, tm, n_ctx, row_off):
    i = pl.program_id(1)
    rows = row_off + i * tm + lax.broadcasted_iota(jnp.int32, (tm, 1), 0)
    is_ctx = rows < n_ctx

    def branch(gate_ref, k, o_br_ref, w_ref):
        gate = jax.nn.sigmoid(gate_ref[0] + bg_ref[k:k + 1, :])
        return gate * jnp.dot(o_br_ref[0].astype(BF16), w_ref[...], preferred_element_type=F32)

    m = branch(ga_ref, 0, ossd_ref, wssd_ref) + branch(gb_ref, 1, oswa_ref, wswa_ref) + branch(gn_ref, 2, ona_ref, wna_ref)
    y = jnp.dot(m.astype(BF16), wout_ref[...], preferred_element_type=F32)
    o_ref[0] = x_ref[0] + _row_mod(ml_ref, mc_ref, 2, is_ctx) * y


def merge(x, proj, o_ssd, o_swa, o_na, b_gate, w_ssd, w_swa, w_na, w_out, mod, n_ctx, row_off):
    bsz, l, d = x.shape
    tm = 256
    gate_blk = COL_GATE // d
    off_blk = row_off // tm

    def row(a):
        o = off_blk if a.shape[1] == l else 0
        return pl.BlockSpec((1, tm, a.shape[-1]), lambda b, i: (b, i + o, 0))

    full = lambda a: pl.BlockSpec(a.shape, lambda b, i: (0,) * a.ndim)
    return pl.pallas_call(
        functools.partial(_merge_kernel, tm=tm, n_ctx=n_ctx, row_off=row_off),
        out_shape=jax.ShapeDtypeStruct((bsz, l - row_off, d), F32),
        grid=(bsz, (l - row_off) // tm),
        in_specs=[row(x),
                  pl.BlockSpec((1, tm, d), lambda b, i: (b, i + off_blk, gate_blk)),
                  pl.BlockSpec((1, tm, d), lambda b, i: (b, i + off_blk, gate_blk + 1)),
                  pl.BlockSpec((1, tm, d), lambda b, i: (b, i + off_blk, gate_blk + 2)),
                  row(o_ssd), row(o_swa), row(o_na),
                  full(b_gate), full(w_ssd), full(w_swa), full(w_na), full(w_out),
                  pl.BlockSpec((1, 6, d), lambda b, i: (b, 0, 0)),
                  pl.BlockSpec((1, 6, d), lambda b, i: (bsz, 0, 0))],
        out_specs=pl.BlockSpec((1, tm, d), lambda b, i: (b, i, 0)),
        compiler_params=_cparams(("parallel", "parallel")),
        name="merge",
    )(x, proj, proj, proj, o_ssd, o_swa, o_na, b_gate, w_ssd, w_swa, w_na, w_out, mod, mod)


def _router_kernel(x_ref, ml_ref, mc_ref, g_ref, rw_ref, rb_ref, hid_ref, idx_ref, gate_ref, *, tm, n_ctx, row_off):
    i = pl.program_id(1)
    rows = row_off + i * tm + lax.broadcasted_iota(jnp.int32, (tm, 1), 0)
    is_ctx = rows < n_ctx
    h = _rms_mod(x_ref[0], g_ref[...], _row_mod(ml_ref, mc_ref, 4, is_ctx), _row_mod(ml_ref, mc_ref, 3, is_ctx))
    hid_ref[0] = h
    logits = jnp.dot(h.astype(BF16), rw_ref[...], preferred_element_type=F32) + rb_ref[...]
    lane = lax.broadcasted_iota(jnp.int32, logits.shape, 1)
    vals, idxs = [], []
    for _ in range(TOP_K):
        m = jnp.max(logits, axis=-1, keepdims=True)
        sel = jnp.min(jnp.where(logits == m, lane, N_EXPERTS), axis=-1, keepdims=True)
        vals.append(m)
        idxs.append(sel)
        logits = jnp.where(lane == sel, -jnp.inf, logits)
    ex = [jnp.exp(v - vals[0]) for v in vals]
    denom = ex[0] + ex[1] + ex[2] + ex[3]
    col = lax.broadcasted_iota(jnp.int32, (tm, TOP_K), 1)
    idx_out = jnp.zeros((tm, TOP_K), jnp.int32)
    gate_out = jnp.zeros((tm, TOP_K), F32)
    for k in range(TOP_K):
        idx_out = jnp.where(col == k, idxs[k], idx_out)
        gate_out = jnp.where(col == k, ex[k] / denom, gate_out)
    idx_ref[0] = idx_out
    gate_ref[0] = gate_out


def moe_router(x, mod, g, router_w, router_b, n_ctx, row_off):
    bsz, l, d = x.shape
    tm = 256
    nrow = (l - row_off) // tm
    off_blk = row_off // tm
    return pl.pallas_call(
        functools.partial(_router_kernel, tm=tm, n_ctx=n_ctx, row_off=row_off),
        out_shape=(jax.ShapeDtypeStruct((bsz, l - row_off, d), F32),
                   jax.ShapeDtypeStruct((bsz, l - row_off, TOP_K), jnp.int32),
                   jax.ShapeDtypeStruct((bsz, l - row_off, TOP_K), F32)),
        grid=(bsz, nrow),
        in_specs=[pl.BlockSpec((1, tm, d), lambda b, i: (b, i + off_blk, 0)),
                  pl.BlockSpec((1, 6, d), lambda b, i: (b, 0, 0)),
                  pl.BlockSpec((1, 6, d), lambda b, i: (bsz, 0, 0)),
                  pl.BlockSpec((1, d), lambda b, i: (0, 0)),
                  pl.BlockSpec((d, N_EXPERTS), lambda b, i: (0, 0)),
                  pl.BlockSpec((1, N_EXPERTS), lambda b, i: (0, 0))],
        out_specs=(pl.BlockSpec((1, tm, d), lambda b, i: (b, i, 0)),
                   pl.BlockSpec((1, tm, TOP_K), lambda b, i: (b, i, 0)),
                   pl.BlockSpec((1, tm, TOP_K), lambda b, i: (b, i, 0))),
        compiler_params=_cparams(("parallel", "parallel")),
        name="moe_router",
    )(x, mod, mod, g.reshape(1, d), router_w.astype(BF16), router_b.reshape(1, N_EXPERTS))


def _expert_kernel(be_ref, nv_ref, nu_ref, idx_hbm, hid_hbm, w1_ref, b1_ref, w2_ref, b2_ref, out_hbm,
                   idx_smem, xbuf, ybuf, idx_sem, in_sem, out_sem):
    i = pl.program_id(0)
    n_used = nu_ref[0]
    bm = xbuf.shape[1]
    n_idx_slots = idx_smem.shape[0]

    def idx_copy(blk):
        s = blk % n_idx_slots
        return pltpu.make_async_copy(idx_hbm.at[blk], idx_smem.at[s], idx_sem.at[s])

    def row_in(blk, r, wait=False):
        tok = 0 if wait else idx_smem[blk % n_idx_slots, r]
        s = blk % 2
        return pltpu.make_async_copy(hid_hbm.at[pl.ds(tok, 1)], xbuf.at[s, pl.ds(r, 1)], in_sem.at[s])

    def row_out(blk, r, wait=False):
        dst = 0 if wait else idx_smem[blk % n_idx_slots, bm + r]
        s = blk % 2
        return pltpu.make_async_copy(ybuf.at[s, pl.ds(r, 1)], out_hbm.at[pl.ds(dst, 1)], out_sem.at[s])

    def for_rows(n, fn):
        def body(r, carry):
            fn(r)
            return carry
        lax.fori_loop(0, n, body, 0)

    @pl.when(i == 0)
    def _():
        idx_copy(0).start()
        idx_copy(0).wait()
        for_rows(bm, lambda r: row_in(0, r).start())

        @pl.when(n_used > 1)
        def _():
            idx_copy(1).start()

    @pl.when(i + 1 < n_used)
    def _():
        idx_copy(i + 1).wait()
        for_rows(bm, lambda r: row_in(i + 1, r).start())

        @pl.when(i + 2 < n_used)
        def _():
            idx_copy(i + 2).start()

    @pl.when(i < n_used)
    def _():
        s = i % 2
        for_rows(bm, lambda r: row_in(i, r, wait=True).wait())
        de = w2_ref.shape[1]
        gu = jnp.dot(xbuf[s].astype(BF16), w1_ref[0], preferred_element_type=F32) + b1_ref[0]
        gate = jnp.minimum(gu[:, :de], SWIGLU_LIMIT)
        up = jnp.clip(gu[:, de:], -SWIGLU_LIMIT, SWIGLU_LIMIT)
        act = (up + 1.0) * gate * jax.nn.sigmoid(SWIGLU_ALPHA * gate)
        ybuf[s] = jnp.dot(act.astype(BF16), w2_ref[0], preferred_element_type=F32) + b2_ref[0]

    @pl.when(jnp.logical_and(i >= 1, i <= n_used))
    def _():
        for_rows(nv_ref[i - 1], lambda r: row_out(i - 1, r, wait=True).wait())

    @pl.when(i < n_used)
    def _():
        for_rows(nv_ref[i], lambda r: row_out(i, r).start())


def moe_experts(hid, idx, block_e, n_valid, n_used, w1, b1, w2, b2):
    t, d = hid.shape
    n_blocks = idx.shape[0]
    bm = idx.shape[1] // 2
    n_exp, _, two_de = w1.shape
    de = two_de // 2
    any_spec = pl.BlockSpec(memory_space=pl.ANY)
    return pl.pallas_call(
        _expert_kernel,
        out_shape=jax.ShapeDtypeStruct((t * TOP_K, d), F32),
        grid_spec=pltpu.PrefetchScalarGridSpec(
            num_scalar_prefetch=3,
            grid=(n_blocks,),
            in_specs=[any_spec, any_spec,
                      pl.BlockSpec((1, d, two_de), lambda i, be, nv, nu: (be[i], 0, 0)),
                      pl.BlockSpec((1, 1, two_de), lambda i, be, nv, nu: (be[i], 0, 0)),
                      pl.BlockSpec((1, de, d), lambda i, be, nv, nu: (be[i], 0, 0)),
                      pl.BlockSpec((1, 1, d), lambda i, be, nv, nu: (be[i], 0, 0))],
            out_specs=any_spec,
            scratch_shapes=[pltpu.SMEM((3, 2 * bm), jnp.int32),
                            pltpu.VMEM((2, bm, d), F32),
                            pltpu.VMEM((2, bm, d), F32),
                            pltpu.SemaphoreType.DMA((3,)),
                            pltpu.SemaphoreType.DMA((2,)),
                            pltpu.SemaphoreType.DMA((2,))]),
        compiler_params=_cparams(("arbitrary",)),
        name="moe_experts",
    )(block_e, n_valid, n_used, idx, hid, w1, b1.reshape(n_exp, 1, two_de), w2, b2.reshape(n_exp, 1, d))


def _combine_kernel(x_ref, y_ref, gate_ref, ml_ref, mc_ref, o_ref, *, tm, n_ctx, row_off):
    i = pl.program_id(1)
    rows = row_off + i * tm + lax.broadcasted_iota(jnp.int32, (tm, 1), 0)
    is_ctx = rows < n_ctx
    gates = gate_ref[0]
    y = y_ref[0, :, 0, :] * gates[:, 0:1]
    for k in range(1, TOP_K):
        y = y + y_ref[0, :, k, :] * gates[:, k:k + 1]
    o_ref[0] = x_ref[0] + _row_mod(ml_ref, mc_ref, 5, is_ctx) * y


def moe_combine(x, yk, gates, mod, n_ctx, row_off):
    bsz, l, d = x.shape
    tm = 128
    nrow = (l - row_off) // tm
    off_blk = row_off // tm
    return pl.pallas_call(
        functools.partial(_combine_kernel, tm=tm, n_ctx=n_ctx, row_off=row_off),
        out_shape=jax.ShapeDtypeStruct((bsz, l - row_off, d), F32),
        grid=(bsz, nrow),
        in_specs=[pl.BlockSpec((1, tm, d), lambda b, i: (b, i + off_blk, 0)),
                  pl.BlockSpec((1, tm, TOP_K, d), lambda b, i: (b, i, 0, 0)),
                  pl.BlockSpec((1, tm, TOP_K), lambda b, i: (b, i, 0)),
                  pl.BlockSpec((1, 6, d), lambda b, i: (b, 0, 0)),
                  pl.BlockSpec((1, 6, d), lambda b, i: (bsz, 0, 0))],
        out_specs=pl.BlockSpec((1, tm, d), lambda b, i: (b, i, 0)),
        compiler_params=_cparams(("parallel", "parallel")),
        name="moe_combine",
    )(x, yk, gates, mod, mod)


def moe_layer(x, mod, g, router_w, router_b, w1, b1, w2, b2, n_ctx, row_off):
    bsz, l, d = x.shape
    hid, top_i, gates = moe_router(x, mod, g, router_w, router_b, n_ctx, row_off)
    t = bsz * (l - row_off)
    bm = MOE_BLOCK_ROWS
    n_blocks = (t * TOP_K + bm - 1) // bm + N_EXPERTS
    flat_e = top_i.reshape(-1)
    order = jnp.argsort(flat_e).astype(jnp.int32)
    counts = jnp.sum((flat_e[:, None] == jnp.arange(N_EXPERTS)[None, :]).astype(jnp.int32), axis=0)
    starts = jnp.cumsum(counts) - counts
    padded = (counts + bm - 1) // bm * bm
    pends = jnp.cumsum(padded)
    pstarts = pends - padded
    blk_first = jnp.arange(n_blocks, dtype=jnp.int32) * bm
    block_e = jnp.clip(jnp.searchsorted(pends, blk_first, side='right'), 0, N_EXPERTS - 1).astype(jnp.int32)
    off = blk_first - pstarts[block_e]
    n_valid = jnp.clip(counts[block_e] - off, 0, bm).astype(jnp.int32)
    n_used = (pends[-1] // bm).astype(jnp.int32).reshape(1)
    r = jnp.arange(bm, dtype=jnp.int32)[None, :]
    pos = jnp.where(r < n_valid[:, None], (starts[block_e] + off)[:, None] + r, 0)
    src = order[pos]
    idx = jnp.concatenate([src // TOP_K, src], axis=1).astype(jnp.int32)
    ys = moe_experts(hid.reshape(t, d), idx, block_e, n_valid, n_used, w1.astype(BF16), b1, w2.astype(BF16), b2)
    return moe_combine(x, ys.reshape(bsz, l - row_off, TOP_K, d), gates, mod, n_ctx, row_off)


def _final_norm_kernel(x_ref, g_ref, o_ref):
    x = x_ref[0]
    o_ref[0] = (x * lax.rsqrt(jnp.mean(x * x, axis=-1, keepdims=True) + NORM_EPS)) * g_ref[...]


def final_norm(x, g):
    bsz, l, d = x.shape
    tm = 512
    return pl.pallas_call(
        _final_norm_kernel,
        out_shape=jax.ShapeDtypeStruct((bsz, l, d), F32),
        grid=(bsz, l // tm),
        in_specs=[pl.BlockSpec((1, tm, d), lambda b, i: (b, i, 0)), pl.BlockSpec((1, d), lambda b, i: (0, 0))],
        out_specs=pl.BlockSpec((1, tm, d), lambda b, i: (b, i, 0)),
        compiler_params=_cparams(("parallel", "parallel")),
        name="final_norm",
    )(x, g.reshape(1, d))


ATT_BLOCK = 128
NEG_BIG = -1e30


def _qk(q, k):
    return lax.dot_general(q, k, (((1,), (1,)), ((), ())), preferred_element_type=F32)


def _softmax_pv(scores, values, extra_logit=None):
    m = functools.reduce(jnp.maximum, [jnp.max(s, axis=-1, keepdims=True) for s in scores])
    if extra_logit is not None:
        m = jnp.maximum(m, extra_logit)
    ps = [jnp.exp(s - m) for s in scores]
    den = functools.reduce(jnp.add, [jnp.sum(p, axis=-1, keepdims=True) for p in ps])
    if extra_logit is not None:
        den = den + jnp.exp(extra_logit - m)
    acc = functools.reduce(jnp.add, [jnp.dot(p.astype(BF16), v, preferred_element_type=F32) for p, v in zip(ps, values)])
    return acc / den


def _na_kernel(q_ref, k_ref, v_ref, bias_ref, o_ref, *, n_ctx, blk_off, n_dbl):
    blk = pl.program_id(1) + blk_off
    n_ctx_blk = n_ctx // ATT_BLOCK
    s2 = jnp.clip(blk - n_ctx_blk - 2, 0, n_dbl - 5)
    start = pl.multiple_of(n_ctx + s2 * ATT_BLOCK, ATT_BLOCK)
    lane = lax.broadcasted_iota(jnp.int32, (1, 2 * HEAD_DIM), 1)
    scale = HEAD_DIM ** -0.5

    def pair_out(p, local):
        cs = slice(p * 2 * HEAD_DIM, (p + 1) * 2 * HEAD_DIM)
        q2 = q_ref[0, :, cs] * scale
        kc = k_ref[0, 0:n_ctx, cs].astype(BF16)
        vc = v_ref[0, 0:n_ctx, cs].astype(BF16)
        if local:
            kl = k_ref[0, pl.ds(start, 5 * ATT_BLOCK), cs].astype(BF16)
            vl = v_ref[0, pl.ds(start, 5 * ATT_BLOCK), cs].astype(BF16)
        outs = []
        for hh in range(2):
            qh = jnp.where((lane < HEAD_DIM) == (hh == 0), q2, 0.0).astype(BF16)
            scores, values = [_qk(qh, kc)], [vc]
            if local:
                scores.append(_qk(qh, kl) + bias_ref[0, 2 * p + hh])
                values.append(vl)
            outs.append(_softmax_pv(scores, values))
        o_ref[0, :, cs] = jnp.where(lane < HEAD_DIM, outs[0], outs[1])

    @pl.when(blk < n_ctx_blk)
    def _():
        for p in range(NA_HEADS // 2):
            pair_out(p, False)

    @pl.when(blk >= n_ctx_blk)
    def _():
        for p in range(NA_HEADS // 2):
            pair_out(p, True)


def _na_bias_tables(rpb, rows):
    n_dbl = rows // 2
    wr = min(NA_WIN_R, rows)
    cidx = np.arange(GRID_W)
    c0 = np.clip(cidx - NA_WIN_C // 2, 0, GRID_W - NA_WIN_C)
    cvalid = (cidx[None, :] >= c0[:, None]) & (cidx[None, :] < c0[:, None] + NA_WIN_C)
    coff = np.clip(cidx[None, :] - cidx[:, None] + (NA_WIN_C - 1), 0, 2 * NA_WIN_C - 2)
    col_onehot = (coff[:, :, None] == np.arange(2 * NA_WIN_C - 1)[None, None, :]) & cvalid[:, :, None]
    pats = (0, 1, 2, n_dbl - 2, n_dbl - 1)
    row_onehot = np.zeros((len(pats), 2, 10, 2 * NA_WIN_R - 1), np.float32)
    for pi, r2 in enumerate(pats):
        s2 = int(np.clip(r2 - 2, 0, n_dbl - 5))
        for qh in range(2):
            qr = 2 * r2 + qh
            r0 = int(np.clip(qr - wr // 2, 0, rows - wr))
            for kk in range(10):
                kr = 2 * s2 + kk
                if r0 <= kr < r0 + wr:
                    row_onehot[pi, qh, kk, kr - qr + NA_WIN_R - 1] = 1.0
    t1 = jnp.einsum('hrc,qkc->hrqk', rpb, jnp.asarray(col_onehot, F32), precision=lax.Precision.HIGHEST)
    bias = jnp.einsum('pajr,hrqk->phaqjk', jnp.asarray(row_onehot), t1, precision=lax.Precision.HIGHEST)
    valid = (row_onehot.sum(-1)[:, None, :, None, :, None] > 0) & cvalid[None, None, None, :, None, :]
    bias = jnp.where(jnp.asarray(valid), bias, NEG_BIG)
    return bias.reshape(len(pats), rpb.shape[0], 2 * GRID_W, 10 * GRID_W)


def na_attention(proj, rpb, n_ctx, blk_off):
    bsz, l, _ = proj.shape
    rows = (l - n_ctx) // GRID_W
    n_dbl = rows // 2
    n_ctx_blk = n_ctx // ATT_BLOCK
    bias = _na_bias_tables(rpb, rows)

    def bias_idx(b, i):
        r2 = i + blk_off - n_ctx_blk
        pat = jnp.where(r2 < 2, r2, jnp.where(r2 >= n_dbl - 2, r2 - (n_dbl - 5), 2))
        return (jnp.clip(pat, 0, 4), 0, 0, 0)

    return pl.pallas_call(
        functools.partial(_na_kernel, n_ctx=n_ctx, blk_off=blk_off, n_dbl=n_dbl),
        out_shape=jax.ShapeDtypeStruct((bsz, l - blk_off * ATT_BLOCK, NA_DIM), F32),
        grid=(bsz, l // ATT_BLOCK - blk_off),
        in_specs=[pl.BlockSpec((1, ATT_BLOCK, NA_DIM), lambda b, i: (b, i + blk_off, COL_QN // NA_DIM)),
                  pl.BlockSpec((1, l, NA_DIM), lambda b, i: (b, 0, COL_KN // NA_DIM)),
                  pl.BlockSpec((1, l, NA_DIM), lambda b, i: (b, 0, COL_VN // NA_DIM)),
                  pl.BlockSpec((1, NA_HEADS, ATT_BLOCK, 5 * ATT_BLOCK), bias_idx)],
        out_specs=pl.BlockSpec((1, ATT_BLOCK, NA_DIM), lambda b, i: (b, i, 0)),
        compiler_params=_cparams(("parallel", "arbitrary")),
        name="na_attention",
    )(proj, proj, proj, bias)


def _swa_kernel(sink_ref, q_ref, k_ref, v_ref, cq_ref, sq_ref, ck_ref, sk_ref, o_ref, *, n_ctx, blk_off, n_blk):
    blk = pl.program_id(1) + blk_off
    n_ctx_blk = n_ctx // ATT_BLOCK
    n = blk - n_ctx_blk
    first = jnp.clip(n - 1, 0, n_blk - 3)
    kstart = pl.multiple_of(first * ATT_BLOCK, ATT_BLOCK)
    lane = lax.broadcasted_iota(jnp.int32, (1, 2 * HEAD_DIM), 1)
    scale = HEAD_DIM ** -0.5
    quarter = HEAD_DIM // 4

    def rope(u, cos, sin):
        w = u.shape[-1]
        li = lax.broadcasted_iota(jnp.int32, (1, w), 1)
        swapped = jnp.where(li % (2 * quarter) < quarter, pltpu.roll(u, w - quarter, 1), pltpu.roll(u, quarter, 1))
        return u * cos + swapped * sin

    def dup_head(x, h):
        other = pltpu.roll(x, HEAD_DIM, 1)
        return jnp.where((lane < HEAD_DIM) == (h == 0), x, other)

    def run(local):
        q = q_ref[0] * scale
        kc_all = k_ref[0, 0:n_ctx, :]
        vc_all = v_ref[0, 0:n_ctx, :]
        if local:
            q = rope(q, cq_ref[...], sq_ref[...])
            kl_all = rope(k_ref[0, pl.ds(n_ctx + kstart, 3 * ATT_BLOCK), :],
                          ck_ref[pl.ds(kstart, 3 * ATT_BLOCK), :], sk_ref[pl.ds(kstart, 3 * ATT_BLOCK), :])
            vl_all = v_ref[0, pl.ds(n_ctx + kstart, 3 * ATT_BLOCK), :]
            qpos = n * ATT_BLOCK + lax.broadcasted_iota(jnp.int32, (ATT_BLOCK, 3 * ATT_BLOCK), 0)
            kpos = kstart + lax.broadcasted_iota(jnp.int32, (ATT_BLOCK, 3 * ATT_BLOCK), 1)
            band = jnp.where(jnp.abs(kpos - qpos) <= SWA_WINDOW, 0.0, NEG_BIG)
        group = SWA_Q_HEADS // SWA_KV_HEADS
        for h in range(SWA_KV_HEADS):
            kc = dup_head(kc_all, h).astype(BF16)
            vc = dup_head(vc_all, h).astype(BF16)
            if local:
                kl = dup_head(kl_all, h).astype(BF16)
                vl = dup_head(vl_all, h).astype(BF16)
            for p in range(h * group // 2, (h + 1) * group // 2):
                cs = slice(p * 2 * HEAD_DIM, (p + 1) * 2 * HEAD_DIM)
                q2 = q[:, cs]
                outs = []
                for hh in range(2):
                    qh = jnp.where((lane < HEAD_DIM) == (hh == 0), q2, 0.0).astype(BF16)
                    scores, values = [_qk(qh, kc)], [vc]
                    if local:
                        scores.append(_qk(qh, kl) + band)
                        values.append(vl)
                    outs.append(_softmax_pv(scores, values, extra_logit=sink_ref[2 * p + hh]))
                o_ref[0, :, cs] = jnp.where(lane < HEAD_DIM, outs[0], outs[1])

    @pl.when(blk < n_ctx_blk)
    def _():
        run(False)

    @pl.when(blk >= n_ctx_blk)
    def _():
        run(True)


def _rope_tables(s):
    pos = np.arange(s)
    quarter = HEAD_DIM // 4
    inv = ROPE_THETA ** (-jnp.arange(0, 2 * quarter, 2, dtype=F32) / (2 * quarter))

    def axis_tables(p):
        ang = jnp.asarray(p, F32)[:, None] * inv[None, :]
        c, sn = jnp.cos(ang), jnp.sin(ang)
        return jnp.concatenate([c, c], axis=1), jnp.concatenate([-sn, sn], axis=1)
    cr, sr = axis_tables(pos // GRID_W)
    cc, sc = axis_tables(pos % GRID_W)
    return jnp.concatenate([cr, cc], axis=1), jnp.concatenate([sr, sc], axis=1)


def swa_attention(proj, sink, n_ctx, blk_off):
    bsz, l, _ = proj.shape
    s = l - n_ctx
    n_blk = s // ATT_BLOCK
    n_ctx_blk = n_ctx // ATT_BLOCK
    cos, sin = _rope_tables(s)
    cq, sq = jnp.tile(cos, (1, SWA_Q_HEADS)), jnp.tile(sin, (1, SWA_Q_HEADS))
    ck, sk = jnp.tile(cos, (1, SWA_KV_HEADS)), jnp.tile(sin, (1, SWA_KV_HEADS))
    qtab = lambda b, i: (jnp.maximum(i + blk_off - n_ctx_blk, 0), 0)
    return pl.pallas_call(
        functools.partial(_swa_kernel, n_ctx=n_ctx, blk_off=blk_off, n_blk=n_blk),
        out_shape=jax.ShapeDtypeStruct((bsz, l - blk_off * ATT_BLOCK, SWA_Q), F32),
        grid=(bsz, l // ATT_BLOCK - blk_off),
        in_specs=[pl.BlockSpec(memory_space=pltpu.SMEM),
                  pl.BlockSpec((1, ATT_BLOCK, SWA_Q), lambda b, i: (b, i + blk_off, COL_QS // SWA_Q)),
                  pl.BlockSpec((1, l, SWA_KV), lambda b, i: (b, 0, COL_KS // SWA_KV)),
                  pl.BlockSpec((1, l, SWA_KV), lambda b, i: (b, 0, COL_VS // SWA_KV)),
                  pl.BlockSpec((ATT_BLOCK, SWA_Q), qtab),
                  pl.BlockSpec((ATT_BLOCK, SWA_Q), qtab),
                  pl.BlockSpec((s, SWA_KV), lambda b, i: (0, 0)),
                  pl.BlockSpec((s, SWA_KV), lambda b, i: (0, 0))],
        out_specs=pl.BlockSpec((1, ATT_BLOCK, SWA_Q), lambda b, i: (b, i, 0)),
        compiler_params=_cparams(("parallel", "arbitrary")),
        name="swa_attention",
    )(sink, proj, proj, proj, cq, sq, ck, sk)


CONV_HALO = 8


def _ssd_kernel(*refs, reverse, n_ctx, n_chunks, final):
    if final:
        (xs_c, xs_p, xs_n, bc_c, bc_p, bc_n, dt_ref, cwx_ref, cbx_ref, cwb_ref, cbb_ref, dtbias_ref, acoef_ref,
         z_ref, yf_ref, dskip_ref, ng_ref, o_ref, state_ref) = refs
    else:
        (xs_c, xs_p, xs_n, bc_c, bc_p, bc_n, dt_ref, cwx_ref, cbx_ref, cwb_ref, cbb_ref, dtbias_ref, acoef_ref,
         o_ref, state_ref) = refs
    q = SSD_CHUNK
    n_st = SSD_STATE
    gw = SSD_INNER // SSD_GROUPS
    j = pl.program_id(1)
    c = _ssd_chunk_of_step(j, reverse, n_ctx // q, n_chunks)
    n_cc = n_ctx // q
    seg_first = jnp.logical_or(c == 0, c == n_cc)
    seg_last = jnp.logical_or(c == n_cc - 1, c == n_chunks - 1)

    @pl.when(j == 0)
    def _():
        state_ref[...] = jnp.zeros_like(state_ref)

    def conv_silu(cur_ref, prev_ref, next_ref, w_ref, b_ref):
        half = w_ref.shape[0] // 2
        prev = jnp.where(seg_first, 0.0, prev_ref[0])
        nxt = jnp.where(seg_last, 0.0, next_ref[0])
        ext = jnp.concatenate([prev, cur_ref[0], nxt], axis=0)
        acc = b_ref[...]
        for k in range(w_ref.shape[0]):
            o = CONV_HALO - half + k
            acc = acc + ext[o:o + q] * w_ref[k:k + 1, :]
        return acc * jax.nn.sigmoid(acc)

    xs = conv_silu(xs_c, xs_p, xs_n, cwx_ref, cbx_ref)
    bcv = conv_silu(bc_c, bc_p, bc_n, cwb_ref, cbb_ref)

    lo = SSD_HEADS if reverse else 0
    dt_raw = dt_ref[0] + dtbias_ref[...]
    dt_all = jnp.maximum(dt_raw, 0.0) + jnp.log(1.0 + jnp.exp(-jnp.abs(dt_raw)))
    a_all = dt_all * acoef_ref[...]
    ri = lax.broadcasted_iota(jnp.int32, (q, q), 0)
    ci = lax.broadcasted_iota(jnp.int32, (q, q), 1)
    causal = (ci >= ri) if reverse else (ci <= ri)
    a_cs = jnp.dot(causal.astype(F32), a_all, precision=lax.Precision.HIGHEST, preferred_element_type=F32)
    total = a_cs[0:1] if reverse else a_cs[q - 1:q]
    a_cs_t = a_cs.T

    lane = lax.broadcasted_iota(jnp.int32, (1, 2 * SSD_HEAD_DIM), 1)

    def expand(v):
        parts = [jnp.where(lane < SSD_HEAD_DIM, v[:, lo + 2 * p:lo + 2 * p + 1], v[:, lo + 2 * p + 1:lo + 2 * p + 2])
                 for p in range(SSD_HEADS // 2)]
        return jnp.concatenate(parts, axis=1)

    xdt = xs * expand(dt_all)
    dec_start = expand(jnp.exp(a_cs))
    dec_end = expand(jnp.exp(total - a_cs))
    dec_total = expand(jnp.exp(total))

    heads_per_group = SSD_HEADS // SSD_GROUPS
    y_parts = []
    for g in range(SSD_GROUPS):
        b_g = bcv[:, g * n_st:(g + 1) * n_st]
        c_g = bcv[:, (SSD_GROUPS + g) * n_st:(SSD_GROUPS + g + 1) * n_st].astype(BF16)
        cb = _qk(c_g, b_g.astype(BF16))
        cols = slice(g * gw, (g + 1) * gw)
        st = state_ref[:, cols]
        y_off = jnp.dot(c_g, st.astype(BF16), preferred_element_type=F32) * dec_start[:, cols]
        diag = []
        for p in range(g * heads_per_group // 2, (g + 1) * heads_per_group // 2):
            x2 = xdt[:, p * 2 * SSD_HEAD_DIM:(p + 1) * 2 * SSD_HEAD_DIM].astype(BF16)
            outs = []
            for hh in range(2):
                h = lo + 2 * p + hh
                seg = jnp.where(causal, a_cs[:, h:h + 1] - a_cs_t[h:h + 1, :], NEG_BIG)
                m = (cb * jnp.exp(seg)).astype(BF16)
                outs.append(jnp.dot(m, x2, preferred_element_type=F32))
            diag.append(jnp.where(lane < SSD_HEAD_DIM, outs[0], outs[1]))
        y_parts.append(jnp.concatenate(diag, axis=1) + y_off)
        xw = (xdt[:, cols] * dec_end[:, cols]).astype(BF16)
        state_ref[:, cols] = dec_total[:, cols] * st + jnp.dot(b_g.T.astype(BF16), xw, preferred_element_type=F32)
    y = jnp.concatenate(y_parts, axis=1)

    if final:
        y = yf_ref[0] + y + dskip_ref[...] * xs
        z = z_ref[0]
        y = y * (z * jax.nn.sigmoid(z))
        o_ref[0] = (y * lax.rsqrt(jnp.mean(y * y, axis=-1, keepdims=True) + NORM_EPS)) * ng_ref[...]
    else:
        o_ref[0] = y


def _ssd_chunk_of_step(j, reverse, n_ctx_chunks, n_chunks):
    if not reverse:
        return j
    return jnp.where(j < n_ctx_chunks, n_ctx_chunks - 1 - j, n_chunks - 1 - (j - n_ctx_chunks))


def ssd_mixer(proj, conv_w, conv_b, a_log, dt_bias, d_skip, norm_g, n_ctx):
    bsz, l, _ = proj.shape
    q = SSD_CHUNK
    n_chunks = l // q
    n_cc = n_ctx // q
    hb = q // CONV_HALO
    pad = lambda v: jnp.concatenate([v.reshape(1, -1), jnp.zeros((1, 128 - v.size), F32)], axis=1)
    dtb = pad(dt_bias)
    acoef = pad(-jnp.exp(a_log))
    cwx, cwb = conv_w[:, :SSD_INNER], conv_w[:, SSD_INNER:]
    cbx, cbb = conv_b[:SSD_INNER].reshape(1, -1), conv_b[SSD_INNER:].reshape(1, -1)
    wbc = cwb.shape[1]

    def call(reverse, extra_in, extra_specs, final):
        chunk = lambda jj: _ssd_chunk_of_step(jj, reverse, n_cc, n_chunks)
        cur = lambda w, off: pl.BlockSpec((1, q, w), lambda b, jj: (b, chunk(jj), off // w))
        prev = lambda w, off: pl.BlockSpec((1, CONV_HALO, w), lambda b, jj: (b, jnp.maximum(chunk(jj) * hb - 1, 0), off // w))
        nxt = lambda w, off: pl.BlockSpec((1, CONV_HALO, w),
                                          lambda b, jj: (b, jnp.minimum((chunk(jj) + 1) * hb, l // CONV_HALO - 1), off // w))
        full = lambda a: pl.BlockSpec(a.shape, lambda b, jj: (0, 0))
        return pl.pallas_call(
            functools.partial(_ssd_kernel, reverse=reverse, n_ctx=n_ctx, n_chunks=n_chunks, final=final),
            out_shape=jax.ShapeDtypeStruct((bsz, l, SSD_INNER), F32),
            grid=(bsz, n_chunks),
            in_specs=[cur(SSD_INNER, COL_XS), prev(SSD_INNER, COL_XS), nxt(SSD_INNER, COL_XS),
                      cur(wbc, COL_BC), prev(wbc, COL_BC), nxt(wbc, COL_BC),
                      cur(128, COL_DT), full(cwx), full(cbx), full(cwb), full(cbb), full(dtb), full(acoef)]
                     + extra_specs(cur, full),
            out_specs=pl.BlockSpec((1, q, SSD_INNER), lambda b, jj: (b, chunk(jj), 0)),
            scratch_shapes=[pltpu.VMEM((SSD_STATE, SSD_INNER), F32)],
            compiler_params=_cparams(("parallel", "arbitrary")),
            name="ssd_bwd" if reverse else "ssd_fwd",
        )(proj, proj, proj, proj, proj, proj, proj, cwx, cbx, cwb, cbb, dtb, acoef, *extra_in)

    y_f = call(False, (), lambda cur, full: [], False)
    dsk = jnp.repeat(d_skip, SSD_HEAD_DIM).reshape(1, SSD_INNER)
    ng = norm_g.reshape(1, SSD_INNER)
    return call(True, (proj, y_f, dsk, ng),
                lambda cur, full: [cur(SSD_INNER, COL_Z), cur(SSD_INNER, 0), full(dsk), full(ng)], True)


def _rmsnorm(x, g):
    y = x * lax.rsqrt(jnp.mean(x * x, axis=-1, keepdims=True) + NORM_EPS)
    return y * g


def _to_heads(t, n):
    return t.reshape(t.shape[0], t.shape[1], n, HEAD_DIM)


def _rope_axis(u, pos):
    d = u.shape[-1]
    inv = ROPE_THETA ** (-jnp.arange(0, d, 2, dtype=F32) / d)
    ang = pos.astype(F32)[:, None] * inv[None, :]
    cos = jnp.cos(ang)[None, :, None, :]
    sin = jnp.sin(ang)[None, :, None, :]
    u1, u2 = u[..., : d // 2], u[..., d // 2:]
    return jnp.concatenate([u1 * cos - u2 * sin, u2 * cos + u1 * sin], axis=-1)


def _rope_2d(t, row, col):
    half = t.shape[-1] // 2
    return jnp.concatenate([_rope_axis(t[..., :half], row), _rope_axis(t[..., half:], col)], axis=-1)


def _dwconv_centred(u, w, b):
    pad = w.shape[0] // 2
    y = lax.conv_general_dilated(u, w[:, None, :], window_strides=(1,), padding=[(pad, pad)],
                                 dimension_numbers=('NWC', 'WIO', 'NWC'), feature_group_count=u.shape[-1])
    return y + b


def _segsum(a):
    t = a.shape[-1]
    cs = jnp.cumsum(a, axis=-1)
    d = cs[..., :, None] - cs[..., None, :]
    mask = jnp.tril(jnp.ones((t, t), dtype=bool))
    return jnp.where(mask, d, -jnp.inf)


def _ssd_scan(x, dt, a_coef, bm, cm, h0):
    b, l, h, p = x.shape
    g, n = bm.shape[2], bm.shape[3]
    hg = h // g
    q = SSD_CHUNK
    nc = l // q
    xf = (x * dt[..., None]).reshape(b, nc, q, g, hg, p)
    a = jnp.moveaxis((dt * a_coef).reshape(b, nc, q, g, hg), 2, -1)
    a_cs = jnp.cumsum(a, axis=-1)
    bc = bm.reshape(b, nc, q, g, n)
    cc = cm.reshape(b, nc, q, g, n)
    lmat = jnp.exp(_segsum(a))
    cb = jnp.einsum('bclgn,bcsgn->bcgls', cc, bc)
    y_diag = jnp.einsum('bcgjls,bcsgjp->bclgjp', cb[:, :, :, None] * lmat, xf)
    decay_states = jnp.exp(a_cs[..., -1:] - a_cs)
    states = jnp.einsum('bcsgn,bcgjs,bcsgjp->bcgjpn', bc, decay_states, xf)
    states = jnp.concatenate([h0.reshape(b, 1, g, hg, p, n), states], axis=1)
    chunk_a = jnp.pad(a_cs[..., -1], ((0, 0), (1, 0), (0, 0), (0, 0)))
    decay_chunk = jnp.exp(_segsum(jnp.moveaxis(chunk_a, 1, -1)))
    new_states = jnp.einsum('bgjzc,bcgjpn->bzgjpn', decay_chunk, states)
    prev_states, final = new_states[:, :-1], new_states[:, -1]
    y_off = jnp.einsum('bclgn,bcgjpn,bcgjl->bclgjp', cc, prev_states, jnp.exp(a_cs))
    y = (y_diag + y_off).reshape(b, l, h, p)
    return y, final.reshape(b, h, p, n)


def _ssd_branch(z, xbc, dtf_raw, dtb_raw, p, h0_f, h0_b):
    bsz, l, _ = z.shape
    xbc = jax.nn.silu(_dwconv_centred(xbc, p['ssd_conv_w'], p['ssd_conv_b']))
    xs = xbc[..., :SSD_INNER].reshape(bsz, l, SSD_HEADS, SSD_HEAD_DIM)
    bm = xbc[..., SSD_INNER:SSD_INNER + SSD_GROUPS * SSD_STATE].reshape(bsz, l, SSD_GROUPS, SSD_STATE)
    cm = xbc[..., SSD_INNER + SSD_GROUPS * SSD_STATE:].reshape(bsz, l, SSD_GROUPS, SSD_STATE)
    a_log = p['ssd_a_log']
    dt_bias = p['ssd_dt_bias']
    dt_f = jax.nn.softplus(dtf_raw + dt_bias[0])
    dt_b = jax.nn.softplus(dtb_raw + dt_bias[1])
    y_f, hf = _ssd_scan(xs, dt_f, -jnp.exp(a_log[0]), bm, cm, h0_f)
    rev = lambda t: jnp.flip(t, axis=1)
    y_b, hb = _ssd_scan(rev(xs), rev(dt_b), -jnp.exp(a_log[1]), rev(bm), rev(cm), h0_b)
    y = y_f + rev(y_b) + p['ssd_d'][:, None] * xs
    y = y.reshape(bsz, l, SSD_INNER) * jax.nn.silu(z)
    return _rmsnorm(y, p['ssd_norm_g']), hf, hb


def _swa_latent(q, k, v, kc, vc, sink):
    bsz, s, hq, d = q.shape
    hkv = k.shape[2]
    g = hq // hkv
    blk = SWA_BLOCK
    nb = s // blk
    lc = kc.shape[1]
    scale = d ** -0.5
    qb = q.reshape(bsz, nb, blk, hkv, g, d)

    def band(t):
        tp = jnp.pad(t, ((0, 0), (blk, blk), (0, 0), (0, 0))).reshape(bsz, nb + 2, blk, hkv, d)
        return jnp.concatenate([tp[:, :-2], tp[:, 1:-1], tp[:, 2:]], axis=2)

    kb, vb = band(k), band(v)
    s_loc = jnp.einsum('bnqhgd,bnjhd->bnhgqj', qb, kb) * scale
    qi = jnp.arange(blk)[None, :, None]
    kj = jnp.arange(3 * blk)[None, None, :]
    nidx = jnp.arange(nb)[:, None, None]
    kpos = nidx * blk - blk + kj
    qpos = nidx * blk + qi
    mask = (jnp.abs(kpos - qpos) <= SWA_WINDOW) & (kpos >= 0) & (kpos < s)
    s_loc = jnp.where(mask[None, :, None, None], s_loc, -jnp.inf)
    s_ctx = jnp.einsum('bnqhgd,bchd->bnhgqc', qb, kc) * scale
    s_sink = jnp.broadcast_to(sink.reshape(hkv, g)[None, None, :, :, None, None], s_loc.shape[:-1] + (1,))
    prob = jax.nn.softmax(jnp.concatenate([s_loc, s_ctx, s_sink], axis=-1), axis=-1)
    p_loc = prob[..., : 3 * blk]
    p_ctx = prob[..., 3 * blk: 3 * blk + lc]
    o = jnp.einsum('bnhgqj,bnjhd->bnqhgd', p_loc, vb) + jnp.einsum('bnhgqc,bchd->bnqhgd', p_ctx, vc)
    return o.reshape(bsz, s, hq * d)


def _na_latent(q, k, v, kc, vc, rpb):
    bsz, s, h, d = q.shape
    rows = s // GRID_W
    wr = min(NA_WIN_R, rows)
    wc = NA_WIN_C
    scale = d ** -0.5
    r = jnp.arange(rows)
    cidx = jnp.arange(GRID_W)
    r0 = jnp.clip(r - wr // 2, 0, rows - wr)
    c0 = jnp.clip(cidx - wc // 2, 0, GRID_W - wc)
    krow = r0[:, None] + jnp.arange(wr)[None, :]
    q4 = q.reshape(bsz, rows, GRID_W, h, d)
    k4 = k.reshape(bsz, rows, GRID_W, h, d)[:, krow]
    v4 = v.reshape(bsz, rows, GRID_W, h, d)[:, krow]
    sc = jnp.einsum('brqhd,brikhd->bhrqik', q4, k4) * scale
    roff = krow - r[:, None] + (NA_WIN_R - 1)
    coff = jnp.clip(cidx[None, :] - cidx[:, None] + (NA_WIN_C - 1), 0, 2 * NA_WIN_C - 2)
    cvalid = (cidx[None, :] >= c0[:, None]) & (cidx[None, :] < c0[:, None] + wc)
    bias = rpb[:, roff[:, None, :, None], coff[None, :, None, :]]
    sc = jnp.where(cvalid[:, None, :], sc + bias[None], -jnp.inf)
    s_ctx = jnp.einsum('brqhd,bchd->bhrqc', q4, kc) * scale
    nk = wr * GRID_W
    prob = jax.nn.softmax(jnp.concatenate([sc.reshape(bsz, h, rows, GRID_W, nk), s_ctx], axis=-1), axis=-1)
    p_loc = prob[..., :nk].reshape(bsz, h, rows, GRID_W, wr, GRID_W)
    p_ctx = prob[..., nk:]
    o = jnp.einsum('bhrqik,brikhd->brqhd', p_loc, v4) + jnp.einsum('bhrqc,bchd->brqhd', p_ctx, vc)
    return o.reshape(bsz, s, h * d)


def _ctx_attention(qc, kc, vc, sink=None):
    bsz, lc, hq, d = qc.shape
    hkv = kc.shape[2]
    g = hq // hkv
    qg = qc.reshape(bsz, lc, hkv, g, d)
    s = jnp.einsum('bqhgd,bkhd->bhgqk', qg, kc) * d ** -0.5
    if sink is None:
        prob = jax.nn.softmax(s, axis=-1)
    else:
        s_sink = jnp.broadcast_to(sink.reshape(hkv, g)[None, :, :, None, None], s.shape[:-1] + (1,))
        prob = jax.nn.softmax(jnp.concatenate([s, s_sink], axis=-1), axis=-1)[..., :-1]
    o = jnp.einsum('bhgqk,bkhd->bqhgd', prob, vc)
    return o.reshape(bsz, lc, hq * d)


def _mixers(proj, p, row, col, last):
    lc = CTX_LEN
    o_ssd = ssd_mixer(proj, p['ssd_conv_w'], p['ssd_conv_b'], p['ssd_a_log'], p['ssd_dt_bias'], p['ssd_d'],
                      p['ssd_norm_g'], lc)
    blk_off = lc // ATT_BLOCK if last else 0
    return o_ssd, swa_attention(proj, p['swa_sink'], lc, blk_off), na_attention(proj, p['na_rpb'], lc, blk_off)


def _pack_w_in(w):
    o = 0
    segs = {}
    for name, width in (('z', SSD_INNER), ('xs', SSD_INNER), ('bc', 512), ('dt', 2 * SSD_HEADS), ('qs', SWA_Q),
                        ('ks', SWA_KV), ('vs', SWA_KV), ('qn', NA_DIM), ('kn', NA_DIM), ('vn', NA_DIM),
                        ('g', N_BRANCH * D_MODEL)):
        segs[name] = w[:, o:o + width]
        o += width
    pad = jnp.zeros((w.shape[0], 128 - 2 * SSD_HEADS), w.dtype)
    order = ('g', 'z', 'xs', 'bc', 'qs', 'qn', 'kn', 'vn', 'ks', 'vs', 'dt')
    return jnp.concatenate([segs[k] for k in order] + [pad], axis=1).astype(BF16)


def kernel(x, c, ctx, c_ctx, norm1_g, norm2_g, w_ada, b_ada, w_in, b_gate, ssd_conv_w, ssd_conv_b, ssd_a_log,
           ssd_dt_bias, ssd_d, ssd_norm_g, swa_sink, na_rpb, w_br_ssd, w_br_swa, w_br_na, w_out, router_w, router_b,
           moe_w1, moe_b1, moe_w2, moe_b2, final_norm_g):
    bsz, s, d = x.shape
    lc = ctx.shape[1]
    depth = w_in.shape[0]
    pos = jnp.arange(s)
    row, col = pos // GRID_W, pos % GRID_W
    n_mod_rows = 16
    cond = jnp.concatenate([c, c_ctx[None, :], jnp.zeros((n_mod_rows - bsz - 1, d), F32)], axis=0)
    xs = jnp.concatenate([ctx, x], axis=1)
    for i in range(depth):
        last = i == depth - 1
        p = {'ssd_conv_w': ssd_conv_w[i], 'ssd_conv_b': ssd_conv_b[i], 'ssd_a_log': ssd_a_log[i],
             'ssd_dt_bias': ssd_dt_bias[i], 'ssd_d': ssd_d[i], 'ssd_norm_g': ssd_norm_g[i],
             'swa_sink': swa_sink[i], 'na_rpb': na_rpb[i]}
        mod = ada_modulation(cond, w_ada[i], b_ada[i]).reshape(n_mod_rows, 6, d)
        proj = norm_proj(xs, mod, norm1_g[i], _pack_w_in(w_in[i]), lc)
        o_ssd, o_swa, o_na = _mixers(proj, p, row, col, last)
        xs = merge(xs, proj, o_ssd, o_swa, o_na, b_gate[i].reshape(N_BRANCH, d), w_br_ssd[i].astype(BF16),
                   w_br_swa[i].astype(BF16), w_br_na[i].astype(BF16), w_out[i].astype(BF16), mod, lc,
                   lc if last else 0)
        xs = moe_layer(xs, mod, norm2_g[i], router_w[i], router_b[i], moe_w1[i], moe_b1[i], moe_w2[i], moe_b2[i],
                       0 if last else lc, 0)
    return final_norm(xs, final_norm_g)
```

```python
import functools
import math

import jax
import jax.numpy as jnp
import numpy as np
from jax import lax
from jax.experimental import pallas as pl
from jax.experimental.pallas import tpu as pltpu

F32 = jnp.float32
BF16 = jnp.bfloat16

D_MODEL = 1024
GRID_W = 64
CTX_LEN = 256
HEAD_DIM = 64
NORM_EPS = 1e-6

SSD_HEADS = 16
SSD_HEAD_DIM = 64
SSD_INNER = SSD_HEADS * SSD_HEAD_DIM
SSD_GROUPS = 2
SSD_STATE = 128
SSD_CHUNK = 128

SWA_Q_HEADS = 8
SWA_KV_HEADS = 2
SWA_WINDOW = 128
SWA_BLOCK = 128
SWA_Q = SWA_Q_HEADS * HEAD_DIM
SWA_KV = SWA_KV_HEADS * HEAD_DIM
ROPE_THETA = 10000.0

NA_HEADS = 8
NA_WIN_R = 8
NA_WIN_C = 16
NA_DIM = NA_HEADS * HEAD_DIM

N_BRANCH = 3
N_EXPERTS = 32
TOP_K = 4
SWIGLU_LIMIT = 7.0
SWIGLU_ALPHA = 1.702

COL_GATE = 0
COL_Z = 3 * D_MODEL
COL_XS = COL_Z + SSD_INNER
COL_BC = COL_XS + SSD_INNER
COL_QS = COL_BC + 512
COL_QN = COL_QS + SWA_Q
COL_KN = COL_QN + NA_DIM
COL_VN = COL_KN + NA_DIM
COL_KS = COL_VN + NA_DIM
COL_VS = COL_KS + SWA_KV
COL_DT = COL_VS + SWA_KV
IN_COLS_PACKED = COL_DT + 128

MOE_BLOCK_ROWS = 256
ROW_UNROLL = 8
VMEM_LIMIT = 56 * 1024 * 1024


def _cparams(sem):
    return pltpu.CompilerParams(dimension_semantics=sem, vmem_limit_bytes=VMEM_LIMIT)


def _ada_kernel(c_ref, w_ref, b_ref, o_ref):
    c = c_ref[...]
    s = (c * jax.nn.sigmoid(c)).astype(BF16)
    o_ref[...] = jnp.dot(s, w_ref[...].astype(BF16), preferred_element_type=F32) + b_ref[...]


def ada_modulation(rows, w_ada, b_ada):
    r, d = rows.shape
    n = w_ada.shape[1]
    tn = 1024
    return pl.pallas_call(
        _ada_kernel,
        out_shape=jax.ShapeDtypeStruct((r, n), F32),
        grid=(n // tn,),
        in_specs=[pl.BlockSpec((r, d), lambda j: (0, 0)),
                  pl.BlockSpec((d, tn), lambda j: (0, j)),
                  pl.BlockSpec((1, tn), lambda j: (0, j))],
        out_specs=pl.BlockSpec((r, tn), lambda j: (0, j)),
        compiler_params=_cparams(("parallel",)),
        name="ada_modulation",
    )(rows, w_ada, b_ada.reshape(1, n))


def _row_mod(mod_l_ref, mod_c_ref, idx, is_ctx):
    return jnp.where(is_ctx, mod_c_ref[0, idx:idx + 1, :], mod_l_ref[0, idx:idx + 1, :])


def _rms_mod(x, g, scale, shift):
    y = x * lax.rsqrt(jnp.mean(x * x, axis=-1, keepdims=True) + NORM_EPS)
    return (y * g) * (1.0 + scale) + shift


def _norm_proj_kernel(x_ref, ml_ref, mc_ref, g_ref, w_ref, o_ref, h_ref, *, tm, n_ctx):
    i = pl.program_id(1)
    j = pl.program_id(2)

    @pl.when(j == 0)
    def _():
        rows = i * tm + lax.broadcasted_iota(jnp.int32, (tm, 1), 0)
        is_ctx = rows < n_ctx
        h = _rms_mod(x_ref[0], g_ref[...], _row_mod(ml_ref, mc_ref, 1, is_ctx), _row_mod(ml_ref, mc_ref, 0, is_ctx))
        h_ref[...] = h.astype(BF16)

    o_ref[0] = jnp.dot(h_ref[...], w_ref[...], preferred_element_type=F32)


def norm_proj(x, mod, g, w_packed, n_ctx):
    bsz, l, d = x.shape
    n = w_packed.shape[1]
    tm, tn = 768, 1152
    return pl.pallas_call(
        functools.partial(_norm_proj_kernel, tm=tm, n_ctx=n_ctx),
        out_shape=jax.ShapeDtypeStruct((bsz, l, n), F32),
        grid=(bsz, l // tm, n // tn),
        in_specs=[pl.BlockSpec((1, tm, d), lambda b, i, j: (b, i, 0)),
                  pl.BlockSpec((1, 6, d), lambda b, i, j: (b, 0, 0)),
                  pl.BlockSpec((1, 6, d), lambda b, i, j: (bsz, 0, 0)),
                  pl.BlockSpec((1, d), lambda b, i, j: (0, 0)),
                  pl.BlockSpec((d, tn), lambda b, i, j: (0, j))],
        out_specs=pl.BlockSpec((1, tm, tn), lambda b, i, j: (b, i, j)),
        scratch_shapes=[pltpu.VMEM((tm, d), BF16)],
        compiler_params=_cparams(("parallel", "parallel", "arbitrary")),
        name="norm_proj",
    )(x, mod, mod, g.reshape(1, d), w_packed)


def _merge_kernel(x_ref, ga_ref, gb_ref, gn_ref, ossd_ref, oswa_ref, ona_ref, bg_ref, wssd_ref, wswa_ref, wna_ref,
                  wout_ref, ml_ref, mc_ref, o_ref, *, tm, n_ctx, row_off):
    i = pl.program_id(1)
    rows = row_off + i * tm + lax.broadcasted_iota(jnp.int32, (tm, 1), 0)
    is_ctx = rows < n_ctx

    def branch(gate_ref, k, o_br_ref, w_ref):
        gate = jax.nn.sigmoid(gate_ref[0] + bg_ref[k:k + 1, :])
        return gate * jnp.dot(o_br_ref[0].astype(BF16), w_ref[...], preferred_element_type=F32)

    m = branch(ga_ref, 0, ossd_ref, wssd_ref) + branch(gb_ref, 1, oswa_ref, wswa_ref) + branch(gn_ref, 2, ona_ref, wna_ref)
    y = jnp.dot(m.astype(BF16), wout_ref[...], preferred_element_type=F32)
    o_ref[0] = x_ref[0] + _row_mod(ml_ref, mc_ref, 2, is_ctx) * y


def merge(x, proj, o_ssd, o_swa, o_na, b_gate, w_ssd, w_swa, w_na, w_out, mod, n_ctx, row_off):
    bsz, l, d = x.shape
    tm = 256
    gate_blk = COL_GATE // d
    off_blk = row_off // tm

    def row(a):
        o = off_blk if a.shape[1] == l else 0
        return pl.BlockSpec((1, tm, a.shape[-1]), lambda b, i: (b, i + o, 0))

    full = lambda a: pl.BlockSpec(a.shape, lambda b, i: (0,) * a.ndim)
    return pl.pallas_call(
        functools.partial(_merge_kernel, tm=tm, n_ctx=n_ctx, row_off=row_off),
        out_shape=jax.ShapeDtypeStruct((bsz, l - row_off, d), F32),
        grid=(bsz, (l - row_off) // tm),
        in_specs=[row(x),
                  pl.BlockSpec((1, tm, d), lambda b, i: (b, i + off_blk, gate_blk)),
                  pl.BlockSpec((1, tm, d), lambda b, i: (b, i + off_blk, gate_blk + 1)),
                  pl.BlockSpec((1, tm, d), lambda b, i: (b, i + off_blk, gate_blk + 2)),
                  row(o_ssd), row(o_swa), row(o_na),
                  full(b_gate), full(w_ssd), full(w_swa), full(w_na), full(w_out),
                  pl.BlockSpec((1, 6, d), lambda b, i: (b, 0, 0)),
                  pl.BlockSpec((1, 6, d), lambda b, i: (bsz, 0, 0))],
        out_specs=pl.BlockSpec((1, tm, d), lambda b, i: (b, i, 0)),
        compiler_params=_cparams(("parallel", "parallel")),
        name="merge",
    )(x, proj, proj, proj, o_ssd, o_swa, o_na, b_gate, w_ssd, w_swa, w_na, w_out, mod, mod)


def _router_kernel(x_ref, ml_ref, mc_ref, g_ref, rw_ref, rb_ref, hid_ref, idx_ref, gate_ref, *, tm, n_ctx, row_off):
    i = pl.program_id(1)
    rows = row_off + i * tm + lax.broadcasted_iota(jnp.int32, (tm, 1), 0)
    is_ctx = rows < n_ctx
    h = _rms_mod(x_ref[0], g_ref[...], _row_mod(ml_ref, mc_ref, 4, is_ctx), _row_mod(ml_ref, mc_ref, 3, is_ctx))
    hid_ref[0] = h
    logits = jnp.dot(h.astype(BF16), rw_ref[...], preferred_element_type=F32) + rb_ref[...]
    lane = lax.broadcasted_iota(jnp.int32, logits.shape, 1)
    vals, idxs = [], []
    for _ in range(TOP_K):
        m = jnp.max(logits, axis=-1, keepdims=True)
        sel = jnp.min(jnp.where(logits == m, lane, N_EXPERTS), axis=-1, keepdims=True)
        vals.append(m)
        idxs.append(sel)
        logits = jnp.where(lane == sel, -jnp.inf, logits)
    ex = [jnp.exp(v - vals[0]) for v in vals]
    denom = ex[0] + ex[1] + ex[2] + ex[3]
    col = lax.broadcasted_iota(jnp.int32, (tm, TOP_K), 1)
    idx_out = jnp.zeros((tm, TOP_K), jnp.int32)
    gate_out = jnp.zeros((tm, TOP_K), F32)
    for k in range(TOP_K):
        idx_out = jnp.where(col == k, idxs[k], idx_out)
        gate_out = jnp.where(col == k, ex[k] / denom, gate_out)
    idx_ref[0] = idx_out
    gate_ref[0] = gate_out


def moe_router(x, mod, g, router_w, router_b, n_ctx, row_off):
    bsz, l, d = x.shape
    tm = 256
    nrow = (l - row_off) // tm
    off_blk = row_off // tm
    return pl.pallas_call(
        functools.partial(_router_kernel, tm=tm, n_ctx=n_ctx, row_off=row_off),
        out_shape=(jax.ShapeDtypeStruct((bsz, l - row_off, d), F32),
                   jax.ShapeDtypeStruct((bsz, l - row_off, TOP_K), jnp.int32),
                   jax.ShapeDtypeStruct((bsz, l - row_off, TOP_K), F32)),
        grid=(bsz, nrow),
        in_specs=[pl.BlockSpec((1, tm, d), lambda b, i: (b, i + off_blk, 0)),
                  pl.BlockSpec((1, 6, d), lambda b, i: (b, 0, 0)),
                  pl.BlockSpec((1, 6, d), lambda b, i: (bsz, 0, 0)),
                  pl.BlockSpec((1, d), lambda b, i: (0, 0)),
                  pl.BlockSpec((d, N_EXPERTS), lambda b, i: (0, 0)),
                  pl.BlockSpec((1, N_EXPERTS), lambda b, i: (0, 0))],
        out_specs=(pl.BlockSpec((1, tm, d), lambda b, i: (b, i, 0)),
                   pl.BlockSpec((1, tm, TOP_K), lambda b, i: (b, i, 0)),
                   pl.BlockSpec((1, tm, TOP_K), lambda b, i: (b, i, 0))),
        compiler_params=_cparams(("parallel", "parallel")),
        name="moe_router",
    )(x, mod, mod, g.reshape(1, d), router_w.astype(BF16), router_b.reshape(1, N_EXPERTS))


def _expert_kernel(be_ref, nu_ref, idx_hbm, hid_hbm, w1_ref, b1_ref, w2_ref, b2_ref, out_hbm,
                   idx_smem, xbuf, ybuf, w1b_ref, w2b_ref, idx_sem, in_sem, out_sem):
    i = pl.program_id(0)
    n_used = nu_ref[0]
    bm = xbuf.shape[1]
    n_idx_slots = idx_smem.shape[0]

    def idx_copy(blk):
        s = blk % n_idx_slots
        return pltpu.make_async_copy(idx_hbm.at[blk], idx_smem.at[s], idx_sem.at[s])

    def row_in(blk, r):
        tok = idx_smem[blk % n_idx_slots, r]
        s = blk % 2
        return pltpu.make_async_copy(hid_hbm.at[pl.ds(tok, 1)], xbuf.at[s, pl.ds(r, 1)], in_sem.at[s])

    def row_out(blk, r):
        dst = idx_smem[blk % n_idx_slots, bm + r]
        s = blk % 2
        return pltpu.make_async_copy(ybuf.at[s, pl.ds(r, 1)], out_hbm.at[pl.ds(dst, 1)], out_sem.at[s])

    def rows_in_wait(blk):
        s = blk % 2
        pltpu.make_async_copy(hid_hbm.at[pl.ds(0, bm)], xbuf.at[s], in_sem.at[s]).wait()

    def rows_out_wait(blk):
        s = blk % 2
        pltpu.make_async_copy(ybuf.at[s], out_hbm.at[pl.ds(0, bm)], out_sem.at[s]).wait()

    def for_rows(fn):
        def group(gi, carry):
            for u in range(ROW_UNROLL):
                fn(gi * ROW_UNROLL + u)
            return carry

        lax.fori_loop(0, bm // ROW_UNROLL, group, 0)

    @pl.when(i == 0)
    def _():
        idx_copy(0).start()
        idx_copy(0).wait()
        for_rows(lambda r: row_in(0, r).start())

        @pl.when(n_used > 1)
        def _():
            idx_copy(1).start()

    @pl.when(i + 1 < n_used)
    def _():
        idx_copy(i + 1).wait()
        for_rows(lambda r: row_in(i + 1, r).start())

        @pl.when(i + 2 < n_used)
        def _():
            idx_copy(i + 2).start()

    @pl.when(jnp.logical_and(i < n_used, jnp.logical_or(i == 0, be_ref[i] != be_ref[jnp.maximum(i - 1, 0)])))
    def _():
        w1b_ref[...] = w1_ref[0].astype(BF16)
        w2b_ref[...] = w2_ref[0].astype(BF16)

    @pl.when(i < n_used)
    def _():
        s = i % 2
        rows_in_wait(i)
        de = w2_ref.shape[1]
        gu = jnp.dot(xbuf[s].astype(BF16), w1b_ref[...], preferred_element_type=F32) + b1_ref[0]
        gate = jnp.minimum(gu[:, :de], SWIGLU_LIMIT)
        up = jnp.clip(gu[:, de:], -SWIGLU_LIMIT, SWIGLU_LIMIT)
        act = (up + 1.0) * gate * jax.nn.sigmoid(SWIGLU_ALPHA * gate)
        ybuf[s] = jnp.dot(act.astype(BF16), w2b_ref[...], preferred_element_type=F32) + b2_ref[0]

    @pl.when(jnp.logical_and(i >= 1, i <= n_used))
    def _():
        rows_out_wait(i - 1)

    @pl.when(i < n_used)
    def _():
        for_rows(lambda r: row_out(i, r).start())


def moe_experts(hid, idx, block_e, n_used, w1, b1, w2, b2):
    t, d = hid.shape
    n_blocks = idx.shape[0]
    bm = idx.shape[1] // 2
    n_exp, _, two_de = w1.shape
    de = two_de // 2
    any_spec = pl.BlockSpec(memory_space=pl.ANY)
    return pl.pallas_call(
        _expert_kernel,
        out_shape=jax.ShapeDtypeStruct(((TOP_K + 1) * t, d), F32),
        grid_spec=pltpu.PrefetchScalarGridSpec(
            num_scalar_prefetch=2,
            grid=(n_blocks,),
            in_specs=[any_spec, any_spec,
                      pl.BlockSpec((1, d, two_de), lambda i, be, nu: (be[i], 0, 0)),
                      pl.BlockSpec((1, 1, two_de), lambda i, be, nu: (be[i], 0, 0)),
                      pl.BlockSpec((1, de, d), lambda i, be, nu: (be[i], 0, 0)),
                      pl.BlockSpec((1, 1, d), lambda i, be, nu: (be[i], 0, 0))],
            out_specs=any_spec,
            scratch_shapes=[pltpu.SMEM((3, 2 * bm), jnp.int32),
                            pltpu.VMEM((2, bm, d), F32),
                            pltpu.VMEM((2, bm, d), F32),
                            pltpu.VMEM((d, two_de), BF16),
                            pltpu.VMEM((de, d), BF16),
                            pltpu.SemaphoreType.DMA((3,)),
                            pltpu.SemaphoreType.DMA((2,)),
                            pltpu.SemaphoreType.DMA((2,))]),
        compiler_params=_cparams(("arbitrary",)),
        name="moe_experts",
    )(block_e, n_used, idx, hid, w1, b1.reshape(n_exp, 1, two_de), w2, b2.reshape(n_exp, 1, d))


def _combine_kernel(x_ref, y_ref, gate_ref, ml_ref, mc_ref, o_ref, *, tm, n_ctx, row_off):
    i = pl.program_id(1)
    rows = row_off + i * tm + lax.broadcasted_iota(jnp.int32, (tm, 1), 0)
    is_ctx = rows < n_ctx
    gates = gate_ref[0]
    y = y_ref[0, 0] * gates[:, 0:1]
    for k in range(1, TOP_K):
        y = y + y_ref[k, 0] * gates[:, k:k + 1]
    o_ref[0] = x_ref[0] + _row_mod(ml_ref, mc_ref, 5, is_ctx) * y


def moe_combine(x, yk, gates, mod, n_ctx, row_off):
    bsz, l, d = x.shape
    tm = 256
    nrow = (l - row_off) // tm
    off_blk = row_off // tm
    return pl.pallas_call(
        functools.partial(_combine_kernel, tm=tm, n_ctx=n_ctx, row_off=row_off),
        out_shape=jax.ShapeDtypeStruct((bsz, l - row_off, d), F32),
        grid=(bsz, nrow),
        in_specs=[pl.BlockSpec((1, tm, d), lambda b, i: (b, i + off_blk, 0)),
                  pl.BlockSpec((TOP_K, 1, tm, d), lambda b, i: (0, b, i, 0)),
                  pl.BlockSpec((1, tm, TOP_K), lambda b, i: (b, i, 0)),
                  pl.BlockSpec((1, 6, d), lambda b, i: (b, 0, 0)),
                  pl.BlockSpec((1, 6, d), lambda b, i: (bsz, 0, 0))],
        out_specs=pl.BlockSpec((1, tm, d), lambda b, i: (b, i, 0)),
        compiler_params=_cparams(("parallel", "parallel")),
        name="moe_combine",
    )(x, yk, gates, mod, mod)


def moe_layer(x, mod, g, router_w, router_b, w1, b1, w2, b2, n_ctx, row_off):
    bsz, l, d = x.shape
    hid, top_i, gates = moe_router(x, mod, g, router_w, router_b, n_ctx, row_off)
    t = bsz * (l - row_off)
    bm = MOE_BLOCK_ROWS
    n_blocks = (t * TOP_K + bm - 1) // bm + N_EXPERTS
    flat_e = top_i.reshape(-1)
    order = jnp.argsort(flat_e).astype(jnp.int32)
    counts = jnp.sum((flat_e[:, None] == jnp.arange(N_EXPERTS)[None, :]).astype(jnp.int32), axis=0)
    starts = jnp.cumsum(counts) - counts
    padded = (counts + bm - 1) // bm * bm
    pends = jnp.cumsum(padded)
    pstarts = pends - padded
    blk_first = jnp.arange(n_blocks, dtype=jnp.int32) * bm
    in_expert = jnp.logical_and(blk_first[:, None] >= pstarts[None, :], blk_first[:, None] < pends[None, :])
    pick = lambda v: jnp.sum(jnp.where(in_expert, v[None, :], 0), axis=1)
    block_e = jnp.minimum(jnp.sum((blk_first[:, None] >= pends[None, :]).astype(jnp.int32), axis=1), N_EXPERTS - 1)
    off = blk_first - pick(pstarts)
    n_valid = jnp.clip(pick(counts) - off, 0, bm).astype(jnp.int32)
    n_used = (pends[-1] // bm).astype(jnp.int32).reshape(1)
    r = jnp.arange(bm, dtype=jnp.int32)[None, :]
    valid = r < n_valid[:, None]
    pos = jnp.where(valid, (pick(starts) + off)[:, None] + r, 0)
    src = order[pos]
    tok, kk = src // TOP_K, src % TOP_K
    dst = jnp.where(valid, kk * t + tok, TOP_K * t + r)
    idx = jnp.concatenate([tok, dst], axis=1).astype(jnp.int32)
    ys = moe_experts(hid.reshape(t, d), idx, block_e.astype(jnp.int32), n_used, w1, b1, w2, b2)
    return moe_combine(x, ys.reshape(TOP_K + 1, bsz, l - row_off, d), gates, mod, n_ctx, row_off)


def _final_norm_kernel(x_ref, g_ref, o_ref):
    x = x_ref[0]
    o_ref[0] = (x * lax.rsqrt(jnp.mean(x * x, axis=-1, keepdims=True) + NORM_EPS)) * g_ref[...]


def final_norm(x, g):
    bsz, l, d = x.shape
    tm = 512
    return pl.pallas_call(
        _final_norm_kernel,
        out_shape=jax.ShapeDtypeStruct((bsz, l, d), F32),
        grid=(bsz, l // tm),
        in_specs=[pl.BlockSpec((1, tm, d), lambda b, i: (b, i, 0)), pl.BlockSpec((1, d), lambda b, i: (0, 0))],
        out_specs=pl.BlockSpec((1, tm, d), lambda b, i: (b, i, 0)),
        compiler_params=_cparams(("parallel", "parallel")),
        name="final_norm",
    )(x, g.reshape(1, d))


ATT_BLOCK = 128
NEG_BIG = -1e30


def _qk(q, k):
    return lax.dot_general(q, k, (((1,), (1,)), ((), ())), preferred_element_type=F32)


def _softmax_pv(scores, values, extra_logit=None):
    m = functools.reduce(jnp.maximum, [jnp.max(s, axis=-1, keepdims=True) for s in scores])
    if extra_logit is not None:
        m = jnp.maximum(m, extra_logit)
    ps = [jnp.exp(s - m) for s in scores]
    den = functools.reduce(jnp.add, [jnp.sum(p, axis=-1, keepdims=True) for p in ps])
    if extra_logit is not None:
        den = den + jnp.exp(extra_logit - m)
    acc = functools.reduce(jnp.add, [jnp.dot(p.astype(BF16), v, preferred_element_type=F32) for p, v in zip(ps, values)])
    return acc / den


def _na_kernel(q_ref, k_ref, v_ref, bias_ref, o_ref, *, n_ctx, blk_off, n_dbl):
    blk = pl.program_id(1) + blk_off
    n_ctx_blk = n_ctx // ATT_BLOCK
    s2 = jnp.clip(blk - n_ctx_blk - 2, 0, n_dbl - 5)
    start = pl.multiple_of(n_ctx + s2 * ATT_BLOCK, ATT_BLOCK)
    lane = lax.broadcasted_iota(jnp.int32, (1, 2 * HEAD_DIM), 1)
    scale = HEAD_DIM ** -0.5

    def pair_out(p, local):
        cs = slice(p * 2 * HEAD_DIM, (p + 1) * 2 * HEAD_DIM)
        q2 = q_ref[0, :, cs] * scale
        kc = k_ref[0, 0:n_ctx, cs].astype(BF16)
        vc = v_ref[0, 0:n_ctx, cs].astype(BF16)
        if local:
            kl = k_ref[0, pl.ds(start, 5 * ATT_BLOCK), cs].astype(BF16)
            vl = v_ref[0, pl.ds(start, 5 * ATT_BLOCK), cs].astype(BF16)
        outs = []
        for hh in range(2):
            qh = jnp.where((lane < HEAD_DIM) == (hh == 0), q2, 0.0).astype(BF16)
            scores, values = [_qk(qh, kc)], [vc]
            if local:
                scores.append(_qk(qh, kl) + bias_ref[0, 2 * p + hh])
                values.append(vl)
            outs.append(_softmax_pv(scores, values))
        o_ref[0, :, cs] = jnp.where(lane < HEAD_DIM, outs[0], outs[1])

    @pl.when(blk < n_ctx_blk)
    def _():
        for p in range(NA_HEADS // 2):
            pair_out(p, False)

    @pl.when(blk >= n_ctx_blk)
    def _():
        for p in range(NA_HEADS // 2):
            pair_out(p, True)


def _na_bias_tables(rpb, rows):
    n_dbl = rows // 2
    wr = min(NA_WIN_R, rows)
    cidx = np.arange(GRID_W)
    c0 = np.clip(cidx - NA_WIN_C // 2, 0, GRID_W - NA_WIN_C)
    cvalid = (cidx[None, :] >= c0[:, None]) & (cidx[None, :] < c0[:, None] + NA_WIN_C)
    coff = np.clip(cidx[None, :] - cidx[:, None] + (NA_WIN_C - 1), 0, 2 * NA_WIN_C - 2)
    col_onehot = (coff[:, :, None] == np.arange(2 * NA_WIN_C - 1)[None, None, :]) & cvalid[:, :, None]
    pats = (0, 1, 2, n_dbl - 2, n_dbl - 1)
    row_onehot = np.zeros((len(pats), 2, 10, 2 * NA_WIN_R - 1), np.float32)
    for pi, r2 in enumerate(pats):
        s2 = int(np.clip(r2 - 2, 0, n_dbl - 5))
        for qh in range(2):
            qr = 2 * r2 + qh
            r0 = int(np.clip(qr - wr // 2, 0, rows - wr))
            for kk in range(10):
                kr = 2 * s2 + kk
                if r0 <= kr < r0 + wr:
                    row_onehot[pi, qh, kk, kr - qr + NA_WIN_R - 1] = 1.0
    t1 = jnp.einsum('hrc,qkc->hrqk', rpb, jnp.asarray(col_onehot, F32), precision=lax.Precision.HIGHEST)
    bias = jnp.einsum('pajr,hrqk->phaqjk', jnp.asarray(row_onehot), t1, precision=lax.Precision.HIGHEST)
    valid = (row_onehot.sum(-1)[:, None, :, None, :, None] > 0) & cvalid[None, None, None, :, None, :]
    bias = jnp.where(jnp.asarray(valid), bias, NEG_BIG)
    return bias.reshape(len(pats), rpb.shape[0], 2 * GRID_W, 10 * GRID_W)


def na_attention(proj, rpb, n_ctx, blk_off):
    bsz, l, _ = proj.shape
    rows = (l - n_ctx) // GRID_W
    n_dbl = rows // 2
    n_ctx_blk = n_ctx // ATT_BLOCK
    bias = _na_bias_tables(rpb, rows)

    def bias_idx(b, i):
        r2 = i + blk_off - n_ctx_blk
        pat = jnp.where(r2 < 2, r2, jnp.where(r2 >= n_dbl - 2, r2 - (n_dbl - 5), 2))
        return (jnp.clip(pat, 0, 4), 0, 0, 0)

    return pl.pallas_call(
        functools.partial(_na_kernel, n_ctx=n_ctx, blk_off=blk_off, n_dbl=n_dbl),
        out_shape=jax.ShapeDtypeStruct((bsz, l - blk_off * ATT_BLOCK, NA_DIM), F32),
        grid=(bsz, l // ATT_BLOCK - blk_off),
        in_specs=[pl.BlockSpec((1, ATT_BLOCK, NA_DIM), lambda b, i: (b, i + blk_off, COL_QN // NA_DIM)),
                  pl.BlockSpec((1, l, NA_DIM), lambda b, i: (b, 0, COL_KN // NA_DIM)),
                  pl.BlockSpec((1, l, NA_DIM), lambda b, i: (b, 0, COL_VN // NA_DIM)),
                  pl.BlockSpec((1, NA_HEADS, ATT_BLOCK, 5 * ATT_BLOCK), bias_idx)],
        out_specs=pl.BlockSpec((1, ATT_BLOCK, NA_DIM), lambda b, i: (b, i, 0)),
        compiler_params=_cparams(("parallel", "arbitrary")),
        name="na_attention",
    )(proj, proj, proj, bias)


def _swa_kernel(sink_ref, q_ref, k_ref, v_ref, cq_ref, sq_ref, ck_ref, sk_ref, o_ref, *, n_ctx, blk_off, n_blk):
    blk = pl.program_id(1) + blk_off
    n_ctx_blk = n_ctx // ATT_BLOCK
    n = blk - n_ctx_blk
    first = jnp.clip(n - 1, 0, n_blk - 3)
    kstart = pl.multiple_of(first * ATT_BLOCK, ATT_BLOCK)
    lane = lax.broadcasted_iota(jnp.int32, (1, 2 * HEAD_DIM), 1)
    scale = HEAD_DIM ** -0.5
    quarter = HEAD_DIM // 4

    def rope(u, cos, sin):
        w = u.shape[-1]
        li = lax.broadcasted_iota(jnp.int32, (1, w), 1)
        swapped = jnp.where(li % (2 * quarter) < quarter, pltpu.roll(u, w - quarter, 1), pltpu.roll(u, quarter, 1))
        return u * cos + swapped * sin

    def dup_head(x, h):
        other = pltpu.roll(x, HEAD_DIM, 1)
        return jnp.where((lane < HEAD_DIM) == (h == 0), x, other)

    def run(local):
        q = q_ref[0] * scale
        kc_all = k_ref[0, 0:n_ctx, :]
        vc_all = v_ref[0, 0:n_ctx, :]
        if local:
            q = rope(q, cq_ref[...], sq_ref[...])
            kl_all = rope(k_ref[0, pl.ds(n_ctx + kstart, 3 * ATT_BLOCK), :],
                          ck_ref[pl.ds(kstart, 3 * ATT_BLOCK), :], sk_ref[pl.ds(kstart, 3 * ATT_BLOCK), :])
            vl_all = v_ref[0, pl.ds(n_ctx + kstart, 3 * ATT_BLOCK), :]
            qpos = n * ATT_BLOCK + lax.broadcasted_iota(jnp.int32, (ATT_BLOCK, 3 * ATT_BLOCK), 0)
            kpos = kstart + lax.broadcasted_iota(jnp.int32, (ATT_BLOCK, 3 * ATT_BLOCK), 1)
            band = jnp.where(jnp.abs(kpos - qpos) <= SWA_WINDOW, 0.0, NEG_BIG)
        group = SWA_Q_HEADS // SWA_KV_HEADS
        for h in range(SWA_KV_HEADS):
            kc = dup_head(kc_all, h).astype(BF16)
            vc = dup_head(vc_all, h).astype(BF16)
            if local:
                kl = dup_head(kl_all, h).astype(BF16)
                vl = dup_head(vl_all, h).astype(BF16)
            for p in range(h * group // 2, (h + 1) * group // 2):
                cs = slice(p * 2 * HEAD_DIM, (p + 1) * 2 * HEAD_DIM)
                q2 = q[:, cs]
                outs = []
                for hh in range(2):
                    qh = jnp.where((lane < HEAD_DIM) == (hh == 0), q2, 0.0).astype(BF16)
                    scores, values = [_qk(qh, kc)], [vc]
                    if local:
                        scores.append(_qk(qh, kl) + band)
                        values.append(vl)
                    outs.append(_softmax_pv(scores, values, extra_logit=sink_ref[2 * p + hh]))
                o_ref[0, :, cs] = jnp.where(lane < HEAD_DIM, outs[0], outs[1])

    @pl.when(blk < n_ctx_blk)
    def _():
        run(False)

    @pl.when(blk >= n_ctx_blk)
    def _():
        run(True)


def _rope_tables(s):
    pos = np.arange(s)
    quarter = HEAD_DIM // 4
    inv = ROPE_THETA ** (-jnp.arange(0, 2 * quarter, 2, dtype=F32) / (2 * quarter))

    def axis_tables(p):
        ang = jnp.asarray(p, F32)[:, None] * inv[None, :]
        c, sn = jnp.cos(ang), jnp.sin(ang)
        return jnp.concatenate([c, c], axis=1), jnp.concatenate([-sn, sn], axis=1)
    cr, sr = axis_tables(pos // GRID_W)
    cc, sc = axis_tables(pos % GRID_W)
    return jnp.concatenate([cr, cc], axis=1), jnp.concatenate([sr, sc], axis=1)


def swa_attention(proj, sink, n_ctx, blk_off):
    bsz, l, _ = proj.shape
    s = l - n_ctx
    n_blk = s // ATT_BLOCK
    n_ctx_blk = n_ctx // ATT_BLOCK
    cos, sin = _rope_tables(s)
    cq, sq = jnp.tile(cos, (1, SWA_Q_HEADS)), jnp.tile(sin, (1, SWA_Q_HEADS))
    ck, sk = jnp.tile(cos, (1, SWA_KV_HEADS)), jnp.tile(sin, (1, SWA_KV_HEADS))
    qtab = lambda b, i: (jnp.maximum(i + blk_off - n_ctx_blk, 0), 0)
    return pl.pallas_call(
        functools.partial(_swa_kernel, n_ctx=n_ctx, blk_off=blk_off, n_blk=n_blk),
        out_shape=jax.ShapeDtypeStruct((bsz, l - blk_off * ATT_BLOCK, SWA_Q), F32),
        grid=(bsz, l // ATT_BLOCK - blk_off),
        in_specs=[pl.BlockSpec(memory_space=pltpu.SMEM),
                  pl.BlockSpec((1, ATT_BLOCK, SWA_Q), lambda b, i: (b, i + blk_off, COL_QS // SWA_Q)),
                  pl.BlockSpec((1, l, SWA_KV), lambda b, i: (b, 0, COL_KS // SWA_KV)),
                  pl.BlockSpec((1, l, SWA_KV), lambda b, i: (b, 0, COL_VS // SWA_KV)),
                  pl.BlockSpec((ATT_BLOCK, SWA_Q), qtab),
                  pl.BlockSpec((ATT_BLOCK, SWA_Q), qtab),
                  pl.BlockSpec((s, SWA_KV), lambda b, i: (0, 0)),
                  pl.BlockSpec((s, SWA_KV), lambda b, i: (0, 0))],
        out_specs=pl.BlockSpec((1, ATT_BLOCK, SWA_Q), lambda b, i: (b, i, 0)),
        compiler_params=_cparams(("parallel", "arbitrary")),
        name="swa_attention",
    )(sink, proj, proj, proj, cq, sq, ck, sk)


CONV_HALO = 8


def _ssd_kernel(*refs, reverse, n_ctx, n_chunks, final):
    if final:
        (xs_c, xs_p, xs_n, bc_c, bc_p, bc_n, dt_ref, cwx_ref, cbx_ref, cwb_ref, cbb_ref, dtbias_ref, acoef_ref,
         z_ref, yf_ref, dskip_ref, ng_ref, o_ref, state_ref) = refs
    else:
        (xs_c, xs_p, xs_n, bc_c, bc_p, bc_n, dt_ref, cwx_ref, cbx_ref, cwb_ref, cbb_ref, dtbias_ref, acoef_ref,
         o_ref, state_ref) = refs
    q = SSD_CHUNK
    n_st = SSD_STATE
    gw = SSD_INNER // SSD_GROUPS
    j = pl.program_id(1)
    c = _ssd_chunk_of_step(j, reverse, n_ctx // q, n_chunks)
    n_cc = n_ctx // q
    seg_first = jnp.logical_or(c == 0, c == n_cc)
    seg_last = jnp.logical_or(c == n_cc - 1, c == n_chunks - 1)

    @pl.when(j == 0)
    def _():
        state_ref[...] = jnp.zeros_like(state_ref)

    def conv_silu(cur_ref, prev_ref, next_ref, w_ref, b_ref):
        half = w_ref.shape[0] // 2
        prev = jnp.where(seg_first, 0.0, prev_ref[0])
        nxt = jnp.where(seg_last, 0.0, next_ref[0])
        ext = jnp.concatenate([prev, cur_ref[0], nxt], axis=0)
        acc = b_ref[...]
        for k in range(w_ref.shape[0]):
            o = CONV_HALO - half + k
            acc = acc + ext[o:o + q] * w_ref[k:k + 1, :]
        return acc * jax.nn.sigmoid(acc)

    xs = conv_silu(xs_c, xs_p, xs_n, cwx_ref, cbx_ref)
    bcv = conv_silu(bc_c, bc_p, bc_n, cwb_ref, cbb_ref)

    lo = SSD_HEADS if reverse else 0
    dt_raw = dt_ref[0] + dtbias_ref[...]
    dt_all = jnp.maximum(dt_raw, 0.0) + jnp.log(1.0 + jnp.exp(-jnp.abs(dt_raw)))
    a_all = dt_all * acoef_ref[...]
    ri = lax.broadcasted_iota(jnp.int32, (q, q), 0)
    ci = lax.broadcasted_iota(jnp.int32, (q, q), 1)
    causal = (ci >= ri) if reverse else (ci <= ri)
    a_cs = jnp.dot(causal.astype(F32), a_all, precision=lax.Precision.HIGHEST, preferred_element_type=F32)
    total = a_cs[0:1] if reverse else a_cs[q - 1:q]
    a_cs_t = a_cs.T

    lane = lax.broadcasted_iota(jnp.int32, (1, 2 * SSD_HEAD_DIM), 1)

    def expand(v):
        parts = [jnp.where(lane < SSD_HEAD_DIM, v[:, lo + 2 * p:lo + 2 * p + 1], v[:, lo + 2 * p + 1:lo + 2 * p + 2])
                 for p in range(SSD_HEADS // 2)]
        return jnp.concatenate(parts, axis=1)

    xdt = xs * expand(dt_all)
    dec_start = expand(jnp.exp(a_cs))
    dec_end = expand(jnp.exp(total - a_cs))
    dec_total = expand(jnp.exp(total))

    heads_per_group = SSD_HEADS // SSD_GROUPS
    y_parts = []
    for g in range(SSD_GROUPS):
        b_g = bcv[:, g * n_st:(g + 1) * n_st]
        c_g = bcv[:, (SSD_GROUPS + g) * n_st:(SSD_GROUPS + g + 1) * n_st].astype(BF16)
        cb = _qk(c_g, b_g.astype(BF16))
        cols = slice(g * gw, (g + 1) * gw)
        st = state_ref[:, cols]
        y_off = jnp.dot(c_g, st.astype(BF16), preferred_element_type=F32) * dec_start[:, cols]
        diag = []
        for p in range(g * heads_per_group // 2, (g + 1) * heads_per_group // 2):
            x2 = xdt[:, p * 2 * SSD_HEAD_DIM:(p + 1) * 2 * SSD_HEAD_DIM].astype(BF16)
            outs = []
            for hh in range(2):
                h = lo + 2 * p + hh
                seg = jnp.where(causal, a_cs[:, h:h + 1] - a_cs_t[h:h + 1, :], NEG_BIG)
                m = (cb * jnp.exp(seg)).astype(BF16)
                outs.append(jnp.dot(m, x2, preferred_element_type=F32))
            diag.append(jnp.where(lane < SSD_HEAD_DIM, outs[0], outs[1]))
        y_parts.append(jnp.concatenate(diag, axis=1) + y_off)
        xw = (xdt[:, cols] * dec_end[:, cols]).astype(BF16)
        state_ref[:, cols] = dec_total[:, cols] * st + jnp.dot(b_g.T.astype(BF16), xw, preferred_element_type=F32)
    y = jnp.concatenate(y_parts, axis=1)

    if final:
        y = yf_ref[0] + y + dskip_ref[...] * xs
        z = z_ref[0]
        y = y * (z * jax.nn.sigmoid(z))
        o_ref[0] = (y * lax.rsqrt(jnp.mean(y * y, axis=-1, keepdims=True) + NORM_EPS)) * ng_ref[...]
    else:
        o_ref[0] = y


def _ssd_chunk_of_step(j, reverse, n_ctx_chunks, n_chunks):
    if not reverse:
        return j
    return jnp.where(j < n_ctx_chunks, n_ctx_chunks - 1 - j, n_chunks - 1 - (j - n_ctx_chunks))


def ssd_mixer(proj, conv_w, conv_b, a_log, dt_bias, d_skip, norm_g, n_ctx):
    bsz, l, _ = proj.shape
    q = SSD_CHUNK
    n_chunks = l // q
    n_cc = n_ctx // q
    hb = q // CONV_HALO
    pad = lambda v: jnp.concatenate([v.reshape(1, -1), jnp.zeros((1, 128 - v.size), F32)], axis=1)
    dtb = pad(dt_bias)
    acoef = pad(-jnp.exp(a_log))
    cwx, cwb = conv_w[:, :SSD_INNER], conv_w[:, SSD_INNER:]
    cbx, cbb = conv_b[:SSD_INNER].reshape(1, -1), conv_b[SSD_INNER:].reshape(1, -1)
    wbc = cwb.shape[1]

    def call(reverse, extra_in, extra_specs, final):
        chunk = lambda jj: _ssd_chunk_of_step(jj, reverse, n_cc, n_chunks)
        cur = lambda w, off: pl.BlockSpec((1, q, w), lambda b, jj: (b, chunk(jj), off // w))
        prev = lambda w, off: pl.BlockSpec((1, CONV_HALO, w), lambda b, jj: (b, jnp.maximum(chunk(jj) * hb - 1, 0), off // w))
        nxt = lambda w, off: pl.BlockSpec((1, CONV_HALO, w),
                                          lambda b, jj: (b, jnp.minimum((chunk(jj) + 1) * hb, l // CONV_HALO - 1), off // w))
        full = lambda a: pl.BlockSpec(a.shape, lambda b, jj: (0, 0))
        return pl.pallas_call(
            functools.partial(_ssd_kernel, reverse=reverse, n_ctx=n_ctx, n_chunks=n_chunks, final=final),
            out_shape=jax.ShapeDtypeStruct((bsz, l, SSD_INNER), F32),
            grid=(bsz, n_chunks),
            in_specs=[cur(SSD_INNER, COL_XS), prev(SSD_INNER, COL_XS), nxt(SSD_INNER, COL_XS),
                      cur(wbc, COL_BC), prev(wbc, COL_BC), nxt(wbc, COL_BC),
                      cur(128, COL_DT), full(cwx), full(cbx), full(cwb), full(cbb), full(dtb), full(acoef)]
                     + extra_specs(cur, full),
            out_specs=pl.BlockSpec((1, q, SSD_INNER), lambda b, jj: (b, chunk(jj), 0)),
            scratch_shapes=[pltpu.VMEM((SSD_STATE, SSD_INNER), F32)],
            compiler_params=_cparams(("parallel", "arbitrary")),
            name="ssd_bwd" if reverse else "ssd_fwd",
        )(proj, proj, proj, proj, proj, proj, proj, cwx, cbx, cwb, cbb, dtb, acoef, *extra_in)

    y_f = call(False, (), lambda cur, full: [], False)
    dsk = jnp.repeat(d_skip, SSD_HEAD_DIM).reshape(1, SSD_INNER)
    ng = norm_g.reshape(1, SSD_INNER)
    return call(True, (proj, y_f, dsk, ng),
                lambda cur, full: [cur(SSD_INNER, COL_Z), cur(SSD_INNER, 0), full(dsk), full(ng)], True)


def _rmsnorm(x, g):
    y = x * lax.rsqrt(jnp.mean(x * x, axis=-1, keepdims=True) + NORM_EPS)
    return y * g


def _to_heads(t, n):
    return t.reshape(t.shape[0], t.shape[1], n, HEAD_DIM)


def _rope_axis(u, pos):
    d = u.shape[-1]
    inv = ROPE_THETA ** (-jnp.arange(0, d, 2, dtype=F32) / d)
    ang = pos.astype(F32)[:, None] * inv[None, :]
    cos = jnp.cos(ang)[None, :, None, :]
    sin = jnp.sin(ang)[None, :, None, :]
    u1, u2 = u[..., : d // 2], u[..., d // 2:]
    return jnp.concatenate([u1 * cos - u2 * sin, u2 * cos + u1 * sin], axis=-1)


def _rope_2d(t, row, col):
    half = t.shape[-1] // 2
    return jnp.concatenate([_rope_axis(t[..., :half], row), _rope_axis(t[..., half:], col)], axis=-1)


def _dwconv_centred(u, w, b):
    pad = w.shape[0] // 2
    y = lax.conv_general_dilated(u, w[:, None, :], window_strides=(1,), padding=[(pad, pad)],
                                 dimension_numbers=('NWC', 'WIO', 'NWC'), feature_group_count=u.shape[-1])
    return y + b


def _segsum(a):
    t = a.shape[-1]
    cs = jnp.cumsum(a, axis=-1)
    d = cs[..., :, None] - cs[..., None, :]
    mask = jnp.tril(jnp.ones((t, t), dtype=bool))
    return jnp.where(mask, d, -jnp.inf)


def _ssd_scan(x, dt, a_coef, bm, cm, h0):
    b, l, h, p = x.shape
    g, n = bm.shape[2], bm.shape[3]
    hg = h // g
    q = SSD_CHUNK
    nc = l // q
    xf = (x * dt[..., None]).reshape(b, nc, q, g, hg, p)
    a = jnp.moveaxis((dt * a_coef).reshape(b, nc, q, g, hg), 2, -1)
    a_cs = jnp.cumsum(a, axis=-1)
    bc = bm.reshape(b, nc, q, g, n)
    cc = cm.reshape(b, nc, q, g, n)
    lmat = jnp.exp(_segsum(a))
    cb = jnp.einsum('bclgn,bcsgn->bcgls', cc, bc)
    y_diag = jnp.einsum('bcgjls,bcsgjp->bclgjp', cb[:, :, :, None] * lmat, xf)
    decay_states = jnp.exp(a_cs[..., -1:] - a_cs)
    states = jnp.einsum('bcsgn,bcgjs,bcsgjp->bcgjpn', bc, decay_states, xf)
    states = jnp.concatenate([h0.reshape(b, 1, g, hg, p, n), states], axis=1)
    chunk_a = jnp.pad(a_cs[..., -1], ((0, 0), (1, 0), (0, 0), (0, 0)))
    decay_chunk = jnp.exp(_segsum(jnp.moveaxis(chunk_a, 1, -1)))
    new_states = jnp.einsum('bgjzc,bcgjpn->bzgjpn', decay_chunk, states)
    prev_states, final = new_states[:, :-1], new_states[:, -1]
    y_off = jnp.einsum('bclgn,bcgjpn,bcgjl->bclgjp', cc, prev_states, jnp.exp(a_cs))
    y = (y_diag + y_off).reshape(b, l, h, p)
    return y, final.reshape(b, h, p, n)


def _ssd_branch(z, xbc, dtf_raw, dtb_raw, p, h0_f, h0_b):
    bsz, l, _ = z.shape
    xbc = jax.nn.silu(_dwconv_centred(xbc, p['ssd_conv_w'], p['ssd_conv_b']))
    xs = xbc[..., :SSD_INNER].reshape(bsz, l, SSD_HEADS, SSD_HEAD_DIM)
    bm = xbc[..., SSD_INNER:SSD_INNER + SSD_GROUPS * SSD_STATE].reshape(bsz, l, SSD_GROUPS, SSD_STATE)
    cm = xbc[..., SSD_INNER + SSD_GROUPS * SSD_STATE:].reshape(bsz, l, SSD_GROUPS, SSD_STATE)
    a_log = p['ssd_a_log']
    dt_bias = p['ssd_dt_bias']
    dt_f = jax.nn.softplus(dtf_raw + dt_bias[0])
    dt_b = jax.nn.softplus(dtb_raw + dt_bias[1])
    y_f, hf = _ssd_scan(xs, dt_f, -jnp.exp(a_log[0]), bm, cm, h0_f)
    rev = lambda t: jnp.flip(t, axis=1)
    y_b, hb = _ssd_scan(rev(xs), rev(dt_b), -jnp.exp(a_log[1]), rev(bm), rev(cm), h0_b)
    y = y_f + rev(y_b) + p['ssd_d'][:, None] * xs
    y = y.reshape(bsz, l, SSD_INNER) * jax.nn.silu(z)
    return _rmsnorm(y, p['ssd_norm_g']), hf, hb


def _swa_latent(q, k, v, kc, vc, sink):
    bsz, s, hq, d = q.shape
    hkv = k.shape[2]
    g = hq // hkv
    blk = SWA_BLOCK
    nb = s // blk
    lc = kc.shape[1]
    scale = d ** -0.5
    qb = q.reshape(bsz, nb, blk, hkv, g, d)

    def band(t):
        tp = jnp.pad(t, ((0, 0), (blk, blk), (0, 0), (0, 0))).reshape(bsz, nb + 2, blk, hkv, d)
        return jnp.concatenate([tp[:, :-2], tp[:, 1:-1], tp[:, 2:]], axis=2)

    kb, vb = band(k), band(v)
    s_loc = jnp.einsum('bnqhgd,bnjhd->bnhgqj', qb, kb) * scale
    qi = jnp.arange(blk)[None, :, None]
    kj = jnp.arange(3 * blk)[None, None, :]
    nidx = jnp.arange(nb)[:, None, None]
    kpos = nidx * blk - blk + kj
    qpos = nidx * blk + qi
    mask = (jnp.abs(kpos - qpos) <= SWA_WINDOW) & (kpos >= 0) & (kpos < s)
    s_loc = jnp.where(mask[None, :, None, None], s_loc, -jnp.inf)
    s_ctx = jnp.einsum('bnqhgd,bchd->bnhgqc', qb, kc) * scale
    s_sink = jnp.broadcast_to(sink.reshape(hkv, g)[None, None, :, :, None, None], s_loc.shape[:-1] + (1,))
    prob = jax.nn.softmax(jnp.concatenate([s_loc, s_ctx, s_sink], axis=-1), axis=-1)
    p_loc = prob[..., : 3 * blk]
    p_ctx = prob[..., 3 * blk: 3 * blk + lc]
    o = jnp.einsum('bnhgqj,bnjhd->bnqhgd', p_loc, vb) + jnp.einsum('bnhgqc,bchd->bnqhgd', p_ctx, vc)
    return o.reshape(bsz, s, hq * d)


def _na_latent(q, k, v, kc, vc, rpb):
    bsz, s, h, d = q.shape
    rows = s // GRID_W
    wr = min(NA_WIN_R, rows)
    wc = NA_WIN_C
    scale = d ** -0.5
    r = jnp.arange(rows)
    cidx = jnp.arange(GRID_W)
    r0 = jnp.clip(r - wr // 2, 0, rows - wr)
    c0 = jnp.clip(cidx - wc // 2, 0, GRID_W - wc)
    krow = r0[:, None] + jnp.arange(wr)[None, :]
    q4 = q.reshape(bsz, rows, GRID_W, h, d)
    k4 = k.reshape(bsz, rows, GRID_W, h, d)[:, krow]
    v4 = v.reshape(bsz, rows, GRID_W, h, d)[:, krow]
    sc = jnp.einsum('brqhd,brikhd->bhrqik', q4, k4) * scale
    roff = krow - r[:, None] + (NA_WIN_R - 1)
    coff = jnp.clip(cidx[None, :] - cidx[:, None] + (NA_WIN_C - 1), 0, 2 * NA_WIN_C - 2)
    cvalid = (cidx[None, :] >= c0[:, None]) & (cidx[None, :] < c0[:, None] + wc)
    bias = rpb[:, roff[:, None, :, None], coff[None, :, None, :]]
    sc = jnp.where(cvalid[:, None, :], sc + bias[None], -jnp.inf)
    s_ctx = jnp.einsum('brqhd,bchd->bhrqc', q4, kc) * scale
    nk = wr * GRID_W
    prob = jax.nn.softmax(jnp.concatenate([sc.reshape(bsz, h, rows, GRID_W, nk), s_ctx], axis=-1), axis=-1)
    p_loc = prob[..., :nk].reshape(bsz, h, rows, GRID_W, wr, GRID_W)
    p_ctx = prob[..., nk:]
    o = jnp.einsum('bhrqik,brikhd->brqhd', p_loc, v4) + jnp.einsum('bhrqc,bchd->brqhd', p_ctx, vc)
    return o.reshape(bsz, s, h * d)


def _ctx_attention(qc, kc, vc, sink=None):
    bsz, lc, hq, d = qc.shape
    hkv = kc.shape[2]
    g = hq // hkv
    qg = qc.reshape(bsz, lc, hkv, g, d)
    s = jnp.einsum('bqhgd,bkhd->bhgqk', qg, kc) * d ** -0.5
    if sink is None:
        prob = jax.nn.softmax(s, axis=-1)
    else:
        s_sink = jnp.broadcast_to(sink.reshape(hkv, g)[None, :, :, None, None], s.shape[:-1] + (1,))
        prob = jax.nn.softmax(jnp.concatenate([s, s_sink], axis=-1), axis=-1)[..., :-1]
    o = jnp.einsum('bhgqk,bkhd->bqhgd', prob, vc)
    return o.reshape(bsz, lc, hq * d)


def _mixers(proj, p, row, col, last):
    lc = CTX_LEN
    o_ssd = ssd_mixer(proj, p['ssd_conv_w'], p['ssd_conv_b'], p['ssd_a_log'], p['ssd_dt_bias'], p['ssd_d'],
                      p['ssd_norm_g'], lc)
    blk_off = lc // ATT_BLOCK if last else 0
    return o_ssd, swa_attention(proj, p['swa_sink'], lc, blk_off), na_attention(proj, p['na_rpb'], lc, blk_off)


def _pack_w_in(w):
    o = 0
    segs = {}
    for name, width in (('z', SSD_INNER), ('xs', SSD_INNER), ('bc', 512), ('dt', 2 * SSD_HEADS), ('qs', SWA_Q),
                        ('ks', SWA_KV), ('vs', SWA_KV), ('qn', NA_DIM), ('kn', NA_DIM), ('vn', NA_DIM),
                        ('g', N_BRANCH * D_MODEL)):
        segs[name] = w[:, o:o + width]
        o += width
    pad = jnp.zeros((w.shape[0], 128 - 2 * SSD_HEADS), w.dtype)
    order = ('g', 'z', 'xs', 'bc', 'qs', 'qn', 'kn', 'vn', 'ks', 'vs', 'dt')
    return jnp.concatenate([segs[k] for k in order] + [pad], axis=1).astype(BF16)


def kernel(x, c, ctx, c_ctx, norm1_g, norm2_g, w_ada, b_ada, w_in, b_gate, ssd_conv_w, ssd_conv_b, ssd_a_log,
           ssd_dt_bias, ssd_d, ssd_norm_g, swa_sink, na_rpb, w_br_ssd, w_br_swa, w_br_na, w_out, router_w, router_b,
           moe_w1, moe_b1, moe_w2, moe_b2, final_norm_g):
    bsz, s, d = x.shape
    lc = ctx.shape[1]
    depth = w_in.shape[0]
    pos = jnp.arange(s)
    row, col = pos // GRID_W, pos % GRID_W
    n_mod_rows = 16
    cond = jnp.concatenate([c, c_ctx[None, :], jnp.zeros((n_mod_rows - bsz - 1, d), F32)], axis=0)
    xs = jnp.concatenate([ctx, x], axis=1)
    for i in range(depth):
        last = i == depth - 1
        p = {'ssd_conv_w': ssd_conv_w[i], 'ssd_conv_b': ssd_conv_b[i], 'ssd_a_log': ssd_a_log[i],
             'ssd_dt_bias': ssd_dt_bias[i], 'ssd_d': ssd_d[i], 'ssd_norm_g': ssd_norm_g[i],
             'swa_sink': swa_sink[i], 'na_rpb': na_rpb[i]}
        mod = ada_modulation(cond, w_ada[i], b_ada[i]).reshape(n_mod_rows, 6, d)
        proj = norm_proj(xs, mod, norm1_g[i], _pack_w_in(w_in[i]), lc)
        o_ssd, o_swa, o_na = _mixers(proj, p, row, col, last)
        xs = merge(xs, proj, o_ssd, o_swa, o_na, b_gate[i].reshape(N_BRANCH, d), w_br_ssd[i].astype(BF16),
                   w_br_swa[i].astype(BF16), w_br_na[i].astype(BF16), w_out[i].astype(BF16), mod, lc,
                   lc if last else 0)
        xs = moe_layer(xs, mod, norm2_g[i], router_w[i], router_b[i], moe_w1[i], moe_b1[i], moe_w2[i], moe_b2[i],
                       0 if last else lc, 0)
    return final_norm(xs, final_norm_g)
```

```python
import functools

import jax
import jax.numpy as jnp
import numpy as np
from jax import lax
from jax.experimental import pallas as pl
from jax.experimental.pallas import tpu as pltpu

F32 = jnp.float32
BF16 = jnp.bfloat16

D_MODEL = 1024
GRID_W = 64
HEAD_DIM = 64
NORM_EPS = 1e-6

SSD_HEADS = 16
SSD_HEAD_DIM = 64
SSD_INNER = SSD_HEADS * SSD_HEAD_DIM
SSD_GROUPS = 2
SSD_STATE = 128
SSD_CHUNK = 128

SWA_Q_HEADS = 8
SWA_KV_HEADS = 2
SWA_WINDOW = 128
SWA_Q = SWA_Q_HEADS * HEAD_DIM
SWA_KV = SWA_KV_HEADS * HEAD_DIM
ROPE_THETA = 10000.0

NA_HEADS = 8
NA_WIN_R = 8
NA_WIN_C = 16
NA_DIM = NA_HEADS * HEAD_DIM

N_BRANCH = 3
N_EXPERTS = 32
TOP_K = 4
SWIGLU_LIMIT = 7.0
SWIGLU_ALPHA = 1.702

COL_GATE = 0
COL_Z = 3 * D_MODEL
COL_XS = COL_Z + SSD_INNER
COL_BC = COL_XS + SSD_INNER
COL_QS = COL_BC + 512
COL_QN = COL_QS + SWA_Q
COL_KN = COL_QN + NA_DIM
COL_VN = COL_KN + NA_DIM
COL_KS = COL_VN + NA_DIM
COL_VS = COL_KS + SWA_KV
COL_DT = COL_VS + SWA_KV
IN_COLS_PACKED = COL_DT + 128

MOE_BLOCK_ROWS = 256
ROW_UNROLL = 8
EXPERT_CHUNKS = 8
VMEM_LIMIT = 56 * 1024 * 1024


def _cparams(sem):
    return pltpu.CompilerParams(dimension_semantics=sem, vmem_limit_bytes=VMEM_LIMIT)


def _ada_kernel(c_ref, w_ref, b_ref, o_ref):
    c = c_ref[...]
    s = (c * jax.nn.sigmoid(c)).astype(BF16)
    o_ref[...] = jnp.dot(s, w_ref[0].astype(BF16), preferred_element_type=F32) + b_ref[0]


def ada_modulation(rows, w_ada, b_ada, layer):
    r, d = rows.shape
    n = w_ada.shape[2]
    tn = 1024
    return pl.pallas_call(
        _ada_kernel,
        out_shape=jax.ShapeDtypeStruct((r, n), F32),
        grid=(n // tn,),
        in_specs=[pl.BlockSpec((r, d), lambda j: (0, 0)),
                  pl.BlockSpec((1, d, tn), lambda j: (layer, 0, j)),
                  pl.BlockSpec((1, 1, tn), lambda j: (layer, 0, j))],
        out_specs=pl.BlockSpec((r, tn), lambda j: (0, j)),
        compiler_params=_cparams(("parallel",)),
        name="ada_modulation",
    )(rows, w_ada, b_ada.reshape(b_ada.shape[0], 1, n))


def _row_mod(mod_l_ref, mod_c_ref, idx, is_ctx):
    return jnp.where(is_ctx, mod_c_ref[0, idx:idx + 1, :], mod_l_ref[0, idx:idx + 1, :])


def _rms_mod(x, g, scale, shift):
    y = x * lax.rsqrt(jnp.mean(x * x, axis=-1, keepdims=True) + NORM_EPS)
    return (y * g) * (1.0 + scale) + shift


def _norm_proj_kernel(x_ref, ml_ref, mc_ref, g_ref, w_ref, o_ref, h_ref, *, tm, n_ctx):
    i = pl.program_id(1)
    j = pl.program_id(2)

    @pl.when(j == 0)
    def _():
        rows = i * tm + lax.broadcasted_iota(jnp.int32, (tm, 1), 0)
        is_ctx = rows < n_ctx
        h = _rms_mod(x_ref[0], g_ref[...], _row_mod(ml_ref, mc_ref, 1, is_ctx), _row_mod(ml_ref, mc_ref, 0, is_ctx))
        h_ref[...] = h.astype(BF16)

    o_ref[0] = jnp.dot(h_ref[...], w_ref[...], preferred_element_type=F32)


def norm_proj(x, mod, g, w_packed, n_ctx):
    bsz, l, d = x.shape
    n = w_packed.shape[1]
    tm, tn = 768, 1152
    return pl.pallas_call(
        functools.partial(_norm_proj_kernel, tm=tm, n_ctx=n_ctx),
        out_shape=jax.ShapeDtypeStruct((bsz, l, n), F32),
        grid=(bsz, l // tm, n // tn),
        in_specs=[pl.BlockSpec((1, tm, d), lambda b, i, j: (b, i, 0)),
                  pl.BlockSpec((1, 6, d), lambda b, i, j: (b, 0, 0)),
                  pl.BlockSpec((1, 6, d), lambda b, i, j: (bsz, 0, 0)),
                  pl.BlockSpec((1, d), lambda b, i, j: (0, 0)),
                  pl.BlockSpec((d, tn), lambda b, i, j: (0, j))],
        out_specs=pl.BlockSpec((1, tm, tn), lambda b, i, j: (b, i, j)),
        scratch_shapes=[pltpu.VMEM((tm, d), BF16)],
        compiler_params=_cparams(("parallel", "parallel", "arbitrary")),
        name="norm_proj",
    )(x, mod, mod, g.reshape(1, d), w_packed)


def _merge_kernel(x_ref, ga_ref, gb_ref, gn_ref, ossd_ref, oswa_ref, ona_ref, bg_ref, wssd_ref, wswa_ref, wna_ref,
                  wout_ref, ml_ref, mc_ref, o_ref, *, tm, n_ctx, row_off):
    i = pl.program_id(1)
    rows = row_off + i * tm + lax.broadcasted_iota(jnp.int32, (tm, 1), 0)
    is_ctx = rows < n_ctx

    def branch(gate_ref, k, o_br_ref, w_ref):
        gate = jax.nn.sigmoid(gate_ref[0] + bg_ref[k:k + 1, :])
        return gate * jnp.dot(o_br_ref[0].astype(BF16), w_ref[...], preferred_element_type=F32)

    m = branch(ga_ref, 0, ossd_ref, wssd_ref) + branch(gb_ref, 1, oswa_ref, wswa_ref) + branch(gn_ref, 2, ona_ref, wna_ref)
    y = jnp.dot(m.astype(BF16), wout_ref[...], preferred_element_type=F32)
    o_ref[0] = x_ref[0] + _row_mod(ml_ref, mc_ref, 2, is_ctx) * y


def merge(x, proj, o_ssd, o_swa, o_na, b_gate, w_ssd, w_swa, w_na, w_out, mod, n_ctx, row_off):
    bsz, l, d = x.shape
    tm = 256
    gate_blk = COL_GATE // d
    off_blk = row_off // tm

    def row(a):
        o = off_blk if a.shape[1] == l else 0
        return pl.BlockSpec((1, tm, a.shape[-1]), lambda b, i: (b, i + o, 0))

    full = lambda a: pl.BlockSpec(a.shape, lambda b, i: (0,) * a.ndim)
    return pl.pallas_call(
        functools.partial(_merge_kernel, tm=tm, n_ctx=n_ctx, row_off=row_off),
        out_shape=jax.ShapeDtypeStruct((bsz, l - row_off, d), F32),
        grid=(bsz, (l - row_off) // tm),
        in_specs=[row(x),
                  pl.BlockSpec((1, tm, d), lambda b, i: (b, i + off_blk, gate_blk)),
                  pl.BlockSpec((1, tm, d), lambda b, i: (b, i + off_blk, gate_blk + 1)),
                  pl.BlockSpec((1, tm, d), lambda b, i: (b, i + off_blk, gate_blk + 2)),
                  row(o_ssd), row(o_swa), row(o_na),
                  full(b_gate), full(w_ssd), full(w_swa), full(w_na), full(w_out),
                  pl.BlockSpec((1, 6, d), lambda b, i: (b, 0, 0)),
                  pl.BlockSpec((1, 6, d), lambda b, i: (bsz, 0, 0))],
        out_specs=pl.BlockSpec((1, tm, d), lambda b, i: (b, i, 0)),
        compiler_params=_cparams(("parallel", "parallel")),
        name="merge",
    )(x, proj, proj, proj, o_ssd, o_swa, o_na, b_gate, w_ssd, w_swa, w_na, w_out, mod, mod)


def _router_kernel(x_ref, ml_ref, mc_ref, g_ref, rw_ref, rb_ref, hid_ref, idx_ref, gate_ref, *, tm, n_ctx, row_off):
    i = pl.program_id(1)
    rows = row_off + i * tm + lax.broadcasted_iota(jnp.int32, (tm, 1), 0)
    is_ctx = rows < n_ctx
    h = _rms_mod(x_ref[0], g_ref[...], _row_mod(ml_ref, mc_ref, 4, is_ctx), _row_mod(ml_ref, mc_ref, 3, is_ctx))
    hid_ref[0] = h
    logits = jnp.dot(h.astype(BF16), rw_ref[...], preferred_element_type=F32) + rb_ref[...]
    lane = lax.broadcasted_iota(jnp.int32, logits.shape, 1)
    vals, idxs = [], []
    for _ in range(TOP_K):
        m = jnp.max(logits, axis=-1, keepdims=True)
        sel = jnp.min(jnp.where(logits == m, lane, N_EXPERTS), axis=-1, keepdims=True)
        vals.append(m)
        idxs.append(sel)
        logits = jnp.where(lane == sel, -jnp.inf, logits)
    ex = [jnp.exp(v - vals[0]) for v in vals]
    denom = ex[0] + ex[1] + ex[2] + ex[3]
    col = lax.broadcasted_iota(jnp.int32, (tm, TOP_K), 1)
    idx_out = jnp.zeros((tm, TOP_K), jnp.int32)
    gate_out = jnp.zeros((tm, TOP_K), F32)
    for k in range(TOP_K):
        idx_out = jnp.where(col == k, idxs[k], idx_out)
        gate_out = jnp.where(col == k, ex[k] / denom, gate_out)
    idx_ref[0] = idx_out
    gate_ref[0] = gate_out


def moe_router(x, mod, g, router_w, router_b, n_ctx, row_off):
    bsz, l, d = x.shape
    tm = 256
    nrow = (l - row_off) // tm
    off_blk = row_off // tm
    return pl.pallas_call(
        functools.partial(_router_kernel, tm=tm, n_ctx=n_ctx, row_off=row_off),
        out_shape=(jax.ShapeDtypeStruct((bsz, l - row_off, d), F32),
                   jax.ShapeDtypeStruct((bsz, l - row_off, TOP_K), jnp.int32),
                   jax.ShapeDtypeStruct((bsz, l - row_off, TOP_K), F32)),
        grid=(bsz, nrow),
        in_specs=[pl.BlockSpec((1, tm, d), lambda b, i: (b, i + off_blk, 0)),
                  pl.BlockSpec((1, 6, d), lambda b, i: (b, 0, 0)),
                  pl.BlockSpec((1, 6, d), lambda b, i: (bsz, 0, 0)),
                  pl.BlockSpec((1, d), lambda b, i: (0, 0)),
                  pl.BlockSpec((d, N_EXPERTS), lambda b, i: (0, 0)),
                  pl.BlockSpec((1, N_EXPERTS), lambda b, i: (0, 0))],
        out_specs=(pl.BlockSpec((1, tm, d), lambda b, i: (b, i, 0)),
                   pl.BlockSpec((1, tm, TOP_K), lambda b, i: (b, i, 0)),
                   pl.BlockSpec((1, tm, TOP_K), lambda b, i: (b, i, 0))),
        compiler_params=_cparams(("parallel", "parallel")),
        name="moe_router",
    )(x, mod, mod, g.reshape(1, d), router_w.astype(BF16), router_b.reshape(1, N_EXPERTS))


def _expert_kernel(be_ref, nu_ref, idx_hbm, hid_hbm, w1_ref, b1_ref, w2_ref, b2_ref, out_hbm,
                   idx_smem, xbuf, ybuf, w1b_ref, w2b_ref, idx_sem, in_sem, out_sem, *, n_real_rows):
    i = pl.program_id(0)
    n_used = nu_ref[0]
    bm = xbuf.shape[1]
    n_idx_slots = idx_smem.shape[0]
    de = w2_ref.shape[2]
    s = i % 2

    def idx_copy(blk):
        sl = blk % n_idx_slots
        return pltpu.make_async_copy(idx_hbm.at[blk], idx_smem.at[sl], idx_sem.at[sl])

    def row_in(blk, r):
        tok = idx_smem[blk % n_idx_slots, r]
        sl = blk % 2
        return pltpu.make_async_copy(hid_hbm.at[pl.ds(tok, 1)], xbuf.at[sl, pl.ds(r, 1)], in_sem.at[sl])

    def row_out(blk, r):
        dst = idx_smem[blk % n_idx_slots, bm + r]
        sl = blk % 2
        return pltpu.make_async_copy(ybuf.at[sl, pl.ds(r, 1)], out_hbm.at[pl.ds(dst, 1)], out_sem.at[sl])

    def rows_in_wait(blk):
        sl = blk % 2
        pltpu.make_async_copy(hid_hbm.at[pl.ds(0, bm)], xbuf.at[sl], in_sem.at[sl]).wait()

    def rows_out_wait(blk):
        sl = blk % 2
        pltpu.make_async_copy(ybuf.at[sl], out_hbm.at[pl.ds(0, bm)], out_sem.at[sl]).wait()

    def for_rows(fn):
        def group(gi, carry):
            for u in range(ROW_UNROLL):
                fn(gi * ROW_UNROLL + u)
            return carry
        lax.fori_loop(0, bm // ROW_UNROLL, group, 0)

    def experts(issue_rows):
        x = xbuf[s].astype(BF16)
        n_chunks = EXPERT_CHUNKS
        cw = 2 * de // n_chunks
        rows_per = bm // n_chunks
        gu = []
        for cix in range(n_chunks):
            issue_rows(cix * rows_per, (cix + 1) * rows_per)
            cols = slice(cix * cw, (cix + 1) * cw)
            gu.append(jnp.dot(x, w1b_ref[:, cols], preferred_element_type=F32) + b1_ref[0, 0, :, cols])
        half = n_chunks // 2
        acts = []
        for cix in range(half):
            gate = jnp.minimum(gu[cix], SWIGLU_LIMIT)
            up = jnp.clip(gu[half + cix], -SWIGLU_LIMIT, SWIGLU_LIMIT)
            acts.append(((up + 1.0) * gate * jax.nn.sigmoid(SWIGLU_ALPHA * gate)).astype(BF16))
        act = jnp.concatenate(acts, axis=1)
        ybuf[s] = jnp.dot(act, w2b_ref[...], preferred_element_type=F32) + b2_ref[0, 0]

    @pl.when(i == 0)
    def _():
        ybuf[1] = jnp.zeros(ybuf.shape[1:], F32)
        fill = pltpu.make_async_copy(ybuf.at[1], out_hbm.at[pl.ds(n_real_rows, bm)], out_sem.at[1])
        fill.start()
        fill.wait()
        idx_copy(0).start()
        idx_copy(0).wait()
        for_rows(lambda r: row_in(0, r).start())

        @pl.when(n_used > 1)
        def _():
            idx_copy(1).start()

    @pl.when(jnp.logical_and(i >= 2, i <= n_used))
    def _():
        rows_out_wait(i - 2)

    @pl.when(i + 1 < n_used)
    def _():
        idx_copy(i + 1).wait()

        @pl.when(i + 2 < n_used)
        def _():
            idx_copy(i + 2).start()

    @pl.when(jnp.logical_and(i < n_used, jnp.logical_or(i == 0, be_ref[i] != be_ref[jnp.maximum(i - 1, 0)])))
    def _():
        w1b_ref[...] = w1_ref[0, 0].astype(BF16)
        w2b_ref[...] = w2_ref[0, 0].astype(BF16)

    @pl.when(i < n_used)
    def _():
        rows_in_wait(i)

    steady = jnp.logical_and(i >= 1, i + 1 < n_used)

    @pl.when(steady)
    def _():
        def issue_rows(lo, hi):
            for r in range(lo, hi):
                row_in(i + 1, r).start()
            for r in range(lo, hi):
                row_out(i - 1, r).start()
        experts(issue_rows)

    @pl.when(jnp.logical_not(steady))
    def _():
        @pl.when(i + 1 < n_used)
        def _():
            for_rows(lambda r: row_in(i + 1, r).start())

        @pl.when(jnp.logical_and(i >= 1, i <= n_used))
        def _():
            for_rows(lambda r: row_out(i - 1, r).start())

        @pl.when(i < n_used)
        def _():
            experts(lambda lo, hi: None)

        @pl.when(i == n_used)
        def _():
            rows_out_wait(i - 1)


def moe_experts(hid, idx, block_e, n_used, w1, b1, w2, b2, layer):
    t, d = hid.shape
    n_blocks = idx.shape[0]
    bm = idx.shape[1] // 2
    _, n_exp, _, two_de = w1.shape
    de = two_de // 2
    any_spec = pl.BlockSpec(memory_space=pl.ANY)
    return pl.pallas_call(
        functools.partial(_expert_kernel, n_real_rows=TOP_K * t),
        out_shape=jax.ShapeDtypeStruct((TOP_K * t + bm, d), F32),
        grid_spec=pltpu.PrefetchScalarGridSpec(
            num_scalar_prefetch=2,
            grid=(n_blocks,),
            in_specs=[any_spec, any_spec,
                      pl.BlockSpec((1, 1, d, two_de), lambda i, be, nu: (layer, be[i], 0, 0)),
                      pl.BlockSpec((1, 1, 1, two_de), lambda i, be, nu: (layer, be[i], 0, 0)),
                      pl.BlockSpec((1, 1, de, d), lambda i, be, nu: (layer, be[i], 0, 0)),
                      pl.BlockSpec((1, 1, 1, d), lambda i, be, nu: (layer, be[i], 0, 0))],
            out_specs=any_spec,
            scratch_shapes=[pltpu.SMEM((4, 2 * bm), jnp.int32),
                            pltpu.VMEM((2, bm, d), F32),
                            pltpu.VMEM((2, bm, d), F32),
                            pltpu.VMEM((d, two_de), BF16),
                            pltpu.VMEM((de, d), BF16),
                            pltpu.SemaphoreType.DMA((4,)),
                            pltpu.SemaphoreType.DMA((2,)),
                            pltpu.SemaphoreType.DMA((2,))]),
        compiler_params=_cparams(("arbitrary",)),
        name="moe_experts",
    )(block_e, n_used, idx, hid, w1, b1.reshape(b1.shape[0], n_exp, 1, two_de), w2, b2.reshape(b2.shape[0], n_exp, 1, d))


def _combine_kernel(x_ref, y0_ref, y1_ref, y2_ref, y3_ref, gate_ref, ml_ref, mc_ref, o_ref, *, tm, n_ctx, row_off):
    i = pl.program_id(1)
    rows = row_off + i * tm + lax.broadcasted_iota(jnp.int32, (tm, 1), 0)
    is_ctx = rows < n_ctx
    gates = gate_ref[0]
    y = y0_ref[...] * gates[:, 0:1]
    for k, y_ref in enumerate((y1_ref, y2_ref, y3_ref), start=1):
        y = y + y_ref[...] * gates[:, k:k + 1]
    o_ref[0] = x_ref[0] + _row_mod(ml_ref, mc_ref, 5, is_ctx) * y


def moe_combine(x, ys, gates, mod, n_ctx, row_off):
    bsz, l, d = x.shape
    tm = 256
    nrow = (l - row_off) // tm
    off_blk = row_off // tm
    plane = lambda k: pl.BlockSpec((tm, d), lambda b, i: ((k * bsz + b) * nrow + i, 0))
    return pl.pallas_call(
        functools.partial(_combine_kernel, tm=tm, n_ctx=n_ctx, row_off=row_off),
        out_shape=jax.ShapeDtypeStruct((bsz, l - row_off, d), F32),
        grid=(bsz, nrow),
        in_specs=[pl.BlockSpec((1, tm, d), lambda b, i: (b, i + off_blk, 0)),
                  plane(0), plane(1), plane(2), plane(3),
                  pl.BlockSpec((1, tm, TOP_K), lambda b, i: (b, i, 0)),
                  pl.BlockSpec((1, 6, d), lambda b, i: (b, 0, 0)),
                  pl.BlockSpec((1, 6, d), lambda b, i: (bsz, 0, 0))],
        out_specs=pl.BlockSpec((1, tm, d), lambda b, i: (b, i, 0)),
        compiler_params=_cparams(("parallel", "parallel")),
        name="moe_combine",
    )(x, ys, ys, ys, ys, gates, mod, mod)


def moe_layer(x, mod, g, router_w, router_b, w1, b1, w2, b2, layer, n_ctx, row_off):
    bsz, l, d = x.shape
    hid, top_i, gates = moe_router(x, mod, g, router_w, router_b, n_ctx, row_off)
    t = bsz * (l - row_off)
    bm = MOE_BLOCK_ROWS
    n_blocks = (t * TOP_K + bm - 1) // bm + N_EXPERTS
    flat_e = top_i.reshape(-1)
    order = jnp.argsort(flat_e).astype(jnp.int32)
    counts = jnp.sum((flat_e[:, None] == jnp.arange(N_EXPERTS)[None, :]).astype(jnp.int32), axis=0)
    starts = jnp.cumsum(counts) - counts
    padded = (counts + bm - 1) // bm * bm
    pends = jnp.cumsum(padded)
    pstarts = pends - padded
    blk_first = jnp.arange(n_blocks, dtype=jnp.int32) * bm
    in_expert = jnp.logical_and(blk_first[:, None] >= pstarts[None, :], blk_first[:, None] < pends[None, :])
    pick = lambda v: jnp.sum(jnp.where(in_expert, v[None, :], 0), axis=1)
    block_e = jnp.minimum(jnp.sum((blk_first[:, None] >= pends[None, :]).astype(jnp.int32), axis=1), N_EXPERTS - 1)
    off = blk_first - pick(pstarts)
    n_valid = jnp.clip(pick(counts) - off, 0, bm).astype(jnp.int32)
    n_used = (pends[-1] // bm).astype(jnp.int32).reshape(1)
    r = jnp.arange(bm, dtype=jnp.int32)[None, :]
    valid = r < n_valid[:, None]
    pos = jnp.where(valid, (pick(starts) + off)[:, None] + r, 0)
    src = order[pos]
    tok, kk = src // TOP_K, src % TOP_K
    dst = jnp.where(valid, kk * t + tok, TOP_K * t + r)
    idx = jnp.concatenate([tok, dst], axis=1).astype(jnp.int32)
    ys = moe_experts(hid.reshape(t, d), idx, block_e.astype(jnp.int32), n_used, w1, b1, w2, b2, layer)
    return moe_combine(x, ys, gates, mod, n_ctx, row_off)


def _final_norm_kernel(x_ref, g_ref, o_ref):
    x = x_ref[0]
    o_ref[0] = (x * lax.rsqrt(jnp.mean(x * x, axis=-1, keepdims=True) + NORM_EPS)) * g_ref[...]


def final_norm(x, g):
    bsz, l, d = x.shape
    tm = 512
    return pl.pallas_call(
        _final_norm_kernel,
        out_shape=jax.ShapeDtypeStruct((bsz, l, d), F32),
        grid=(bsz, l // tm),
        in_specs=[pl.BlockSpec((1, tm, d), lambda b, i: (b, i, 0)), pl.BlockSpec((1, d), lambda b, i: (0, 0))],
        out_specs=pl.BlockSpec((1, tm, d), lambda b, i: (b, i, 0)),
        compiler_params=_cparams(("parallel", "parallel")),
        name="final_norm",
    )(x, g.reshape(1, d))


ATT_BLOCK = 128
NEG_BIG = -1e30


def _qk(q, k):
    return lax.dot_general(q, k, (((1,), (1,)), ((), ())), preferred_element_type=F32)


def _pair_queries(q2, lane):
    return jnp.concatenate([jnp.where(lane < HEAD_DIM, q2, 0.0), jnp.where(lane < HEAD_DIM, 0.0, q2)], axis=0).astype(BF16)


def _pair_outputs(acc, lane):
    r = acc.shape[0] // 2
    return jnp.where(lane < HEAD_DIM, acc[:r], acc[r:])


def _softmax_pv(scores, values, extra_logit=None):
    m = functools.reduce(jnp.maximum, [jnp.max(s, axis=-1, keepdims=True) for s in scores])
    if extra_logit is not None:
        m = jnp.maximum(m, extra_logit)
    ps = [jnp.exp(s - m) for s in scores]
    den = functools.reduce(jnp.add, [jnp.sum(p, axis=-1, keepdims=True) for p in ps])
    if extra_logit is not None:
        den = den + jnp.exp(extra_logit - m)
    acc = functools.reduce(jnp.add, [jnp.dot(p.astype(BF16), v, preferred_element_type=F32) for p, v in zip(ps, values)])
    return acc / den


def _na_kernel(q_ref, k_ref, v_ref, bias_ref, o_ref, *, n_ctx, blk_off, n_dbl):
    blk = pl.program_id(1) + blk_off
    n_ctx_blk = n_ctx // ATT_BLOCK
    s2 = jnp.clip(blk - n_ctx_blk - 2, 0, n_dbl - 5)
    start = pl.multiple_of(n_ctx + s2 * ATT_BLOCK, ATT_BLOCK)
    lane = lax.broadcasted_iota(jnp.int32, (1, 2 * HEAD_DIM), 1)
    scale = HEAD_DIM ** -0.5

    def pair_out(p, local):
        cs = slice(p * 2 * HEAD_DIM, (p + 1) * 2 * HEAD_DIM)
        q2 = q_ref[0, :, cs] * scale
        kc = k_ref[0, 0:n_ctx, cs].astype(BF16)
        vc = v_ref[0, 0:n_ctx, cs].astype(BF16)
        if local:
            kl = k_ref[0, pl.ds(start, 5 * ATT_BLOCK), cs].astype(BF16)
            vl = v_ref[0, pl.ds(start, 5 * ATT_BLOCK), cs].astype(BF16)
        qp = _pair_queries(q2, lane)
        scores, values = [_qk(qp, kc)], [vc]
        if local:
            scores.append(_qk(qp, kl) + bias_ref[0, p])
            values.append(vl)
        o_ref[0, :, cs] = _pair_outputs(_softmax_pv(scores, values), lane)

    @pl.when(blk < n_ctx_blk)
    def _():
        for p in range(NA_HEADS // 2):
            pair_out(p, False)

    @pl.when(blk >= n_ctx_blk)
    def _():
        for p in range(NA_HEADS // 2):
            pair_out(p, True)


def _na_bias_tables(rpb, rows):
    n_dbl = rows // 2
    wr = min(NA_WIN_R, rows)
    cidx = np.arange(GRID_W)
    c0 = np.clip(cidx - NA_WIN_C // 2, 0, GRID_W - NA_WIN_C)
    cvalid = (cidx[None, :] >= c0[:, None]) & (cidx[None, :] < c0[:, None] + NA_WIN_C)
    coff = np.clip(cidx[None, :] - cidx[:, None] + (NA_WIN_C - 1), 0, 2 * NA_WIN_C - 2)
    col_onehot = (coff[:, :, None] == np.arange(2 * NA_WIN_C - 1)[None, None, :]) & cvalid[:, :, None]
    pats = (0, 1, 2, n_dbl - 2, n_dbl - 1)
    row_onehot = np.zeros((len(pats), 2, 10, 2 * NA_WIN_R - 1), np.float32)
    for pi, r2 in enumerate(pats):
        s2 = int(np.clip(r2 - 2, 0, n_dbl - 5))
        for qh in range(2):
            qr = 2 * r2 + qh
            r0 = int(np.clip(qr - wr // 2, 0, rows - wr))
            for kk in range(10):
                kr = 2 * s2 + kk
                if r0 <= kr < r0 + wr:
                    row_onehot[pi, qh, kk, kr - qr + NA_WIN_R - 1] = 1.0
    t1 = jnp.einsum('hrc,qkc->hrqk', rpb, jnp.asarray(col_onehot, F32), precision=lax.Precision.HIGHEST)
    bias = jnp.einsum('pajr,hrqk->phaqjk', jnp.asarray(row_onehot), t1, precision=lax.Precision.HIGHEST)
    valid = (row_onehot.sum(-1)[:, None, :, None, :, None] > 0) & cvalid[None, None, None, :, None, :]
    bias = jnp.where(jnp.asarray(valid), bias, NEG_BIG)
    return bias.reshape(len(pats), rpb.shape[0] // 2, 4 * GRID_W, 10 * GRID_W)


def na_attention(proj, rpb, n_ctx, blk_off):
    bsz, l, _ = proj.shape
    rows = (l - n_ctx) // GRID_W
    n_dbl = rows // 2
    n_ctx_blk = n_ctx // ATT_BLOCK
    bias = _na_bias_tables(rpb, rows)

    def bias_idx(b, i):
        r2 = i + blk_off - n_ctx_blk
        pat = jnp.where(r2 < 2, r2, jnp.where(r2 >= n_dbl - 2, r2 - (n_dbl - 5), 2))
        return (jnp.clip(pat, 0, 4), 0, 0, 0)

    return pl.pallas_call(
        functools.partial(_na_kernel, n_ctx=n_ctx, blk_off=blk_off, n_dbl=n_dbl),
        out_shape=jax.ShapeDtypeStruct((bsz, l - blk_off * ATT_BLOCK, NA_DIM), F32),
        grid=(bsz, l // ATT_BLOCK - blk_off),
        in_specs=[pl.BlockSpec((1, ATT_BLOCK, NA_DIM), lambda b, i: (b, i + blk_off, COL_QN // NA_DIM)),
                  pl.BlockSpec((1, l, NA_DIM), lambda b, i: (b, 0, COL_KN // NA_DIM)),
                  pl.BlockSpec((1, l, NA_DIM), lambda b, i: (b, 0, COL_VN // NA_DIM)),
                  pl.BlockSpec((1, NA_HEADS // 2, 2 * ATT_BLOCK, 5 * ATT_BLOCK), bias_idx)],
        out_specs=pl.BlockSpec((1, ATT_BLOCK, NA_DIM), lambda b, i: (b, i, 0)),
        compiler_params=_cparams(("parallel", "arbitrary")),
        name="na_attention",
    )(proj, proj, proj, bias)


def _swa_kernel(sink_ref, q_ref, k_ref, v_ref, cq_ref, sq_ref, ck_ref, sk_ref, o_ref, *, n_ctx, blk_off, n_blk):
    blk = pl.program_id(1) + blk_off
    n_ctx_blk = n_ctx // ATT_BLOCK
    n = blk - n_ctx_blk
    first = jnp.clip(n - 1, 0, n_blk - 3)
    kstart = pl.multiple_of(first * ATT_BLOCK, ATT_BLOCK)
    lane = lax.broadcasted_iota(jnp.int32, (1, 2 * HEAD_DIM), 1)
    scale = HEAD_DIM ** -0.5
    quarter = HEAD_DIM // 4

    def rope(u, cos, sin):
        w = u.shape[-1]
        li = lax.broadcasted_iota(jnp.int32, (1, w), 1)
        swapped = jnp.where(li % (2 * quarter) < quarter, pltpu.roll(u, w - quarter, 1), pltpu.roll(u, quarter, 1))
        return u * cos + swapped * sin

    def dup_head(x, h):
        other = pltpu.roll(x, HEAD_DIM, 1)
        return jnp.where((lane < HEAD_DIM) == (h == 0), x, other)

    def run(local):
        q = q_ref[0] * scale
        kc_all = k_ref[0, 0:n_ctx, :]
        vc_all = v_ref[0, 0:n_ctx, :]
        if local:
            q = rope(q, cq_ref[...], sq_ref[...])
            kl_all = rope(k_ref[0, pl.ds(n_ctx + kstart, 3 * ATT_BLOCK), :],
                          ck_ref[pl.ds(kstart, 3 * ATT_BLOCK), :], sk_ref[pl.ds(kstart, 3 * ATT_BLOCK), :])
            vl_all = v_ref[0, pl.ds(n_ctx + kstart, 3 * ATT_BLOCK), :]
            qrow = lax.broadcasted_iota(jnp.int32, (2 * ATT_BLOCK, 3 * ATT_BLOCK), 0) % ATT_BLOCK
            kpos = kstart + lax.broadcasted_iota(jnp.int32, (2 * ATT_BLOCK, 3 * ATT_BLOCK), 1)
            band = jnp.where(jnp.abs(kpos - (n * ATT_BLOCK + qrow)) <= SWA_WINDOW, 0.0, NEG_BIG)
        group = SWA_Q_HEADS // SWA_KV_HEADS
        first_head = lax.broadcasted_iota(jnp.int32, (2 * ATT_BLOCK, 1), 0) < ATT_BLOCK
        for h in range(SWA_KV_HEADS):
            kc = dup_head(kc_all, h).astype(BF16)
            vc = dup_head(vc_all, h).astype(BF16)
            if local:
                kl = dup_head(kl_all, h).astype(BF16)
                vl = dup_head(vl_all, h).astype(BF16)
            for p in range(h * group // 2, (h + 1) * group // 2):
                cs = slice(p * 2 * HEAD_DIM, (p + 1) * 2 * HEAD_DIM)
                qp = _pair_queries(q[:, cs], lane)
                scores, values = [_qk(qp, kc)], [vc]
                if local:
                    scores.append(_qk(qp, kl) + band)
                    values.append(vl)
                sink = jnp.where(first_head, sink_ref[2 * p], sink_ref[2 * p + 1])
                o_ref[0, :, cs] = _pair_outputs(_softmax_pv(scores, values, extra_logit=sink), lane)

    @pl.when(blk < n_ctx_blk)
    def _():
        run(False)

    @pl.when(blk >= n_ctx_blk)
    def _():
        run(True)


def _rope_tables(s):
    pos = np.arange(s)
    quarter = HEAD_DIM // 4
    inv = ROPE_THETA ** (-jnp.arange(0, 2 * quarter, 2, dtype=F32) / (2 * quarter))

    def axis_tables(p):
        ang = jnp.asarray(p, F32)[:, None] * inv[None, :]
        c, sn = jnp.cos(ang), jnp.sin(ang)
        return jnp.concatenate([c, c], axis=1), jnp.concatenate([-sn, sn], axis=1)
    cr, sr = axis_tables(pos // GRID_W)
    cc, sc = axis_tables(pos % GRID_W)
    return jnp.concatenate([cr, cc], axis=1), jnp.concatenate([sr, sc], axis=1)


def swa_attention(proj, sink, n_ctx, blk_off):
    bsz, l, _ = proj.shape
    s = l - n_ctx
    n_blk = s // ATT_BLOCK
    n_ctx_blk = n_ctx // ATT_BLOCK
    cos, sin = _rope_tables(s)
    cq, sq = jnp.tile(cos, (1, SWA_Q_HEADS)), jnp.tile(sin, (1, SWA_Q_HEADS))
    ck, sk = jnp.tile(cos, (1, SWA_KV_HEADS)), jnp.tile(sin, (1, SWA_KV_HEADS))
    qtab = lambda b, i: (jnp.maximum(i + blk_off - n_ctx_blk, 0), 0)
    return pl.pallas_call(
        functools.partial(_swa_kernel, n_ctx=n_ctx, blk_off=blk_off, n_blk=n_blk),
        out_shape=jax.ShapeDtypeStruct((bsz, l - blk_off * ATT_BLOCK, SWA_Q), F32),
        grid=(bsz, l // ATT_BLOCK - blk_off),
        in_specs=[pl.BlockSpec(memory_space=pltpu.SMEM),
                  pl.BlockSpec((1, ATT_BLOCK, SWA_Q), lambda b, i: (b, i + blk_off, COL_QS // SWA_Q)),
                  pl.BlockSpec((1, l, SWA_KV), lambda b, i: (b, 0, COL_KS // SWA_KV)),
                  pl.BlockSpec((1, l, SWA_KV), lambda b, i: (b, 0, COL_VS // SWA_KV)),
                  pl.BlockSpec((ATT_BLOCK, SWA_Q), qtab),
                  pl.BlockSpec((ATT_BLOCK, SWA_Q), qtab),
                  pl.BlockSpec((s, SWA_KV), lambda b, i: (0, 0)),
                  pl.BlockSpec((s, SWA_KV), lambda b, i: (0, 0))],
        out_specs=pl.BlockSpec((1, ATT_BLOCK, SWA_Q), lambda b, i: (b, i, 0)),
        compiler_params=_cparams(("parallel", "arbitrary")),
        name="swa_attention",
    )(sink, proj, proj, proj, cq, sq, ck, sk)


CONV_HALO = 8


def _ssd_kernel(*refs, reverse, n_ctx, n_chunks, final):
    if final:
        (xs_c, xs_p, xs_n, bc_c, bc_p, bc_n, dt_ref, cwx_ref, cbx_ref, cwb_ref, cbb_ref, dtbias_ref, acoef_ref,
         z_ref, yf_ref, dskip_ref, ng_ref, o_ref, state_ref) = refs
    else:
        (xs_c, xs_p, xs_n, bc_c, bc_p, bc_n, dt_ref, cwx_ref, cbx_ref, cwb_ref, cbb_ref, dtbias_ref, acoef_ref,
         o_ref, state_ref) = refs
    q = SSD_CHUNK
    n_st = SSD_STATE
    gw = SSD_INNER // SSD_GROUPS
    j = pl.program_id(1)
    c = _ssd_chunk_of_step(j, reverse, n_ctx // q, n_chunks)
    n_cc = n_ctx // q
    seg_first = jnp.logical_or(c == 0, c == n_cc)
    seg_last = jnp.logical_or(c == n_cc - 1, c == n_chunks - 1)

    @pl.when(j == 0)
    def _():
        state_ref[...] = jnp.zeros_like(state_ref)

    def conv_silu(cur_ref, prev_ref, next_ref, w_ref, b_ref):
        half = w_ref.shape[0] // 2
        prev = jnp.where(seg_first, 0.0, prev_ref[0])
        nxt = jnp.where(seg_last, 0.0, next_ref[0])
        ext = jnp.concatenate([prev, cur_ref[0], nxt], axis=0)
        acc = b_ref[...]
        for k in range(w_ref.shape[0]):
            o = CONV_HALO - half + k
            acc = acc + ext[o:o + q] * w_ref[k:k + 1, :]
        return acc * jax.nn.sigmoid(acc)

    xs = conv_silu(xs_c, xs_p, xs_n, cwx_ref, cbx_ref)
    bcv = conv_silu(bc_c, bc_p, bc_n, cwb_ref, cbb_ref)

    lo = SSD_HEADS if reverse else 0
    dt_raw = dt_ref[0] + dtbias_ref[...]
    dt_all = jnp.maximum(dt_raw, 0.0) + jnp.log(1.0 + jnp.exp(-jnp.abs(dt_raw)))
    a_all = dt_all * acoef_ref[...]
    ri = lax.broadcasted_iota(jnp.int32, (q, q), 0)
    ci = lax.broadcasted_iota(jnp.int32, (q, q), 1)
    causal = (ci >= ri) if reverse else (ci <= ri)
    a_cs = jnp.dot(causal.astype(F32), a_all, precision=lax.Precision.HIGHEST, preferred_element_type=F32)
    total = a_cs[0:1] if reverse else a_cs[q - 1:q]
    a_cs_t = a_cs.T

    lane = lax.broadcasted_iota(jnp.int32, (1, 2 * SSD_HEAD_DIM), 1)

    def expand(v):
        parts = [jnp.where(lane < SSD_HEAD_DIM, v[:, lo + 2 * p:lo + 2 * p + 1], v[:, lo + 2 * p + 1:lo + 2 * p + 2])
                 for p in range(SSD_HEADS // 2)]
        return jnp.concatenate(parts, axis=1)

    xdt = xs * expand(dt_all)
    dec_start = expand(jnp.exp(a_cs))
    dec_end = expand(jnp.exp(total - a_cs))
    dec_total = expand(jnp.exp(total))

    heads_per_group = SSD_HEADS // SSD_GROUPS
    y_parts = []
    for g in range(SSD_GROUPS):
        b_g = bcv[:, g * n_st:(g + 1) * n_st]
        c_g = bcv[:, (SSD_GROUPS + g) * n_st:(SSD_GROUPS + g + 1) * n_st].astype(BF16)
        cb = _qk(c_g, b_g.astype(BF16))
        cols = slice(g * gw, (g + 1) * gw)
        st = state_ref[:, cols]
        y_off = jnp.dot(c_g, st.astype(BF16), preferred_element_type=F32) * dec_start[:, cols]
        diag = []
        for p in range(g * heads_per_group // 2, (g + 1) * heads_per_group // 2):
            x2 = xdt[:, p * 2 * SSD_HEAD_DIM:(p + 1) * 2 * SSD_HEAD_DIM].astype(BF16)
            outs = []
            for hh in range(2):
                h = lo + 2 * p + hh
                seg = jnp.where(causal, a_cs[:, h:h + 1] - a_cs_t[h:h + 1, :], NEG_BIG)
                m = (cb * jnp.exp(seg)).astype(BF16)
                outs.append(jnp.dot(m, x2, preferred_element_type=F32))
            diag.append(jnp.where(lane < SSD_HEAD_DIM, outs[0], outs[1]))
        y_parts.append(jnp.concatenate(diag, axis=1) + y_off)
        xw = (xdt[:, cols] * dec_end[:, cols]).astype(BF16)
        state_ref[:, cols] = dec_total[:, cols] * st + jnp.dot(b_g.T.astype(BF16), xw, preferred_element_type=F32)
    y = jnp.concatenate(y_parts, axis=1)

    if final:
        y = yf_ref[0] + y + dskip_ref[...] * xs
        z = z_ref[0]
        y = y * (z * jax.nn.sigmoid(z))
        o_ref[0] = (y * lax.rsqrt(jnp.mean(y * y, axis=-1, keepdims=True) + NORM_EPS)) * ng_ref[...]
    else:
        o_ref[0] = y


def _ssd_chunk_of_step(j, reverse, n_ctx_chunks, n_chunks):
    if not reverse:
        return j
    return jnp.where(j < n_ctx_chunks, n_ctx_chunks - 1 - j, n_chunks - 1 - (j - n_ctx_chunks))


def ssd_mixer(proj, conv_w, conv_b, a_log, dt_bias, d_skip, norm_g, n_ctx):
    bsz, l, _ = proj.shape
    q = SSD_CHUNK
    n_chunks = l // q
    n_cc = n_ctx // q
    hb = q // CONV_HALO
    pad = lambda v: jnp.concatenate([v.reshape(1, -1), jnp.zeros((1, 128 - v.size), F32)], axis=1)
    dtb = pad(dt_bias)
    acoef = pad(-jnp.exp(a_log))
    cwx, cwb = conv_w[:, :SSD_INNER], conv_w[:, SSD_INNER:]
    cbx, cbb = conv_b[:SSD_INNER].reshape(1, -1), conv_b[SSD_INNER:].reshape(1, -1)
    wbc = cwb.shape[1]

    def call(reverse, extra_in, extra_specs, final):
        chunk = lambda jj: _ssd_chunk_of_step(jj, reverse, n_cc, n_chunks)
        cur = lambda w, off: pl.BlockSpec((1, q, w), lambda b, jj: (b, chunk(jj), off // w))
        prev = lambda w, off: pl.BlockSpec((1, CONV_HALO, w), lambda b, jj: (b, jnp.maximum(chunk(jj) * hb - 1, 0), off // w))
        nxt = lambda w, off: pl.BlockSpec((1, CONV_HALO, w),
                                          lambda b, jj: (b, jnp.minimum((chunk(jj) + 1) * hb, l // CONV_HALO - 1), off // w))
        full = lambda a: pl.BlockSpec(a.shape, lambda b, jj: (0, 0))
        return pl.pallas_call(
            functools.partial(_ssd_kernel, reverse=reverse, n_ctx=n_ctx, n_chunks=n_chunks, final=final),
            out_shape=jax.ShapeDtypeStruct((bsz, l, SSD_INNER), F32),
            grid=(bsz, n_chunks),
            in_specs=[cur(SSD_INNER, COL_XS), prev(SSD_INNER, COL_XS), nxt(SSD_INNER, COL_XS),
                      cur(wbc, COL_BC), prev(wbc, COL_BC), nxt(wbc, COL_BC),
                      cur(128, COL_DT), full(cwx), full(cbx), full(cwb), full(cbb), full(dtb), full(acoef)]
                     + extra_specs(cur, full),
            out_specs=pl.BlockSpec((1, q, SSD_INNER), lambda b, jj: (b, chunk(jj), 0)),
            scratch_shapes=[pltpu.VMEM((SSD_STATE, SSD_INNER), F32)],
            compiler_params=_cparams(("parallel", "arbitrary")),
            name="ssd_bwd" if reverse else "ssd_fwd",
        )(proj, proj, proj, proj, proj, proj, proj, cwx, cbx, cwb, cbb, dtb, acoef, *extra_in)

    y_f = call(False, (), lambda cur, full: [], False)
    dsk = jnp.repeat(d_skip, SSD_HEAD_DIM).reshape(1, SSD_INNER)
    ng = norm_g.reshape(1, SSD_INNER)
    return call(True, (proj, y_f, dsk, ng),
                lambda cur, full: [cur(SSD_INNER, COL_Z), cur(SSD_INNER, 0), full(dsk), full(ng)], True)


def _pack_w_in(w):
    o = 0
    segs = {}
    for name, width in (('z', SSD_INNER), ('xs', SSD_INNER), ('bc', 512), ('dt', 2 * SSD_HEADS), ('qs', SWA_Q),
                        ('ks', SWA_KV), ('vs', SWA_KV), ('qn', NA_DIM), ('kn', NA_DIM), ('vn', NA_DIM),
                        ('g', N_BRANCH * D_MODEL)):
        segs[name] = w[:, o:o + width]
        o += width
    pad = jnp.zeros((w.shape[0], 128 - 2 * SSD_HEADS), w.dtype)
    order = ('g', 'z', 'xs', 'bc', 'qs', 'qn', 'kn', 'vn', 'ks', 'vs', 'dt')
    return jnp.concatenate([segs[k] for k in order] + [pad], axis=1).astype(BF16)


def kernel(x, c, ctx, c_ctx, norm1_g, norm2_g, w_ada, b_ada, w_in, b_gate, ssd_conv_w, ssd_conv_b, ssd_a_log,
           ssd_dt_bias, ssd_d, ssd_norm_g, swa_sink, na_rpb, w_br_ssd, w_br_swa, w_br_na, w_out, router_w, router_b,
           moe_w1, moe_b1, moe_w2, moe_b2, final_norm_g):
    bsz, s, d = x.shape
    lc = ctx.shape[1]
    depth = w_in.shape[0]
    n_mod_rows = 16
    cond = jnp.concatenate([c, c_ctx[None, :], jnp.zeros((n_mod_rows - bsz - 1, d), F32)], axis=0)
    xs = jnp.concatenate([ctx, x], axis=1)
    for i in range(depth):
        last = i == depth - 1
        mod = ada_modulation(cond, w_ada, b_ada, i).reshape(n_mod_rows, 6, d)
        proj = norm_proj(xs, mod, norm1_g[i], _pack_w_in(w_in[i]), lc)
        o_ssd = ssd_mixer(proj, ssd_conv_w[i], ssd_conv_b[i], ssd_a_log[i], ssd_dt_bias[i], ssd_d[i], ssd_norm_g[i], lc)
        blk_off = lc // ATT_BLOCK if last else 0
        o_swa = swa_attention(proj, swa_sink[i], lc, blk_off)
        o_na = na_attention(proj, na_rpb[i], lc, blk_off)
        xs = merge(xs, proj, o_ssd, o_swa, o_na, b_gate[i].reshape(N_BRANCH, d), w_br_ssd[i].astype(BF16),
                   w_br_swa[i].astype(BF16), w_br_na[i].astype(BF16), w_out[i].astype(BF16), mod, lc,
                   lc if last else 0)
        xs = moe_layer(xs, mod, norm2_g[i], router_w[i], router_b[i], moe_w1, moe_b1, moe_w2, moe_b2, i,
                       0 if last else lc, 0)
    return final_norm(xs, final_norm_g)
```

```python
import functools

import jax
import jax.numpy as jnp
import numpy as np
from jax import lax
from jax.experimental import pallas as pl
from jax.experimental.pallas import tpu as pltpu

F32 = jnp.float32
BF16 = jnp.bfloat16

D_MODEL = 1024
GRID_W = 64
HEAD_DIM = 64
NORM_EPS = 1e-6

SSD_HEADS = 16
SSD_HEAD_DIM = 64
SSD_INNER = SSD_HEADS * SSD_HEAD_DIM
SSD_GROUPS = 2
SSD_STATE = 128
SSD_CHUNK = 128

SWA_Q_HEADS = 8
SWA_KV_HEADS = 2
SWA_WINDOW = 128
SWA_Q = SWA_Q_HEADS * HEAD_DIM
SWA_KV = SWA_KV_HEADS * HEAD_DIM
ROPE_THETA = 10000.0

NA_HEADS = 8
NA_WIN_R = 8
NA_WIN_C = 16
NA_DIM = NA_HEADS * HEAD_DIM

N_BRANCH = 3
N_EXPERTS = 32
TOP_K = 4
SWIGLU_LIMIT = 7.0
SWIGLU_ALPHA = 1.702

COL_GATE = 0
COL_Z = 3 * D_MODEL
COL_XS = COL_Z + SSD_INNER
COL_BC = COL_XS + SSD_INNER
COL_QS = COL_BC + 512
COL_QN = COL_QS + SWA_Q
COL_KN = COL_QN + NA_DIM
COL_VN = COL_KN + NA_DIM
COL_KS = COL_VN + NA_DIM
COL_VS = COL_KS + SWA_KV
COL_DT = COL_VS + SWA_KV
IN_COLS_PACKED = COL_DT + 128

MOE_BLOCK_ROWS = 256
ROW_UNROLL = 8
EXPERT_CHUNKS = 8
VMEM_LIMIT = 56 * 1024 * 1024


def _cparams(sem):
    return pltpu.CompilerParams(dimension_semantics=sem, vmem_limit_bytes=VMEM_LIMIT)


def _ada_kernel(c_ref, w_ref, b_ref, o_ref):
    c = c_ref[...]
    s = (c * jax.nn.sigmoid(c)).astype(BF16)
    o_ref[...] = jnp.dot(s, w_ref[0].astype(BF16), preferred_element_type=F32) + b_ref[0]


def ada_modulation(rows, w_ada, b_ada, layer):
    r, d = rows.shape
    n = w_ada.shape[2]
    tn = 1024
    return pl.pallas_call(
        _ada_kernel,
        out_shape=jax.ShapeDtypeStruct((r, n), F32),
        grid=(n // tn,),
        in_specs=[pl.BlockSpec((r, d), lambda j: (0, 0)),
                  pl.BlockSpec((1, d, tn), lambda j: (layer, 0, j)),
                  pl.BlockSpec((1, 1, tn), lambda j: (layer, 0, j))],
        out_specs=pl.BlockSpec((r, tn), lambda j: (0, j)),
        compiler_params=_cparams(("parallel",)),
        name="ada_modulation",
    )(rows, w_ada, b_ada.reshape(b_ada.shape[0], 1, n))


def _row_mod(mod_l_ref, mod_c_ref, idx, is_ctx):
    return jnp.where(is_ctx, mod_c_ref[0, idx:idx + 1, :], mod_l_ref[0, idx:idx + 1, :])


def _rms_mod(x, g, scale, shift):
    y = x * lax.rsqrt(jnp.mean(x * x, axis=-1, keepdims=True) + NORM_EPS)
    return (y * g) * (1.0 + scale) + shift


def _norm_proj_kernel(x_ref, ml_ref, mc_ref, g_ref, w_ref, o_ref, h_ref, *, tm, n_ctx):
    i = pl.program_id(1)
    j = pl.program_id(2)

    @pl.when(j == 0)
    def _():
        rows = i * tm + lax.broadcasted_iota(jnp.int32, (tm, 1), 0)
        is_ctx = rows < n_ctx
        h = _rms_mod(x_ref[0], g_ref[...], _row_mod(ml_ref, mc_ref, 1, is_ctx), _row_mod(ml_ref, mc_ref, 0, is_ctx))
        h_ref[...] = h.astype(BF16)

    o_ref[0] = jnp.dot(h_ref[...], w_ref[...], preferred_element_type=F32)


def norm_proj(x, mod, g, w_packed, n_ctx):
    bsz, l, d = x.shape
    n = w_packed.shape[1]
    tm, tn = 768, 2688
    return pl.pallas_call(
        functools.partial(_norm_proj_kernel, tm=tm, n_ctx=n_ctx),
        out_shape=jax.ShapeDtypeStruct((bsz, l, n), F32),
        grid=(bsz, l // tm, n // tn),
        in_specs=[pl.BlockSpec((1, tm, d), lambda b, i, j: (b, i, 0)),
                  pl.BlockSpec((1, 6, d), lambda b, i, j: (b, 0, 0)),
                  pl.BlockSpec((1, 6, d), lambda b, i, j: (bsz, 0, 0)),
                  pl.BlockSpec((1, d), lambda b, i, j: (0, 0)),
                  pl.BlockSpec((d, tn), lambda b, i, j: (0, j))],
        out_specs=pl.BlockSpec((1, tm, tn), lambda b, i, j: (b, i, j)),
        scratch_shapes=[pltpu.VMEM((tm, d), BF16)],
        compiler_params=_cparams(("parallel", "parallel", "arbitrary")),
        name="norm_proj",
    )(x, mod, mod, g.reshape(1, d), w_packed)


def _merge_kernel(x_ref, ga_ref, gb_ref, gn_ref, ossd_ref, oswa_ref, ona_ref, bg_ref, wssd_ref, wswa_ref, wna_ref,
                  wout_ref, ml_ref, mc_ref, o_ref, *, tm, n_ctx, row_off):
    i = pl.program_id(1)
    rows = row_off + i * tm + lax.broadcasted_iota(jnp.int32, (tm, 1), 0)
    is_ctx = rows < n_ctx

    def branch(gate_ref, k, o_br_ref, w_ref):
        gate = jax.nn.sigmoid(gate_ref[0] + bg_ref[k:k + 1, :])
        return gate * jnp.dot(o_br_ref[0].astype(BF16), w_ref[...], preferred_element_type=F32)

    m = branch(ga_ref, 0, ossd_ref, wssd_ref) + branch(gb_ref, 1, oswa_ref, wswa_ref) + branch(gn_ref, 2, ona_ref, wna_ref)
    y = jnp.dot(m.astype(BF16), wout_ref[...], preferred_element_type=F32)
    o_ref[0] = x_ref[0] + _row_mod(ml_ref, mc_ref, 2, is_ctx) * y


def merge(x, proj, o_ssd, o_swa, o_na, b_gate, w_ssd, w_swa, w_na, w_out, mod, n_ctx, row_off):
    bsz, l, d = x.shape
    tm = 256
    gate_blk = COL_GATE // d
    off_blk = row_off // tm

    def row(a):
        o = off_blk if a.shape[1] == l else 0
        return pl.BlockSpec((1, tm, a.shape[-1]), lambda b, i: (b, i + o, 0))

    full = lambda a: pl.BlockSpec(a.shape, lambda b, i: (0,) * a.ndim)
    return pl.pallas_call(
        functools.partial(_merge_kernel, tm=tm, n_ctx=n_ctx, row_off=row_off),
        out_shape=jax.ShapeDtypeStruct((bsz, l - row_off, d), F32),
        grid=(bsz, (l - row_off) // tm),
        in_specs=[row(x),
                  pl.BlockSpec((1, tm, d), lambda b, i: (b, i + off_blk, gate_blk)),
                  pl.BlockSpec((1, tm, d), lambda b, i: (b, i + off_blk, gate_blk + 1)),
                  pl.BlockSpec((1, tm, d), lambda b, i: (b, i + off_blk, gate_blk + 2)),
                  row(o_ssd), row(o_swa), row(o_na),
                  full(b_gate), full(w_ssd), full(w_swa), full(w_na), full(w_out),
                  pl.BlockSpec((1, 6, d), lambda b, i: (b, 0, 0)),
                  pl.BlockSpec((1, 6, d), lambda b, i: (bsz, 0, 0))],
        out_specs=pl.BlockSpec((1, tm, d), lambda b, i: (b, i, 0)),
        compiler_params=_cparams(("parallel", "parallel")),
        name="merge",
    )(x, proj, proj, proj, o_ssd, o_swa, o_na, b_gate, w_ssd, w_swa, w_na, w_out, mod, mod)


def _router_kernel(x_ref, ml_ref, mc_ref, g_ref, rw_ref, rb_ref, hid_ref, idx_ref, gate_ref, *, tm, n_ctx, row_off):
    i = pl.program_id(1)
    rows = row_off + i * tm + lax.broadcasted_iota(jnp.int32, (tm, 1), 0)
    is_ctx = rows < n_ctx
    h = _rms_mod(x_ref[0], g_ref[...], _row_mod(ml_ref, mc_ref, 4, is_ctx), _row_mod(ml_ref, mc_ref, 3, is_ctx))
    hid_ref[0] = h
    logits = jnp.dot(h.astype(BF16), rw_ref[...], preferred_element_type=F32) + rb_ref[...]
    lane = lax.broadcasted_iota(jnp.int32, logits.shape, 1)
    vals, idxs = [], []
    for _ in range(TOP_K):
        m = jnp.max(logits, axis=-1, keepdims=True)
        sel = jnp.min(jnp.where(logits == m, lane, N_EXPERTS), axis=-1, keepdims=True)
        vals.append(m)
        idxs.append(sel)
        logits = jnp.where(lane == sel, -jnp.inf, logits)
    ex = [jnp.exp(v - vals[0]) for v in vals]
    denom = ex[0] + ex[1] + ex[2] + ex[3]
    col = lax.broadcasted_iota(jnp.int32, (tm, TOP_K), 1)
    idx_out = jnp.zeros((tm, TOP_K), jnp.int32)
    gate_out = jnp.zeros((tm, TOP_K), F32)
    for k in range(TOP_K):
        idx_out = jnp.where(col == k, idxs[k], idx_out)
        gate_out = jnp.where(col == k, ex[k] / denom, gate_out)
    idx_ref[0] = idx_out
    gate_ref[0] = gate_out


def moe_router(x, mod, g, router_w, router_b, n_ctx, row_off):
    bsz, l, d = x.shape
    tm = 256
    nrow = (l - row_off) // tm
    off_blk = row_off // tm
    return pl.pallas_call(
        functools.partial(_router_kernel, tm=tm, n_ctx=n_ctx, row_off=row_off),
        out_shape=(jax.ShapeDtypeStruct((bsz, l - row_off, d), F32),
                   jax.ShapeDtypeStruct((bsz, l - row_off, TOP_K), jnp.int32),
                   jax.ShapeDtypeStruct((bsz, l - row_off, TOP_K), F32)),
        grid=(bsz, nrow),
        in_specs=[pl.BlockSpec((1, tm, d), lambda b, i: (b, i + off_blk, 0)),
                  pl.BlockSpec((1, 6, d), lambda b, i: (b, 0, 0)),
                  pl.BlockSpec((1, 6, d), lambda b, i: (bsz, 0, 0)),
                  pl.BlockSpec((1, d), lambda b, i: (0, 0)),
                  pl.BlockSpec((d, N_EXPERTS), lambda b, i: (0, 0)),
                  pl.BlockSpec((1, N_EXPERTS), lambda b, i: (0, 0))],
        out_specs=(pl.BlockSpec((1, tm, d), lambda b, i: (b, i, 0)),
                   pl.BlockSpec((1, tm, TOP_K), lambda b, i: (b, i, 0)),
                   pl.BlockSpec((1, tm, TOP_K), lambda b, i: (b, i, 0))),
        compiler_params=_cparams(("parallel", "parallel")),
        name="moe_router",
    )(x, mod, mod, g.reshape(1, d), router_w.astype(BF16), router_b.reshape(1, N_EXPERTS))


def _expert_kernel(be_ref, nu_ref, idx_hbm, hid_hbm, w1_ref, b1_ref, w2_ref, b2_ref, out_hbm,
                   idx_smem, xbuf, ybuf, w1b_ref, w2b_ref, idx_sem, in_sem, out_sem, *, n_real_rows):
    i = pl.program_id(0)
    n_used = nu_ref[0]
    bm = xbuf.shape[1]
    n_idx_slots = idx_smem.shape[0]
    de = w2_ref.shape[2]
    s = i % 2

    def idx_copy(blk):
        sl = blk % n_idx_slots
        return pltpu.make_async_copy(idx_hbm.at[blk], idx_smem.at[sl], idx_sem.at[sl])

    def row_in(blk, r):
        tok = idx_smem[blk % n_idx_slots, r]
        sl = blk % 2
        return pltpu.make_async_copy(hid_hbm.at[pl.ds(tok, 1)], xbuf.at[sl, pl.ds(r, 1)], in_sem.at[sl])

    def row_out(blk, r):
        dst = idx_smem[blk % n_idx_slots, bm + r]
        sl = blk % 2
        return pltpu.make_async_copy(ybuf.at[sl, pl.ds(r, 1)], out_hbm.at[pl.ds(dst, 1)], out_sem.at[sl])

    def rows_in_wait(blk):
        sl = blk % 2
        pltpu.make_async_copy(hid_hbm.at[pl.ds(0, bm)], xbuf.at[sl], in_sem.at[sl]).wait()

    def rows_out_wait(blk):
        sl = blk % 2
        pltpu.make_async_copy(ybuf.at[sl], out_hbm.at[pl.ds(0, bm)], out_sem.at[sl]).wait()

    def for_rows(copy_of_row):
        def group(gi, carry):
            for u in range(ROW_UNROLL):
                copy_of_row(gi * ROW_UNROLL + u).start(priority=u % 2)
            return carry
        lax.fori_loop(0, bm // ROW_UNROLL, group, 0)

    def experts(issue_rows):
        x = xbuf[s].astype(BF16)
        n_chunks = EXPERT_CHUNKS
        cw = 2 * de // n_chunks
        rows_per = bm // n_chunks
        gu = []
        for cix in range(n_chunks):
            issue_rows(cix * rows_per, (cix + 1) * rows_per)
            cols = slice(cix * cw, (cix + 1) * cw)
            gu.append(jnp.dot(x, w1b_ref[:, cols], preferred_element_type=F32) + b1_ref[0, 0, :, cols])
        half = n_chunks // 2
        acts = []
        for cix in range(half):
            gate = jnp.minimum(gu[cix], SWIGLU_LIMIT)
            up = jnp.clip(gu[half + cix], -SWIGLU_LIMIT, SWIGLU_LIMIT)
            acts.append(((up + 1.0) * gate * jax.nn.sigmoid(SWIGLU_ALPHA * gate)).astype(BF16))
        act = jnp.concatenate(acts, axis=1)
        ybuf[s] = jnp.dot(act, w2b_ref[...], preferred_element_type=F32) + b2_ref[0, 0]

    @pl.when(i == 0)
    def _():
        ybuf[1] = jnp.zeros(ybuf.shape[1:], F32)
        fill = pltpu.make_async_copy(ybuf.at[1], out_hbm.at[pl.ds(n_real_rows, bm)], out_sem.at[1])
        fill.start()
        fill.wait()
        idx_copy(0).start()
        idx_copy(0).wait()
        for_rows(lambda r: row_in(0, r))

        @pl.when(n_used > 1)
        def _():
            idx_copy(1).start()

    @pl.when(jnp.logical_and(i >= 2, i <= n_used))
    def _():
        rows_out_wait(i - 2)

    @pl.when(i + 1 < n_used)
    def _():
        idx_copy(i + 1).wait()

        @pl.when(i + 2 < n_used)
        def _():
            idx_copy(i + 2).start()

    @pl.when(jnp.logical_and(i < n_used, jnp.logical_or(i == 0, be_ref[i] != be_ref[jnp.maximum(i - 1, 0)])))
    def _():
        w1b_ref[...] = w1_ref[0, 0].astype(BF16)
        w2b_ref[...] = w2_ref[0, 0].astype(BF16)

    @pl.when(i < n_used)
    def _():
        rows_in_wait(i)

    steady = jnp.logical_and(i >= 1, i + 1 < n_used)

    @pl.when(steady)
    def _():
        def issue_rows(lo, hi):
            for r in range(lo, hi):
                row_in(i + 1, r).start(priority=r % 2)
            for r in range(lo, hi):
                row_out(i - 1, r).start(priority=r % 2)
        experts(issue_rows)

    @pl.when(jnp.logical_not(steady))
    def _():
        @pl.when(i + 1 < n_used)
        def _():
            for_rows(lambda r: row_in(i + 1, r))

        @pl.when(jnp.logical_and(i >= 1, i <= n_used))
        def _():
            for_rows(lambda r: row_out(i - 1, r))

        @pl.when(i < n_used)
        def _():
            experts(lambda lo, hi: None)

        @pl.when(i == n_used)
        def _():
            rows_out_wait(i - 1)


def moe_experts(hid, idx, block_e, n_used, w1, b1, w2, b2, layer):
    t, d = hid.shape
    n_blocks = idx.shape[0]
    bm = idx.shape[1] // 2
    _, n_exp, _, two_de = w1.shape
    de = two_de // 2
    any_spec = pl.BlockSpec(memory_space=pl.ANY)
    return pl.pallas_call(
        functools.partial(_expert_kernel, n_real_rows=TOP_K * t),
        out_shape=jax.ShapeDtypeStruct((TOP_K * t + bm, d), F32),
        grid_spec=pltpu.PrefetchScalarGridSpec(
            num_scalar_prefetch=2,
            grid=(n_blocks,),
            in_specs=[any_spec, any_spec,
                      pl.BlockSpec((1, 1, d, two_de), lambda i, be, nu: (layer, be[i], 0, 0)),
                      pl.BlockSpec((1, 1, 1, two_de), lambda i, be, nu: (layer, be[i], 0, 0)),
                      pl.BlockSpec((1, 1, de, d), lambda i, be, nu: (layer, be[i], 0, 0)),
                      pl.BlockSpec((1, 1, 1, d), lambda i, be, nu: (layer, be[i], 0, 0))],
            out_specs=any_spec,
            scratch_shapes=[pltpu.SMEM((4, 2 * bm), jnp.int32),
                            pltpu.VMEM((2, bm, d), F32),
                            pltpu.VMEM((2, bm, d), F32),
                            pltpu.VMEM((d, two_de), BF16),
                            pltpu.VMEM((de, d), BF16),
                            pltpu.SemaphoreType.DMA((4,)),
                            pltpu.SemaphoreType.DMA((2,)),
                            pltpu.SemaphoreType.DMA((2,))]),
        compiler_params=_cparams(("arbitrary",)),
        name="moe_experts",
    )(block_e, n_used, idx, hid, w1, b1.reshape(b1.shape[0], n_exp, 1, two_de), w2, b2.reshape(b2.shape[0], n_exp, 1, d))


def _combine_kernel(x_ref, y0_ref, y1_ref, y2_ref, y3_ref, gate_ref, ml_ref, mc_ref, o_ref, *, tm, n_ctx, row_off):
    i = pl.program_id(1)
    rows = row_off + i * tm + lax.broadcasted_iota(jnp.int32, (tm, 1), 0)
    is_ctx = rows < n_ctx
    gates = gate_ref[0]
    y = y0_ref[...] * gates[:, 0:1]
    for k, y_ref in enumerate((y1_ref, y2_ref, y3_ref), start=1):
        y = y + y_ref[...] * gates[:, k:k + 1]
    o_ref[0] = x_ref[0] + _row_mod(ml_ref, mc_ref, 5, is_ctx) * y


def moe_combine(x, ys, gates, mod, n_ctx, row_off):
    bsz, l, d = x.shape
    tm = 256
    nrow = (l - row_off) // tm
    off_blk = row_off // tm
    plane = lambda k: pl.BlockSpec((tm, d), lambda b, i: ((k * bsz + b) * nrow + i, 0))
    return pl.pallas_call(
        functools.partial(_combine_kernel, tm=tm, n_ctx=n_ctx, row_off=row_off),
        out_shape=jax.ShapeDtypeStruct((bsz, l - row_off, d), F32),
        grid=(bsz, nrow),
        in_specs=[pl.BlockSpec((1, tm, d), lambda b, i: (b, i + off_blk, 0)),
                  plane(0), plane(1), plane(2), plane(3),
                  pl.BlockSpec((1, tm, TOP_K), lambda b, i: (b, i, 0)),
                  pl.BlockSpec((1, 6, d), lambda b, i: (b, 0, 0)),
                  pl.BlockSpec((1, 6, d), lambda b, i: (bsz, 0, 0))],
        out_specs=pl.BlockSpec((1, tm, d), lambda b, i: (b, i, 0)),
        compiler_params=_cparams(("parallel", "parallel")),
        name="moe_combine",
    )(x, ys, ys, ys, ys, gates, mod, mod)


def moe_layer(x, mod, g, router_w, router_b, w1, b1, w2, b2, layer, n_ctx, row_off):
    bsz, l, d = x.shape
    hid, top_i, gates = moe_router(x, mod, g, router_w, router_b, n_ctx, row_off)
    t = bsz * (l - row_off)
    bm = MOE_BLOCK_ROWS
    n_blocks = (t * TOP_K + bm - 1) // bm + N_EXPERTS
    flat_e = top_i.reshape(-1)
    order = jnp.argsort(flat_e).astype(jnp.int32)
    counts = jnp.sum((flat_e[:, None] == jnp.arange(N_EXPERTS)[None, :]).astype(jnp.int32), axis=0)
    starts = jnp.cumsum(counts) - counts
    padded = (counts + bm - 1) // bm * bm
    pends = jnp.cumsum(padded)
    pstarts = pends - padded
    blk_first = jnp.arange(n_blocks, dtype=jnp.int32) * bm
    in_expert = jnp.logical_and(blk_first[:, None] >= pstarts[None, :], blk_first[:, None] < pends[None, :])
    pick = lambda v: jnp.sum(jnp.where(in_expert, v[None, :], 0), axis=1)
    block_e = jnp.minimum(jnp.sum((blk_first[:, None] >= pends[None, :]).astype(jnp.int32), axis=1), N_EXPERTS - 1)
    off = blk_first - pick(pstarts)
    n_valid = jnp.clip(pick(counts) - off, 0, bm).astype(jnp.int32)
    n_used = (pends[-1] // bm).astype(jnp.int32).reshape(1)
    r = jnp.arange(bm, dtype=jnp.int32)[None, :]
    valid = r < n_valid[:, None]
    pos = jnp.where(valid, (pick(starts) + off)[:, None] + r, 0)
    src = order[pos]
    tok, kk = src // TOP_K, src % TOP_K
    dst = jnp.where(valid, kk * t + tok, TOP_K * t + r)
    idx = jnp.concatenate([tok, dst], axis=1).astype(jnp.int32)
    ys = moe_experts(hid.reshape(t, d), idx, block_e.astype(jnp.int32), n_used, w1, b1, w2, b2, layer)
    return moe_combine(x, ys, gates, mod, n_ctx, row_off)


def _final_norm_kernel(x_ref, g_ref, o_ref):
    x = x_ref[0]
    o_ref[0] = (x * lax.rsqrt(jnp.mean(x * x, axis=-1, keepdims=True) + NORM_EPS)) * g_ref[...]


def final_norm(x, g):
    bsz, l, d = x.shape
    tm = 512
    return pl.pallas_call(
        _final_norm_kernel,
        out_shape=jax.ShapeDtypeStruct((bsz, l, d), F32),
        grid=(bsz, l // tm),
        in_specs=[pl.BlockSpec((1, tm, d), lambda b, i: (b, i, 0)), pl.BlockSpec((1, d), lambda b, i: (0, 0))],
        out_specs=pl.BlockSpec((1, tm, d), lambda b, i: (b, i, 0)),
        compiler_params=_cparams(("parallel", "parallel")),
        name="final_norm",
    )(x, g.reshape(1, d))


ATT_BLOCK = 128
NEG_BIG = -1e30


def _qk(q, k):
    return lax.dot_general(q, k, (((1,), (1,)), ((), ())), preferred_element_type=F32)


def _pair_queries(q2, lane):
    return jnp.concatenate([jnp.where(lane < HEAD_DIM, q2, 0.0), jnp.where(lane < HEAD_DIM, 0.0, q2)], axis=0).astype(BF16)


def _pair_outputs(acc, lane):
    r = acc.shape[0] // 2
    return jnp.where(lane < HEAD_DIM, acc[:r], acc[r:])


def _softmax_pv(scores, values, extra_logit=None):
    m = functools.reduce(jnp.maximum, [jnp.max(s, axis=-1, keepdims=True) for s in scores])
    if extra_logit is not None:
        m = jnp.maximum(m, extra_logit)
    ps = [jnp.exp(s - m) for s in scores]
    den = functools.reduce(jnp.add, [jnp.sum(p, axis=-1, keepdims=True) for p in ps])
    if extra_logit is not None:
        den = den + jnp.exp(extra_logit - m)
    acc = functools.reduce(jnp.add, [jnp.dot(p.astype(BF16), v, preferred_element_type=F32) for p, v in zip(ps, values)])
    return acc / den


def _na_kernel(q_ref, k_ref, v_ref, bias_ref, o_ref, *, n_ctx, blk_off, n_dbl):
    blk = pl.program_id(1) + blk_off
    n_ctx_blk = n_ctx // ATT_BLOCK
    s2 = jnp.clip(blk - n_ctx_blk - 2, 0, n_dbl - 5)
    start = pl.multiple_of(n_ctx + s2 * ATT_BLOCK, ATT_BLOCK)
    lane = lax.broadcasted_iota(jnp.int32, (1, 2 * HEAD_DIM), 1)
    scale = HEAD_DIM ** -0.5

    def pair_out(p, local):
        cs = slice(p * 2 * HEAD_DIM, (p + 1) * 2 * HEAD_DIM)
        q2 = q_ref[0, :, cs] * scale
        kc = k_ref[0, 0:n_ctx, cs].astype(BF16)
        vc = v_ref[0, 0:n_ctx, cs].astype(BF16)
        if local:
            kl = k_ref[0, pl.ds(start, 5 * ATT_BLOCK), cs].astype(BF16)
            vl = v_ref[0, pl.ds(start, 5 * ATT_BLOCK), cs].astype(BF16)
        qp = _pair_queries(q2, lane)
        scores, values = [_qk(qp, kc)], [vc]
        if local:
            scores.append(_qk(qp, kl) + bias_ref[0, p])
            values.append(vl)
        o_ref[0, :, cs] = _pair_outputs(_softmax_pv(scores, values), lane)

    @pl.when(blk < n_ctx_blk)
    def _():
        for p in range(NA_HEADS // 2):
            pair_out(p, False)

    @pl.when(blk >= n_ctx_blk)
    def _():
        for p in range(NA_HEADS // 2):
            pair_out(p, True)


def _na_bias_tables(rpb, rows):
    n_dbl = rows // 2
    wr = min(NA_WIN_R, rows)
    cidx = np.arange(GRID_W)
    c0 = np.clip(cidx - NA_WIN_C // 2, 0, GRID_W - NA_WIN_C)
    cvalid = (cidx[None, :] >= c0[:, None]) & (cidx[None, :] < c0[:, None] + NA_WIN_C)
    coff = np.clip(cidx[None, :] - cidx[:, None] + (NA_WIN_C - 1), 0, 2 * NA_WIN_C - 2)
    col_onehot = (coff[:, :, None] == np.arange(2 * NA_WIN_C - 1)[None, None, :]) & cvalid[:, :, None]
    pats = (0, 1, 2, n_dbl - 2, n_dbl - 1)
    row_onehot = np.zeros((len(pats), 2, 10, 2 * NA_WIN_R - 1), np.float32)
    for pi, r2 in enumerate(pats):
        s2 = int(np.clip(r2 - 2, 0, n_dbl - 5))
        for qh in range(2):
            qr = 2 * r2 + qh
            r0 = int(np.clip(qr - wr // 2, 0, rows - wr))
            for kk in range(10):
                kr = 2 * s2 + kk
                if r0 <= kr < r0 + wr:
                    row_onehot[pi, qh, kk, kr - qr + NA_WIN_R - 1] = 1.0
    t1 = jnp.einsum('hrc,qkc->hrqk', rpb, jnp.asarray(col_onehot, F32), precision=lax.Precision.HIGHEST)
    bias = jnp.einsum('pajr,hrqk->phaqjk', jnp.asarray(row_onehot), t1, precision=lax.Precision.HIGHEST)
    valid = (row_onehot.sum(-1)[:, None, :, None, :, None] > 0) & cvalid[None, None, None, :, None, :]
    bias = jnp.where(jnp.asarray(valid), bias, NEG_BIG)
    return bias.reshape(len(pats), rpb.shape[0] // 2, 4 * GRID_W, 10 * GRID_W)


def na_attention(proj, rpb, n_ctx, blk_off):
    bsz, l, _ = proj.shape
    rows = (l - n_ctx) // GRID_W
    n_dbl = rows // 2
    n_ctx_blk = n_ctx // ATT_BLOCK
    bias = _na_bias_tables(rpb, rows)

    def bias_idx(b, i):
        r2 = i + blk_off - n_ctx_blk
        pat = jnp.where(r2 < 2, r2, jnp.where(r2 >= n_dbl - 2, r2 - (n_dbl - 5), 2))
        return (jnp.clip(pat, 0, 4), 0, 0, 0)

    return pl.pallas_call(
        functools.partial(_na_kernel, n_ctx=n_ctx, blk_off=blk_off, n_dbl=n_dbl),
        out_shape=jax.ShapeDtypeStruct((bsz, l - blk_off * ATT_BLOCK, NA_DIM), F32),
        grid=(bsz, l // ATT_BLOCK - blk_off),
        in_specs=[pl.BlockSpec((1, ATT_BLOCK, NA_DIM), lambda b, i: (b, i + blk_off, COL_QN // NA_DIM)),
                  pl.BlockSpec((1, l, NA_DIM), lambda b, i: (b, 0, COL_KN // NA_DIM)),
                  pl.BlockSpec((1, l, NA_DIM), lambda b, i: (b, 0, COL_VN // NA_DIM)),
                  pl.BlockSpec((1, NA_HEADS // 2, 2 * ATT_BLOCK, 5 * ATT_BLOCK), bias_idx)],
        out_specs=pl.BlockSpec((1, ATT_BLOCK, NA_DIM), lambda b, i: (b, i, 0)),
        compiler_params=_cparams(("parallel", "arbitrary")),
        name="na_attention",
    )(proj, proj, proj, bias)


def _swa_kernel(sink_ref, q_ref, k_ref, v_ref, cq_ref, sq_ref, ck_ref, sk_ref, o_ref, *, n_ctx, blk_off, n_blk):
    blk = pl.program_id(1) + blk_off
    n_ctx_blk = n_ctx // ATT_BLOCK
    n = blk - n_ctx_blk
    first = jnp.clip(n - 1, 0, n_blk - 3)
    kstart = pl.multiple_of(first * ATT_BLOCK, ATT_BLOCK)
    lane = lax.broadcasted_iota(jnp.int32, (1, 2 * HEAD_DIM), 1)
    scale = HEAD_DIM ** -0.5
    quarter = HEAD_DIM // 4

    def rope(u, cos, sin):
        w = u.shape[-1]
        li = lax.broadcasted_iota(jnp.int32, (1, w), 1)
        swapped = jnp.where(li % (2 * quarter) < quarter, pltpu.roll(u, w - quarter, 1), pltpu.roll(u, quarter, 1))
        return u * cos + swapped * sin

    def dup_head(x, h):
        other = pltpu.roll(x, HEAD_DIM, 1)
        return jnp.where((lane < HEAD_DIM) == (h == 0), x, other)

    def run(local):
        q = q_ref[0] * scale
        kc_all = k_ref[0, 0:n_ctx, :]
        vc_all = v_ref[0, 0:n_ctx, :]
        if local:
            q = rope(q, cq_ref[...], sq_ref[...])
            kl_all = rope(k_ref[0, pl.ds(n_ctx + kstart, 3 * ATT_BLOCK), :],
                          ck_ref[pl.ds(kstart, 3 * ATT_BLOCK), :], sk_ref[pl.ds(kstart, 3 * ATT_BLOCK), :])
            vl_all = v_ref[0, pl.ds(n_ctx + kstart, 3 * ATT_BLOCK), :]
            qrow = lax.broadcasted_iota(jnp.int32, (2 * ATT_BLOCK, 3 * ATT_BLOCK), 0) % ATT_BLOCK
            kpos = kstart + lax.broadcasted_iota(jnp.int32, (2 * ATT_BLOCK, 3 * ATT_BLOCK), 1)
            band = jnp.where(jnp.abs(kpos - (n * ATT_BLOCK + qrow)) <= SWA_WINDOW, 0.0, NEG_BIG)
        group = SWA_Q_HEADS // SWA_KV_HEADS
        first_head = lax.broadcasted_iota(jnp.int32, (2 * ATT_BLOCK, 1), 0) < ATT_BLOCK
        for h in range(SWA_KV_HEADS):
            kc = dup_head(kc_all, h).astype(BF16)
            vc = dup_head(vc_all, h).astype(BF16)
            if local:
                kl = dup_head(kl_all, h).astype(BF16)
                vl = dup_head(vl_all, h).astype(BF16)
            for p in range(h * group // 2, (h + 1) * group // 2):
                cs = slice(p * 2 * HEAD_DIM, (p + 1) * 2 * HEAD_DIM)
                qp = _pair_queries(q[:, cs], lane)
                scores, values = [_qk(qp, kc)], [vc]
                if local:
                    scores.append(_qk(qp, kl) + band)
                    values.append(vl)
                sink = jnp.where(first_head, sink_ref[2 * p], sink_ref[2 * p + 1])
                o_ref[0, :, cs] = _pair_outputs(_softmax_pv(scores, values, extra_logit=sink), lane)

    @pl.when(blk < n_ctx_blk)
    def _():
        run(False)

    @pl.when(blk >= n_ctx_blk)
    def _():
        run(True)


def _rope_tables(s):
    pos = np.arange(s)
    quarter = HEAD_DIM // 4
    inv = ROPE_THETA ** (-jnp.arange(0, 2 * quarter, 2, dtype=F32) / (2 * quarter))

    def axis_tables(p):
        ang = jnp.asarray(p, F32)[:, None] * inv[None, :]
        c, sn = jnp.cos(ang), jnp.sin(ang)
        return jnp.concatenate([c, c], axis=1), jnp.concatenate([-sn, sn], axis=1)
    cr, sr = axis_tables(pos // GRID_W)
    cc, sc = axis_tables(pos % GRID_W)
    return jnp.concatenate([cr, cc], axis=1), jnp.concatenate([sr, sc], axis=1)


def swa_attention(proj, sink, n_ctx, blk_off):
    bsz, l, _ = proj.shape
    s = l - n_ctx
    n_blk = s // ATT_BLOCK
    n_ctx_blk = n_ctx // ATT_BLOCK
    cos, sin = _rope_tables(s)
    cq, sq = jnp.tile(cos, (1, SWA_Q_HEADS)), jnp.tile(sin, (1, SWA_Q_HEADS))
    ck, sk = jnp.tile(cos, (1, SWA_KV_HEADS)), jnp.tile(sin, (1, SWA_KV_HEADS))
    qtab = lambda b, i: (jnp.maximum(i + blk_off - n_ctx_blk, 0), 0)
    return pl.pallas_call(
        functools.partial(_swa_kernel, n_ctx=n_ctx, blk_off=blk_off, n_blk=n_blk),
        out_shape=jax.ShapeDtypeStruct((bsz, l - blk_off * ATT_BLOCK, SWA_Q), F32),
        grid=(bsz, l // ATT_BLOCK - blk_off),
        in_specs=[pl.BlockSpec(memory_space=pltpu.SMEM),
                  pl.BlockSpec((1, ATT_BLOCK, SWA_Q), lambda b, i: (b, i + blk_off, COL_QS // SWA_Q)),
                  pl.BlockSpec((1, l, SWA_KV), lambda b, i: (b, 0, COL_KS // SWA_KV)),
                  pl.BlockSpec((1, l, SWA_KV), lambda b, i: (b, 0, COL_VS // SWA_KV)),
                  pl.BlockSpec((ATT_BLOCK, SWA_Q), qtab),
                  pl.BlockSpec((ATT_BLOCK, SWA_Q), qtab),
                  pl.BlockSpec((s, SWA_KV), lambda b, i: (0, 0)),
                  pl.BlockSpec((s, SWA_KV), lambda b, i: (0, 0))],
        out_specs=pl.BlockSpec((1, ATT_BLOCK, SWA_Q), lambda b, i: (b, i, 0)),
        compiler_params=_cparams(("parallel", "arbitrary")),
        name="swa_attention",
    )(sink, proj, proj, proj, cq, sq, ck, sk)


CONV_HALO = 8


def _ssd_kernel(*refs, reverse, n_ctx, n_chunks):
    if reverse:
        xsc_ref, bcc_ref, dt_ref, dtbias_ref, acoef_ref, z_ref, yf_ref, dskip_ref, ng_ref, o_ref, state_ref = refs
    else:
        (xs_c, xs_p, xs_n, bc_c, bc_p, bc_n, dt_ref, cwx_ref, cbx_ref, cwb_ref, cbb_ref, dtbias_ref, acoef_ref,
         o_ref, xsc_ref, bcc_ref, state_ref) = refs
    q = SSD_CHUNK
    n_st = SSD_STATE
    gw = SSD_INNER // SSD_GROUPS
    j = pl.program_id(1)
    c = _ssd_chunk_of_step(j, reverse, n_ctx // q, n_chunks)
    n_cc = n_ctx // q
    seg_first = jnp.logical_or(c == 0, c == n_cc)
    seg_last = jnp.logical_or(c == n_cc - 1, c == n_chunks - 1)

    @pl.when(j == 0)
    def _():
        state_ref[...] = jnp.zeros_like(state_ref)

    def conv_silu(cur_ref, prev_ref, next_ref, w_ref, b_ref):
        half = w_ref.shape[0] // 2
        prev = jnp.where(seg_first, 0.0, prev_ref[0])
        nxt = jnp.where(seg_last, 0.0, next_ref[0])
        ext = jnp.concatenate([prev, cur_ref[0], nxt], axis=0)
        acc = b_ref[...]
        for k in range(w_ref.shape[0]):
            o = CONV_HALO - half + k
            acc = acc + ext[o:o + q] * w_ref[k:k + 1, :]
        return acc * jax.nn.sigmoid(acc)

    if reverse:
        xs, bcv = xsc_ref[0], bcc_ref[0]
    else:
        xs = conv_silu(xs_c, xs_p, xs_n, cwx_ref, cbx_ref)
        bcv = conv_silu(bc_c, bc_p, bc_n, cwb_ref, cbb_ref)
        xsc_ref[0] = xs
        bcc_ref[0] = bcv

    lo = SSD_HEADS if reverse else 0
    dt_raw = dt_ref[0] + dtbias_ref[...]
    dt_all = jnp.maximum(dt_raw, 0.0) + jnp.log(1.0 + jnp.exp(-jnp.abs(dt_raw)))
    a_all = dt_all * acoef_ref[...]
    ri = lax.broadcasted_iota(jnp.int32, (q, q), 0)
    ci = lax.broadcasted_iota(jnp.int32, (q, q), 1)
    causal = (ci >= ri) if reverse else (ci <= ri)
    a_cs = jnp.dot(causal.astype(F32), a_all, precision=lax.Precision.HIGHEST, preferred_element_type=F32)
    total = a_cs[0:1] if reverse else a_cs[q - 1:q]
    a_cs_t = a_cs.T

    lane = lax.broadcasted_iota(jnp.int32, (1, 2 * SSD_HEAD_DIM), 1)

    def expand(v):
        parts = [jnp.where(lane < SSD_HEAD_DIM, v[:, lo + 2 * p:lo + 2 * p + 1], v[:, lo + 2 * p + 1:lo + 2 * p + 2])
                 for p in range(SSD_HEADS // 2)]
        return jnp.concatenate(parts, axis=1)

    xdt = xs * expand(dt_all)
    dec_start = expand(jnp.exp(a_cs))
    dec_end = expand(jnp.exp(total - a_cs))
    dec_total = expand(jnp.exp(total))

    heads_per_group = SSD_HEADS // SSD_GROUPS
    y_parts = []
    for g in range(SSD_GROUPS):
        b_g = bcv[:, g * n_st:(g + 1) * n_st]
        c_g = bcv[:, (SSD_GROUPS + g) * n_st:(SSD_GROUPS + g + 1) * n_st].astype(BF16)
        cb = _qk(c_g, b_g.astype(BF16))
        cols = slice(g * gw, (g + 1) * gw)
        st = state_ref[:, cols]
        y_off = jnp.dot(c_g, st.astype(BF16), preferred_element_type=F32) * dec_start[:, cols]
        diag = []
        for p in range(g * heads_per_group // 2, (g + 1) * heads_per_group // 2):
            x2 = xdt[:, p * 2 * SSD_HEAD_DIM:(p + 1) * 2 * SSD_HEAD_DIM].astype(BF16)
            outs = []
            for hh in range(2):
                h = lo + 2 * p + hh
                seg = jnp.where(causal, a_cs[:, h:h + 1] - a_cs_t[h:h + 1, :], NEG_BIG)
                m = (cb * jnp.exp(seg)).astype(BF16)
                outs.append(jnp.dot(m, x2, preferred_element_type=F32))
            diag.append(jnp.where(lane < SSD_HEAD_DIM, outs[0], outs[1]))
        y_parts.append(jnp.concatenate(diag, axis=1) + y_off)
        xw = (xdt[:, cols] * dec_end[:, cols]).astype(BF16)
        state_ref[:, cols] = dec_total[:, cols] * st + jnp.dot(b_g.T.astype(BF16), xw, preferred_element_type=F32)
    y = jnp.concatenate(y_parts, axis=1)

    if reverse:
        y = yf_ref[0] + y + dskip_ref[...] * xs
        z = z_ref[0]
        y = y * (z * jax.nn.sigmoid(z))
        o_ref[0] = (y * lax.rsqrt(jnp.mean(y * y, axis=-1, keepdims=True) + NORM_EPS)) * ng_ref[...]
    else:
        o_ref[0] = y


def _ssd_chunk_of_step(j, reverse, n_ctx_chunks, n_chunks):
    if not reverse:
        return j
    return jnp.where(j < n_ctx_chunks, n_ctx_chunks - 1 - j, n_chunks - 1 - (j - n_ctx_chunks))


def ssd_mixer(proj, conv_w, conv_b, a_log, dt_bias, d_skip, norm_g, n_ctx):
    bsz, l, _ = proj.shape
    q = SSD_CHUNK
    n_chunks = l // q
    n_cc = n_ctx // q
    hb = q // CONV_HALO
    pad = lambda v: jnp.concatenate([v.reshape(1, -1), jnp.zeros((1, 128 - v.size), F32)], axis=1)
    dtb = pad(dt_bias)
    acoef = pad(-jnp.exp(a_log))
    cwx, cwb = conv_w[:, :SSD_INNER], conv_w[:, SSD_INNER:]
    cbx, cbb = conv_b[:SSD_INNER].reshape(1, -1), conv_b[SSD_INNER:].reshape(1, -1)
    wbc = cwb.shape[1]

    dsk = jnp.repeat(d_skip, SSD_HEAD_DIM).reshape(1, SSD_INNER)
    ng = norm_g.reshape(1, SSD_INNER)
    full = lambda a: pl.BlockSpec(a.shape, lambda b, jj: (0, 0))

    def call(reverse, inputs, in_specs, out_widths):
        chunk = lambda jj: _ssd_chunk_of_step(jj, reverse, n_cc, n_chunks)
        cur = lambda w, off: pl.BlockSpec((1, q, w), lambda b, jj: (b, chunk(jj), off // w))
        prev = lambda w, off: pl.BlockSpec((1, CONV_HALO, w), lambda b, jj: (b, jnp.maximum(chunk(jj) * hb - 1, 0), off // w))
        nxt = lambda w, off: pl.BlockSpec((1, CONV_HALO, w),
                                          lambda b, jj: (b, jnp.minimum((chunk(jj) + 1) * hb, l // CONV_HALO - 1), off // w))
        return pl.pallas_call(
            functools.partial(_ssd_kernel, reverse=reverse, n_ctx=n_ctx, n_chunks=n_chunks),
            out_shape=tuple(jax.ShapeDtypeStruct((bsz, l, w), F32) for w in out_widths),
            grid=(bsz, n_chunks),
            in_specs=in_specs(cur, prev, nxt),
            out_specs=tuple(cur(w, 0) for w in out_widths),
            scratch_shapes=[pltpu.VMEM((SSD_STATE, SSD_INNER), F32)],
            compiler_params=_cparams(("parallel", "arbitrary")),
            name="ssd_bwd" if reverse else "ssd_fwd",
        )(*inputs)

    y_f, xs_conv, bc_conv = call(
        False, (proj, proj, proj, proj, proj, proj, proj, cwx, cbx, cwb, cbb, dtb, acoef),
        lambda cur, prev, nxt: [cur(SSD_INNER, COL_XS), prev(SSD_INNER, COL_XS), nxt(SSD_INNER, COL_XS),
                                cur(wbc, COL_BC), prev(wbc, COL_BC), nxt(wbc, COL_BC),
                                cur(128, COL_DT), full(cwx), full(cbx), full(cwb), full(cbb), full(dtb), full(acoef)],
        (SSD_INNER, SSD_INNER, wbc))
    (out,) = call(
        True, (xs_conv, bc_conv, proj, dtb, acoef, proj, y_f, dsk, ng),
        lambda cur, prev, nxt: [cur(SSD_INNER, 0), cur(wbc, 0), cur(128, COL_DT), full(dtb), full(acoef),
                                cur(SSD_INNER, COL_Z), cur(SSD_INNER, 0), full(dsk), full(ng)],
        (SSD_INNER,))
    return out


def _pack_w_in(w):
    o = 0
    segs = {}
    for name, width in (('z', SSD_INNER), ('xs', SSD_INNER), ('bc', 512), ('dt', 2 * SSD_HEADS), ('qs', SWA_Q),
                        ('ks', SWA_KV), ('vs', SWA_KV), ('qn', NA_DIM), ('kn', NA_DIM), ('vn', NA_DIM),
                        ('g', N_BRANCH * D_MODEL)):
        segs[name] = w[:, o:o + width]
        o += width
    pad = jnp.zeros((w.shape[0], 128 - 2 * SSD_HEADS), w.dtype)
    order = ('g', 'z', 'xs', 'bc', 'qs', 'qn', 'kn', 'vn', 'ks', 'vs', 'dt')
    return jnp.concatenate([segs[k] for k in order] + [pad], axis=1).astype(BF16)


def kernel(x, c, ctx, c_ctx, norm1_g, norm2_g, w_ada, b_ada, w_in, b_gate, ssd_conv_w, ssd_conv_b, ssd_a_log,
           ssd_dt_bias, ssd_d, ssd_norm_g, swa_sink, na_rpb, w_br_ssd, w_br_swa, w_br_na, w_out, router_w, router_b,
           moe_w1, moe_b1, moe_w2, moe_b2, final_norm_g):
    bsz, s, d = x.shape
    lc = ctx.shape[1]
    depth = w_in.shape[0]
    n_mod_rows = 16
    cond = jnp.concatenate([c, c_ctx[None, :], jnp.zeros((n_mod_rows - bsz - 1, d), F32)], axis=0)
    xs = jnp.concatenate([ctx, x], axis=1)
    for i in range(depth):
        last = i == depth - 1
        mod = ada_modulation(cond, w_ada, b_ada, i).reshape(n_mod_rows, 6, d)
        proj = norm_proj(xs, mod, norm1_g[i], _pack_w_in(w_in[i]), lc)
        o_ssd = ssd_mixer(proj, ssd_conv_w[i], ssd_conv_b[i], ssd_a_log[i], ssd_dt_bias[i], ssd_d[i], ssd_norm_g[i], lc)
        blk_off = lc // ATT_BLOCK if last else 0
        o_swa = swa_attention(proj, swa_sink[i], lc, blk_off)
        o_na = na_attention(proj, na_rpb[i], lc, blk_off)
        xs = merge(xs, proj, o_ssd, o_swa, o_na, b_gate[i].reshape(N_BRANCH, d), w_br_ssd[i].astype(BF16),
                   w_br_swa[i].astype(BF16), w_br_na[i].astype(BF16), w_out[i].astype(BF16), mod, lc,
                   lc if last else 0)
        xs = moe_layer(xs, mod, norm2_g[i], router_w[i], router_b[i], moe_w1, moe_b1, moe_w2, moe_b2, i,
                       0 if last else lc, 0)
    return final_norm(xs, final_norm_g)
```

```python
import functools

import jax
import jax.numpy as jnp
import numpy as np
from jax import lax
from jax.experimental import pallas as pl
from jax.experimental.pallas import tpu as pltpu

F32 = jnp.float32
BF16 = jnp.bfloat16

D_MODEL = 1024
LANES = 128
GRID_W = 64
HEAD_DIM = 64
NORM_EPS = 1e-6

SSD_HEADS = 16
SSD_HEAD_DIM = 64
SSD_INNER = SSD_HEADS * SSD_HEAD_DIM
SSD_GROUPS = 2
SSD_STATE = 128
SSD_CHUNK = 128

SWA_Q_HEADS = 8
SWA_KV_HEADS = 2
SWA_WINDOW = 128
SWA_Q = SWA_Q_HEADS * HEAD_DIM
SWA_KV = SWA_KV_HEADS * HEAD_DIM
ROPE_THETA = 10000.0

NA_HEADS = 8
NA_WIN_R = 8
NA_WIN_C = 16
NA_DIM = NA_HEADS * HEAD_DIM

N_BRANCH = 3
N_EXPERTS = 32
TOP_K = 4
SWIGLU_LIMIT = 7.0
SWIGLU_ALPHA = 1.702

COL_GATE = 0
COL_Z = 3 * D_MODEL
COL_XS = COL_Z + SSD_INNER
COL_BC = COL_XS + SSD_INNER
COL_QS = COL_BC + 512
COL_QN = COL_QS + SWA_Q
COL_KN = COL_QN + NA_DIM
COL_VN = COL_KN + NA_DIM
COL_KS = COL_VN + NA_DIM
COL_VS = COL_KS + SWA_KV
COL_DT = COL_VS + SWA_KV
IN_COLS_PACKED = COL_DT + 128

MOE_BLOCK_ROWS = 256
ROW_UNROLL = 8
EXPERT_CHUNKS = 8
VMEM_LIMIT = 56 * 1024 * 1024


def _cparams(sem):
    return pltpu.CompilerParams(dimension_semantics=sem, vmem_limit_bytes=VMEM_LIMIT)


def _ada_kernel(c_ref, w_ref, b_ref, o_ref):
    c = c_ref[...]
    s = (c * jax.nn.sigmoid(c)).astype(BF16)
    o_ref[...] = jnp.dot(s, w_ref[0].astype(BF16), preferred_element_type=F32) + b_ref[0]


def ada_modulation(rows, w_ada, b_ada, layer):
    r, d = rows.shape
    n = w_ada.shape[2]
    tn = 1024
    return pl.pallas_call(
        _ada_kernel,
        out_shape=jax.ShapeDtypeStruct((r, n), F32),
        grid=(n // tn,),
        in_specs=[pl.BlockSpec((r, d), lambda j: (0, 0)),
                  pl.BlockSpec((1, d, tn), lambda j: (layer, 0, j)),
                  pl.BlockSpec((1, 1, tn), lambda j: (layer, 0, j))],
        out_specs=pl.BlockSpec((r, tn), lambda j: (0, j)),
        compiler_params=_cparams(("parallel",)),
        name="ada_modulation",
    )(rows, w_ada, b_ada.reshape(b_ada.shape[0], 1, n))


def _row_mod(mod_l_ref, mod_c_ref, idx, is_ctx):
    return jnp.where(is_ctx, mod_c_ref[0, idx:idx + 1, :], mod_l_ref[0, idx:idx + 1, :])


def _rms_mod(x, g, scale, shift):
    y = x * lax.rsqrt(jnp.mean(x * x, axis=-1, keepdims=True) + NORM_EPS)
    return (y * g) * (1.0 + scale) + shift


def _store_row_tiles(ref, v):
    for j in range(v.shape[1] // LANES):
        ref[:, j, :] = v[:, j * LANES:(j + 1) * LANES]


def _load_row_tiles(ref):
    return jnp.concatenate([ref[:, j, :] for j in range(ref.shape[1])], axis=1)


def _norm_proj_kernel(x_ref, ml_ref, mc_ref, g_ref, w_ref, o_ref, h_ref, *, tm, n_ctx):
    i = pl.program_id(1)
    j = pl.program_id(2)

    @pl.when(j == 0)
    def _():
        rows = i * tm + lax.broadcasted_iota(jnp.int32, (tm, 1), 0)
        is_ctx = rows < n_ctx
        h = _rms_mod(x_ref[0], g_ref[...], _row_mod(ml_ref, mc_ref, 1, is_ctx), _row_mod(ml_ref, mc_ref, 0, is_ctx))
        h_ref[...] = h.astype(BF16)

    o_ref[0] = jnp.dot(h_ref[...], w_ref[...], preferred_element_type=F32)


def norm_proj(x, mod, g, w_packed, n_ctx):
    bsz, l, d = x.shape
    n = w_packed.shape[1]
    tm, tn = 768, 2688
    return pl.pallas_call(
        functools.partial(_norm_proj_kernel, tm=tm, n_ctx=n_ctx),
        out_shape=jax.ShapeDtypeStruct((bsz, l, n), F32),
        grid=(bsz, l // tm, n // tn),
        in_specs=[pl.BlockSpec((1, tm, d), lambda b, i, j: (b, i, 0)),
                  pl.BlockSpec((1, 6, d), lambda b, i, j: (b, 0, 0)),
                  pl.BlockSpec((1, 6, d), lambda b, i, j: (bsz, 0, 0)),
                  pl.BlockSpec((1, d), lambda b, i, j: (0, 0)),
                  pl.BlockSpec((d, tn), lambda b, i, j: (0, j))],
        out_specs=pl.BlockSpec((1, tm, tn), lambda b, i, j: (b, i, j)),
        scratch_shapes=[pltpu.VMEM((tm, d), BF16)],
        compiler_params=_cparams(("parallel", "parallel", "arbitrary")),
        name="norm_proj",
    )(x, mod, mod, g.reshape(1, d), w_packed)


def _merge_kernel(x_ref, ga_ref, gb_ref, gn_ref, ossd_ref, oswa_ref, ona_ref, bg_ref, wssd_ref, wswa_ref, wna_ref,
                  wout_ref, ml_ref, mc_ref, o_ref, *, tm, n_ctx, row_off):
    i = pl.program_id(1)
    rows = row_off + i * tm + lax.broadcasted_iota(jnp.int32, (tm, 1), 0)
    is_ctx = rows < n_ctx

    def branch(gate_ref, k, o_br_ref, w_ref):
        gate = jax.nn.sigmoid(gate_ref[0] + bg_ref[k:k + 1, :])
        return gate * jnp.dot(o_br_ref[0].astype(BF16), w_ref[...], preferred_element_type=F32)

    m = branch(ga_ref, 0, ossd_ref, wssd_ref) + branch(gb_ref, 1, oswa_ref, wswa_ref) + branch(gn_ref, 2, ona_ref, wna_ref)
    y = jnp.dot(m.astype(BF16), wout_ref[...], preferred_element_type=F32)
    o_ref[0] = x_ref[0] + _row_mod(ml_ref, mc_ref, 2, is_ctx) * y


def merge(x, proj, o_ssd, o_swa, o_na, b_gate, w_ssd, w_swa, w_na, w_out, mod, n_ctx, row_off):
    bsz, l, d = x.shape
    tm = 256
    gate_blk = COL_GATE // d
    off_blk = row_off // tm

    def row(a):
        o = off_blk if a.shape[1] == l else 0
        return pl.BlockSpec((1, tm, a.shape[-1]), lambda b, i: (b, i + o, 0))

    full = lambda a: pl.BlockSpec(a.shape, lambda b, i: (0,) * a.ndim)
    return pl.pallas_call(
        functools.partial(_merge_kernel, tm=tm, n_ctx=n_ctx, row_off=row_off),
        out_shape=jax.ShapeDtypeStruct((bsz, l - row_off, d), F32),
        grid=(bsz, (l - row_off) // tm),
        in_specs=[row(x),
                  pl.BlockSpec((1, tm, d), lambda b, i: (b, i + off_blk, gate_blk)),
                  pl.BlockSpec((1, tm, d), lambda b, i: (b, i + off_blk, gate_blk + 1)),
                  pl.BlockSpec((1, tm, d), lambda b, i: (b, i + off_blk, gate_blk + 2)),
                  row(o_ssd), row(o_swa), row(o_na),
                  full(b_gate), full(w_ssd), full(w_swa), full(w_na), full(w_out),
                  pl.BlockSpec((1, 6, d), lambda b, i: (b, 0, 0)),
                  pl.BlockSpec((1, 6, d), lambda b, i: (bsz, 0, 0))],
        out_specs=pl.BlockSpec((1, tm, d), lambda b, i: (b, i, 0)),
        compiler_params=_cparams(("parallel", "parallel")),
        name="merge",
    )(x, proj, proj, proj, o_ssd, o_swa, o_na, b_gate, w_ssd, w_swa, w_na, w_out, mod, mod)


def _router_kernel(x_ref, ml_ref, mc_ref, g_ref, rw_ref, rb_ref, hid_ref, idx_ref, gate_ref, *, tm, n_ctx, row_off):
    i = pl.program_id(1)
    rows = row_off + i * tm + lax.broadcasted_iota(jnp.int32, (tm, 1), 0)
    is_ctx = rows < n_ctx
    h = _rms_mod(x_ref[0], g_ref[...], _row_mod(ml_ref, mc_ref, 4, is_ctx), _row_mod(ml_ref, mc_ref, 3, is_ctx))
    _store_row_tiles(hid_ref.at[0], h)
    logits = jnp.dot(h.astype(BF16), rw_ref[...], preferred_element_type=F32) + rb_ref[...]
    lane = lax.broadcasted_iota(jnp.int32, logits.shape, 1)
    vals, idxs = [], []
    for _ in range(TOP_K):
        m = jnp.max(logits, axis=-1, keepdims=True)
        sel = jnp.min(jnp.where(logits == m, lane, N_EXPERTS), axis=-1, keepdims=True)
        vals.append(m)
        idxs.append(sel)
        logits = jnp.where(lane == sel, -jnp.inf, logits)
    ex = [jnp.exp(v - vals[0]) for v in vals]
    denom = ex[0] + ex[1] + ex[2] + ex[3]
    col = lax.broadcasted_iota(jnp.int32, (tm, TOP_K), 1)
    idx_out = jnp.zeros((tm, TOP_K), jnp.int32)
    gate_out = jnp.zeros((tm, TOP_K), F32)
    for k in range(TOP_K):
        idx_out = jnp.where(col == k, idxs[k], idx_out)
        gate_out = jnp.where(col == k, ex[k] / denom, gate_out)
    idx_ref[0] = idx_out
    gate_ref[0] = gate_out


def moe_router(x, mod, g, router_w, router_b, n_ctx, row_off):
    bsz, l, d = x.shape
    tm = 256
    nrow = (l - row_off) // tm
    off_blk = row_off // tm
    return pl.pallas_call(
        functools.partial(_router_kernel, tm=tm, n_ctx=n_ctx, row_off=row_off),
        out_shape=(jax.ShapeDtypeStruct((bsz, l - row_off, d // LANES, LANES), F32),
                   jax.ShapeDtypeStruct((bsz, l - row_off, TOP_K), jnp.int32),
                   jax.ShapeDtypeStruct((bsz, l - row_off, TOP_K), F32)),
        grid=(bsz, nrow),
        in_specs=[pl.BlockSpec((1, tm, d), lambda b, i: (b, i + off_blk, 0)),
                  pl.BlockSpec((1, 6, d), lambda b, i: (b, 0, 0)),
                  pl.BlockSpec((1, 6, d), lambda b, i: (bsz, 0, 0)),
                  pl.BlockSpec((1, d), lambda b, i: (0, 0)),
                  pl.BlockSpec((d, N_EXPERTS), lambda b, i: (0, 0)),
                  pl.BlockSpec((1, N_EXPERTS), lambda b, i: (0, 0))],
        out_specs=(pl.BlockSpec((1, tm, d // LANES, LANES), lambda b, i: (b, i, 0, 0)),
                   pl.BlockSpec((1, tm, TOP_K), lambda b, i: (b, i, 0)),
                   pl.BlockSpec((1, tm, TOP_K), lambda b, i: (b, i, 0))),
        compiler_params=_cparams(("parallel", "parallel")),
        name="moe_router",
    )(x, mod, mod, g.reshape(1, d), router_w.astype(BF16), router_b.reshape(1, N_EXPERTS))


def _expert_kernel(be_ref, nu_ref, idx_hbm, hid_hbm, gate_ref, w1_ref, b1_ref, w2_ref, b2_ref, out_hbm,
                   idx_smem, xbuf, ybuf, w1b_ref, w2b_ref, idx_sem, in_sem, out_sem, *, n_real_rows):
    i = pl.program_id(0)
    n_used = nu_ref[0]
    bm = xbuf.shape[1]
    n_idx_slots = idx_smem.shape[0]
    de = w2_ref.shape[2]
    s = i % 2

    def idx_copy(blk):
        sl = blk % n_idx_slots
        return pltpu.make_async_copy(idx_hbm.at[blk], idx_smem.at[sl], idx_sem.at[sl])

    def row_in(blk, r):
        tok = idx_smem[blk % n_idx_slots, r]
        sl = blk % 2
        return pltpu.make_async_copy(hid_hbm.at[tok], xbuf.at[sl, r], in_sem.at[sl])

    def row_out(blk, r):
        dst = idx_smem[blk % n_idx_slots, bm + r]
        sl = blk % 2
        return pltpu.make_async_copy(ybuf.at[sl, r], out_hbm.at[dst], out_sem.at[sl])

    def rows_in_wait(blk):
        sl = blk % 2
        pltpu.make_async_copy(hid_hbm.at[pl.ds(0, bm)], xbuf.at[sl], in_sem.at[sl]).wait()

    def rows_out_wait(blk):
        sl = blk % 2
        pltpu.make_async_copy(ybuf.at[sl], out_hbm.at[pl.ds(0, bm)], out_sem.at[sl]).wait()

    def for_rows(copy_of_row):
        def group(gi, carry):
            for u in range(ROW_UNROLL):
                copy_of_row(gi * ROW_UNROLL + u).start(priority=u % 2)
            return carry
        lax.fori_loop(0, bm // ROW_UNROLL, group, 0)

    def experts(issue_rows):
        x = _load_row_tiles(xbuf.at[s]).astype(BF16)
        n_chunks = EXPERT_CHUNKS
        cw = 2 * de // n_chunks
        rows_per = bm // n_chunks
        gu = []
        for cix in range(n_chunks):
            issue_rows(cix * rows_per, (cix + 1) * rows_per)
            cols = slice(cix * cw, (cix + 1) * cw)
            gu.append(jnp.dot(x, w1b_ref[:, cols], preferred_element_type=F32) + b1_ref[0, 0, :, cols])
        half = n_chunks // 2
        acts = []
        for cix in range(half):
            gate = jnp.minimum(gu[cix], SWIGLU_LIMIT)
            up = jnp.clip(gu[half + cix], -SWIGLU_LIMIT, SWIGLU_LIMIT)
            acts.append(((up + 1.0) * gate * jax.nn.sigmoid(SWIGLU_ALPHA * gate)).astype(BF16))
        act = jnp.concatenate(acts, axis=1)
        y = jnp.dot(act, w2b_ref[...], preferred_element_type=F32) + b2_ref[0, 0]
        _store_row_tiles(ybuf.at[s], y * gate_ref[...])

    @pl.when(i == 0)
    def _():
        ybuf[1] = jnp.zeros(ybuf.shape[1:], F32)
        fill = pltpu.make_async_copy(ybuf.at[1], out_hbm.at[pl.ds(n_real_rows, bm)], out_sem.at[1])
        fill.start()
        fill.wait()
        idx_copy(0).start()
        idx_copy(0).wait()
        for_rows(lambda r: row_in(0, r))

        @pl.when(n_used > 1)
        def _():
            idx_copy(1).start()

    @pl.when(jnp.logical_and(i >= 2, i <= n_used))
    def _():
        rows_out_wait(i - 2)

    @pl.when(i + 1 < n_used)
    def _():
        idx_copy(i + 1).wait()

        @pl.when(i + 2 < n_used)
        def _():
            idx_copy(i + 2).start()

    @pl.when(jnp.logical_and(i < n_used, jnp.logical_or(i == 0, be_ref[i] != be_ref[jnp.maximum(i - 1, 0)])))
    def _():
        w1b_ref[...] = w1_ref[0, 0].astype(BF16)
        w2b_ref[...] = w2_ref[0, 0].astype(BF16)

    @pl.when(i < n_used)
    def _():
        rows_in_wait(i)

    steady = jnp.logical_and(i >= 1, i + 1 < n_used)

    @pl.when(steady)
    def _():
        def issue_rows(lo, hi):
            for r in range(lo, hi):
                row_in(i + 1, r).start(priority=r % 2)
            for r in range(lo, hi):
                row_out(i - 1, r).start(priority=r % 2)
        experts(issue_rows)

    @pl.when(jnp.logical_not(steady))
    def _():
        @pl.when(i + 1 < n_used)
        def _():
            for_rows(lambda r: row_in(i + 1, r))

        @pl.when(jnp.logical_and(i >= 1, i <= n_used))
        def _():
            for_rows(lambda r: row_out(i - 1, r))

        @pl.when(i < n_used)
        def _():
            experts(lambda lo, hi: None)

        @pl.when(i == n_used)
        def _():
            rows_out_wait(i - 1)


def moe_experts(hid, idx, row_gate, block_e, n_used, w1, b1, w2, b2, layer):
    t, nt, lanes = hid.shape
    d = nt * lanes
    n_blocks = idx.shape[0]
    bm = idx.shape[1] // 2
    _, n_exp, _, two_de = w1.shape
    de = two_de // 2
    any_spec = pl.BlockSpec(memory_space=pl.ANY)
    return pl.pallas_call(
        functools.partial(_expert_kernel, n_real_rows=TOP_K * t),
        out_shape=jax.ShapeDtypeStruct((TOP_K * t + bm, nt, lanes), F32),
        grid_spec=pltpu.PrefetchScalarGridSpec(
            num_scalar_prefetch=2,
            grid=(n_blocks,),
            in_specs=[any_spec, any_spec,
                      pl.BlockSpec((bm, 1), lambda i, be, nu: (i, 0)),
                      pl.BlockSpec((1, 1, d, two_de), lambda i, be, nu: (layer, be[i], 0, 0)),
                      pl.BlockSpec((1, 1, 1, two_de), lambda i, be, nu: (layer, be[i], 0, 0)),
                      pl.BlockSpec((1, 1, de, d), lambda i, be, nu: (layer, be[i], 0, 0)),
                      pl.BlockSpec((1, 1, 1, d), lambda i, be, nu: (layer, be[i], 0, 0))],
            out_specs=any_spec,
            scratch_shapes=[pltpu.SMEM((4, 2 * bm), jnp.int32),
                            pltpu.VMEM((2, bm, nt, lanes), F32),
                            pltpu.VMEM((2, bm, nt, lanes), F32),
                            pltpu.VMEM((d, two_de), BF16),
                            pltpu.VMEM((de, d), BF16),
                            pltpu.SemaphoreType.DMA((4,)),
                            pltpu.SemaphoreType.DMA((2,)),
                            pltpu.SemaphoreType.DMA((2,))]),
        compiler_params=_cparams(("arbitrary",)),
        name="moe_experts",
    )(block_e, n_used, idx, hid, row_gate, w1, b1.reshape(b1.shape[0], n_exp, 1, two_de), w2,
      b2.reshape(b2.shape[0], n_exp, 1, d))


def _combine_kernel(x_ref, y0_ref, y1_ref, y2_ref, y3_ref, ml_ref, mc_ref, o_ref, ysum_ref, *, tm, n_ctx, row_off):
    i = pl.program_id(1)
    rows = row_off + i * tm + lax.broadcasted_iota(jnp.int32, (tm, 1), 0)
    is_ctx = rows < n_ctx
    ysum_ref[...] = (y0_ref[...] + y1_ref[...]) + (y2_ref[...] + y3_ref[...])
    o_ref[0] = x_ref[0] + _row_mod(ml_ref, mc_ref, 5, is_ctx) * _load_row_tiles(ysum_ref)


def moe_combine(x, ys, mod, n_ctx, row_off):
    bsz, l, d = x.shape
    tm = 256
    nrow = (l - row_off) // tm
    off_blk = row_off // tm
    nt = d // LANES
    plane = lambda k: pl.BlockSpec((tm, nt, LANES), lambda b, i: ((k * bsz + b) * nrow + i, 0, 0))
    return pl.pallas_call(
        functools.partial(_combine_kernel, tm=tm, n_ctx=n_ctx, row_off=row_off),
        out_shape=jax.ShapeDtypeStruct((bsz, l - row_off, d), F32),
        grid=(bsz, nrow),
        in_specs=[pl.BlockSpec((1, tm, d), lambda b, i: (b, i + off_blk, 0)),
                  plane(0), plane(1), plane(2), plane(3),
                  pl.BlockSpec((1, 6, d), lambda b, i: (b, 0, 0)),
                  pl.BlockSpec((1, 6, d), lambda b, i: (bsz, 0, 0))],
        out_specs=pl.BlockSpec((1, tm, d), lambda b, i: (b, i, 0)),
        scratch_shapes=[pltpu.VMEM((tm, nt, LANES), F32)],
        compiler_params=_cparams(("parallel", "parallel")),
        name="moe_combine",
    )(x, ys, ys, ys, ys, mod, mod)


def moe_layer(x, mod, g, router_w, router_b, w1, b1, w2, b2, layer, n_ctx, row_off):
    bsz, l, d = x.shape
    hid, top_i, gates = moe_router(x, mod, g, router_w, router_b, n_ctx, row_off)
    t = bsz * (l - row_off)
    bm = MOE_BLOCK_ROWS
    n_blocks = (t * TOP_K + bm - 1) // bm + N_EXPERTS
    flat_e = top_i.reshape(-1)
    order = jnp.argsort(flat_e).astype(jnp.int32)
    counts = jnp.sum((flat_e[:, None] == jnp.arange(N_EXPERTS)[None, :]).astype(jnp.int32), axis=0)
    starts = jnp.cumsum(counts) - counts
    padded = (counts + bm - 1) // bm * bm
    pends = jnp.cumsum(padded)
    pstarts = pends - padded
    blk_first = jnp.arange(n_blocks, dtype=jnp.int32) * bm
    in_expert = jnp.logical_and(blk_first[:, None] >= pstarts[None, :], blk_first[:, None] < pends[None, :])
    pick = lambda v: jnp.sum(jnp.where(in_expert, v[None, :], 0), axis=1)
    block_e = jnp.minimum(jnp.sum((blk_first[:, None] >= pends[None, :]).astype(jnp.int32), axis=1), N_EXPERTS - 1)
    off = blk_first - pick(pstarts)
    n_valid = jnp.clip(pick(counts) - off, 0, bm).astype(jnp.int32)
    n_used = (pends[-1] // bm).astype(jnp.int32).reshape(1)
    r = jnp.arange(bm, dtype=jnp.int32)[None, :]
    valid = r < n_valid[:, None]
    pos = jnp.where(valid, (pick(starts) + off)[:, None] + r, 0)
    src = order[pos]
    tok, kk = src // TOP_K, src % TOP_K
    dst = jnp.where(valid, kk * t + tok, TOP_K * t + r)
    idx = jnp.concatenate([tok, dst], axis=1).astype(jnp.int32)
    row_gate = jnp.where(valid, gates.reshape(-1)[src], 0.0).reshape(n_blocks * bm, 1)
    ys = moe_experts(hid.reshape(t, d // LANES, LANES), idx, row_gate, block_e.astype(jnp.int32), n_used,
                     w1, b1, w2, b2, layer)
    return moe_combine(x, ys, mod, n_ctx, row_off)


def _final_norm_kernel(x_ref, g_ref, o_ref):
    x = x_ref[0]
    o_ref[0] = (x * lax.rsqrt(jnp.mean(x * x, axis=-1, keepdims=True) + NORM_EPS)) * g_ref[...]


def final_norm(x, g):
    bsz, l, d = x.shape
    tm = 512
    return pl.pallas_call(
        _final_norm_kernel,
        out_shape=jax.ShapeDtypeStruct((bsz, l, d), F32),
        grid=(bsz, l // tm),
        in_specs=[pl.BlockSpec((1, tm, d), lambda b, i: (b, i, 0)), pl.BlockSpec((1, d), lambda b, i: (0, 0))],
        out_specs=pl.BlockSpec((1, tm, d), lambda b, i: (b, i, 0)),
        compiler_params=_cparams(("parallel", "parallel")),
        name="final_norm",
    )(x, g.reshape(1, d))


ATT_BLOCK = 128
NEG_BIG = -1e30


def _qk(q, k):
    return lax.dot_general(q, k, (((1,), (1,)), ((), ())), preferred_element_type=F32)


def _pair_queries(q2, lane):
    return jnp.concatenate([jnp.where(lane < HEAD_DIM, q2, 0.0), jnp.where(lane < HEAD_DIM, 0.0, q2)], axis=0).astype(BF16)


def _pair_outputs(acc, lane):
    r = acc.shape[0] // 2
    return jnp.where(lane < HEAD_DIM, acc[:r], acc[r:])


def _softmax_pv(scores, values, extra_logit=None):
    m = functools.reduce(jnp.maximum, [jnp.max(s, axis=-1, keepdims=True) for s in scores])
    if extra_logit is not None:
        m = jnp.maximum(m, extra_logit)
    ps = [jnp.exp(s - m) for s in scores]
    den = functools.reduce(jnp.add, [jnp.sum(p, axis=-1, keepdims=True) for p in ps])
    if extra_logit is not None:
        den = den + jnp.exp(extra_logit - m)
    acc = functools.reduce(jnp.add, [jnp.dot(p.astype(BF16), v, preferred_element_type=F32) for p, v in zip(ps, values)])
    return acc / den


def _na_kernel(q_ref, k_ref, v_ref, bias_ref, o_ref, *, n_ctx, blk_off, n_dbl):
    blk = pl.program_id(1) + blk_off
    n_ctx_blk = n_ctx // ATT_BLOCK
    s2 = jnp.clip(blk - n_ctx_blk - 2, 0, n_dbl - 5)
    start = pl.multiple_of(n_ctx + s2 * ATT_BLOCK, ATT_BLOCK)
    lane = lax.broadcasted_iota(jnp.int32, (1, 2 * HEAD_DIM), 1)
    scale = HEAD_DIM ** -0.5

    def pair_out(p, local):
        cs = slice(p * 2 * HEAD_DIM, (p + 1) * 2 * HEAD_DIM)
        q2 = q_ref[0, :, cs] * scale
        kc = k_ref[0, 0:n_ctx, cs].astype(BF16)
        vc = v_ref[0, 0:n_ctx, cs].astype(BF16)
        if local:
            kl = k_ref[0, pl.ds(start, 5 * ATT_BLOCK), cs].astype(BF16)
            vl = v_ref[0, pl.ds(start, 5 * ATT_BLOCK), cs].astype(BF16)
        qp = _pair_queries(q2, lane)
        scores, values = [_qk(qp, kc)], [vc]
        if local:
            scores.append(_qk(qp, kl) + bias_ref[0, p])
            values.append(vl)
        o_ref[0, :, cs] = _pair_outputs(_softmax_pv(scores, values), lane)

    @pl.when(blk < n_ctx_blk)
    def _():
        for p in range(NA_HEADS // 2):
            pair_out(p, False)

    @pl.when(blk >= n_ctx_blk)
    def _():
        for p in range(NA_HEADS // 2):
            pair_out(p, True)


def _na_bias_tables(rpb, rows):
    n_dbl = rows // 2
    wr = min(NA_WIN_R, rows)
    cidx = np.arange(GRID_W)
    c0 = np.clip(cidx - NA_WIN_C // 2, 0, GRID_W - NA_WIN_C)
    cvalid = (cidx[None, :] >= c0[:, None]) & (cidx[None, :] < c0[:, None] + NA_WIN_C)
    coff = np.clip(cidx[None, :] - cidx[:, None] + (NA_WIN_C - 1), 0, 2 * NA_WIN_C - 2)
    col_onehot = (coff[:, :, None] == np.arange(2 * NA_WIN_C - 1)[None, None, :]) & cvalid[:, :, None]
    pats = (0, 1, 2, n_dbl - 2, n_dbl - 1)
    row_onehot = np.zeros((len(pats), 2, 10, 2 * NA_WIN_R - 1), np.float32)
    for pi, r2 in enumerate(pats):
        s2 = int(np.clip(r2 - 2, 0, n_dbl - 5))
        for qh in range(2):
            qr = 2 * r2 + qh
            r0 = int(np.clip(qr - wr // 2, 0, rows - wr))
            for kk in range(10):
                kr = 2 * s2 + kk
                if r0 <= kr < r0 + wr:
                    row_onehot[pi, qh, kk, kr - qr + NA_WIN_R - 1] = 1.0
    t1 = jnp.einsum('hrc,qkc->hrqk', rpb, jnp.asarray(col_onehot, F32), precision=lax.Precision.HIGHEST)
    bias = jnp.einsum('pajr,hrqk->phaqjk', jnp.asarray(row_onehot), t1, precision=lax.Precision.HIGHEST)
    valid = (row_onehot.sum(-1)[:, None, :, None, :, None] > 0) & cvalid[None, None, None, :, None, :]
    bias = jnp.where(jnp.asarray(valid), bias, NEG_BIG)
    return bias.reshape(len(pats), rpb.shape[0] // 2, 4 * GRID_W, 10 * GRID_W)


def na_attention(proj, rpb, n_ctx, blk_off):
    bsz, l, _ = proj.shape
    rows = (l - n_ctx) // GRID_W
    n_dbl = rows // 2
    n_ctx_blk = n_ctx // ATT_BLOCK
    bias = _na_bias_tables(rpb, rows)

    def bias_idx(b, i):
        r2 = i + blk_off - n_ctx_blk
        pat = jnp.where(r2 < 2, r2, jnp.where(r2 >= n_dbl - 2, r2 - (n_dbl - 5), 2))
        return (jnp.clip(pat, 0, 4), 0, 0, 0)

    return pl.pallas_call(
        functools.partial(_na_kernel, n_ctx=n_ctx, blk_off=blk_off, n_dbl=n_dbl),
        out_shape=jax.ShapeDtypeStruct((bsz, l - blk_off * ATT_BLOCK, NA_DIM), F32),
        grid=(bsz, l // ATT_BLOCK - blk_off),
        in_specs=[pl.BlockSpec((1, ATT_BLOCK, NA_DIM), lambda b, i: (b, i + blk_off, COL_QN // NA_DIM)),
                  pl.BlockSpec((1, l, NA_DIM), lambda b, i: (b, 0, COL_KN // NA_DIM)),
                  pl.BlockSpec((1, l, NA_DIM), lambda b, i: (b, 0, COL_VN // NA_DIM)),
                  pl.BlockSpec((1, NA_HEADS // 2, 2 * ATT_BLOCK, 5 * ATT_BLOCK), bias_idx)],
        out_specs=pl.BlockSpec((1, ATT_BLOCK, NA_DIM), lambda b, i: (b, i, 0)),
        compiler_params=_cparams(("parallel", "arbitrary")),
        name="na_attention",
    )(proj, proj, proj, bias)


def _swa_kernel(sink_ref, q_ref, k_ref, v_ref, cq_ref, sq_ref, ck_ref, sk_ref, o_ref, *, n_ctx, blk_off, n_blk):
    blk = pl.program_id(1) + blk_off
    n_ctx_blk = n_ctx // ATT_BLOCK
    n = blk - n_ctx_blk
    first = jnp.clip(n - 1, 0, n_blk - 3)
    kstart = pl.multiple_of(first * ATT_BLOCK, ATT_BLOCK)
    lane = lax.broadcasted_iota(jnp.int32, (1, 2 * HEAD_DIM), 1)
    scale = HEAD_DIM ** -0.5
    quarter = HEAD_DIM // 4

    def rope(u, cos, sin):
        w = u.shape[-1]
        li = lax.broadcasted_iota(jnp.int32, (1, w), 1)
        swapped = jnp.where(li % (2 * quarter) < quarter, pltpu.roll(u, w - quarter, 1), pltpu.roll(u, quarter, 1))
        return u * cos + swapped * sin

    def dup_head(x, h):
        other = pltpu.roll(x, HEAD_DIM, 1)
        return jnp.where((lane < HEAD_DIM) == (h == 0), x, other)

    def run(local):
        q = q_ref[0] * scale
        kc_all = k_ref[0, 0:n_ctx, :]
        vc_all = v_ref[0, 0:n_ctx, :]
        if local:
            q = rope(q, cq_ref[...], sq_ref[...])
            kl_all = rope(k_ref[0, pl.ds(n_ctx + kstart, 3 * ATT_BLOCK), :],
                          ck_ref[pl.ds(kstart, 3 * ATT_BLOCK), :], sk_ref[pl.ds(kstart, 3 * ATT_BLOCK), :])
            vl_all = v_ref[0, pl.ds(n_ctx + kstart, 3 * ATT_BLOCK), :]
            qrow = lax.broadcasted_iota(jnp.int32, (2 * ATT_BLOCK, 3 * ATT_BLOCK), 0) % ATT_BLOCK
            kpos = kstart + lax.broadcasted_iota(jnp.int32, (2 * ATT_BLOCK, 3 * ATT_BLOCK), 1)
            band = jnp.where(jnp.abs(kpos - (n * ATT_BLOCK + qrow)) <= SWA_WINDOW, 0.0, NEG_BIG)
        group = SWA_Q_HEADS // SWA_KV_HEADS
        first_head = lax.broadcasted_iota(jnp.int32, (2 * ATT_BLOCK, 1), 0) < ATT_BLOCK
        for h in range(SWA_KV_HEADS):
            kc = dup_head(kc_all, h).astype(BF16)
            vc = dup_head(vc_all, h).astype(BF16)
            if local:
                kl = dup_head(kl_all, h).astype(BF16)
                vl = dup_head(vl_all, h).astype(BF16)
            for p in range(h * group // 2, (h + 1) * group // 2):
                cs = slice(p * 2 * HEAD_DIM, (p + 1) * 2 * HEAD_DIM)
                qp = _pair_queries(q[:, cs], lane)
                scores, values = [_qk(qp, kc)], [vc]
                if local:
                    scores.append(_qk(qp, kl) + band)
                    values.append(vl)
                sink = jnp.where(first_head, sink_ref[2 * p], sink_ref[2 * p + 1])
                o_ref[0, :, cs] = _pair_outputs(_softmax_pv(scores, values, extra_logit=sink), lane)

    @pl.when(blk < n_ctx_blk)
    def _():
        run(False)

    @pl.when(blk >= n_ctx_blk)
    def _():
        run(True)


def _rope_tables(s):
    pos = np.arange(s)
    quarter = HEAD_DIM // 4
    inv = ROPE_THETA ** (-jnp.arange(0, 2 * quarter, 2, dtype=F32) / (2 * quarter))

    def axis_tables(p):
        ang = jnp.asarray(p, F32)[:, None] * inv[None, :]
        c, sn = jnp.cos(ang), jnp.sin(ang)
        return jnp.concatenate([c, c], axis=1), jnp.concatenate([-sn, sn], axis=1)
    cr, sr = axis_tables(pos // GRID_W)
    cc, sc = axis_tables(pos % GRID_W)
    return jnp.concatenate([cr, cc], axis=1), jnp.concatenate([sr, sc], axis=1)


def swa_attention(proj, sink, n_ctx, blk_off):
    bsz, l, _ = proj.shape
    s = l - n_ctx
    n_blk = s // ATT_BLOCK
    n_ctx_blk = n_ctx // ATT_BLOCK
    cos, sin = _rope_tables(s)
    cq, sq = jnp.tile(cos, (1, SWA_Q_HEADS)), jnp.tile(sin, (1, SWA_Q_HEADS))
    ck, sk = jnp.tile(cos, (1, SWA_KV_HEADS)), jnp.tile(sin, (1, SWA_KV_HEADS))
    qtab = lambda b, i: (jnp.maximum(i + blk_off - n_ctx_blk, 0), 0)
    return pl.pallas_call(
        functools.partial(_swa_kernel, n_ctx=n_ctx, blk_off=blk_off, n_blk=n_blk),
        out_shape=jax.ShapeDtypeStruct((bsz, l - blk_off * ATT_BLOCK, SWA_Q), F32),
        grid=(bsz, l // ATT_BLOCK - blk_off),
        in_specs=[pl.BlockSpec(memory_space=pltpu.SMEM),
                  pl.BlockSpec((1, ATT_BLOCK, SWA_Q), lambda b, i: (b, i + blk_off, COL_QS // SWA_Q)),
                  pl.BlockSpec((1, l, SWA_KV), lambda b, i: (b, 0, COL_KS // SWA_KV)),
                  pl.BlockSpec((1, l, SWA_KV), lambda b, i: (b, 0, COL_VS // SWA_KV)),
                  pl.BlockSpec((ATT_BLOCK, SWA_Q), qtab),
                  pl.BlockSpec((ATT_BLOCK, SWA_Q), qtab),
                  pl.BlockSpec((s, SWA_KV), lambda b, i: (0, 0)),
                  pl.BlockSpec((s, SWA_KV), lambda b, i: (0, 0))],
        out_specs=pl.BlockSpec((1, ATT_BLOCK, SWA_Q), lambda b, i: (b, i, 0)),
        compiler_params=_cparams(("parallel", "arbitrary")),
        name="swa_attention",
    )(sink, proj, proj, proj, cq, sq, ck, sk)


CONV_HALO = 8


def _ssd_kernel(*refs, reverse, n_ctx, n_chunks):
    if reverse:
        xsc_ref, bcc_ref, dt_ref, dtbias_ref, acoef_ref, z_ref, yf_ref, dskip_ref, ng_ref, o_ref, state_ref = refs
    else:
        (xs_c, xs_p, xs_n, bc_c, bc_p, bc_n, dt_ref, cwx_ref, cbx_ref, cwb_ref, cbb_ref, dtbias_ref, acoef_ref,
         o_ref, xsc_ref, bcc_ref, state_ref) = refs
    q = SSD_CHUNK
    n_st = SSD_STATE
    gw = SSD_INNER // SSD_GROUPS
    j = pl.program_id(1)
    c = _ssd_chunk_of_step(j, reverse, n_ctx // q, n_chunks)
    n_cc = n_ctx // q
    seg_first = jnp.logical_or(c == 0, c == n_cc)
    seg_last = jnp.logical_or(c == n_cc - 1, c == n_chunks - 1)

    @pl.when(j == 0)
    def _():
        state_ref[...] = jnp.zeros_like(state_ref)

    def conv_silu(cur_ref, prev_ref, next_ref, w_ref, b_ref):
        half = w_ref.shape[0] // 2
        prev = jnp.where(seg_first, 0.0, prev_ref[0])
        nxt = jnp.where(seg_last, 0.0, next_ref[0])
        ext = jnp.concatenate([prev, cur_ref[0], nxt], axis=0)
        acc = b_ref[...]
        for k in range(w_ref.shape[0]):
            o = CONV_HALO - half + k
            acc = acc + ext[o:o + q] * w_ref[k:k + 1, :]
        return acc * jax.nn.sigmoid(acc)

    if reverse:
        xs, bcv = xsc_ref[0], bcc_ref[0]
    else:
        xs = conv_silu(xs_c, xs_p, xs_n, cwx_ref, cbx_ref)
        bcv = conv_silu(bc_c, bc_p, bc_n, cwb_ref, cbb_ref)
        xsc_ref[0] = xs
        bcc_ref[0] = bcv

    lo = SSD_HEADS if reverse else 0
    dt_raw = dt_ref[0] + dtbias_ref[...]
    dt_all = jnp.maximum(dt_raw, 0.0) + jnp.log(1.0 + jnp.exp(-jnp.abs(dt_raw)))
    a_all = dt_all * acoef_ref[...]
    ri = lax.broadcasted_iota(jnp.int32, (q, q), 0)
    ci = lax.broadcasted_iota(jnp.int32, (q, q), 1)
    causal = (ci >= ri) if reverse else (ci <= ri)
    a_cs = jnp.dot(causal.astype(F32), a_all, precision=lax.Precision.HIGHEST, preferred_element_type=F32)
    total = a_cs[0:1] if reverse else a_cs[q - 1:q]
    a_cs_t = a_cs.T

    lane = lax.broadcasted_iota(jnp.int32, (1, 2 * SSD_HEAD_DIM), 1)

    def expand(v):
        parts = [jnp.where(lane < SSD_HEAD_DIM, v[:, lo + 2 * p:lo + 2 * p + 1], v[:, lo + 2 * p + 1:lo + 2 * p + 2])
                 for p in range(SSD_HEADS // 2)]
        return jnp.concatenate(parts, axis=1)

    xdt = xs * expand(dt_all)
    dec_start = expand(jnp.exp(a_cs))
    dec_end = expand(jnp.exp(total - a_cs))
    dec_total = expand(jnp.exp(total))

    heads_per_group = SSD_HEADS // SSD_GROUPS
    y_parts = []
    for g in range(SSD_GROUPS):
        b_g = bcv[:, g * n_st:(g + 1) * n_st]
        c_g = bcv[:, (SSD_GROUPS + g) * n_st:(SSD_GROUPS + g + 1) * n_st].astype(BF16)
        cb = _qk(c_g, b_g.astype(BF16))
        cols = slice(g * gw, (g + 1) * gw)
        st = state_ref[:, cols]
        y_off = jnp.dot(c_g, st.astype(BF16), preferred_element_type=F32) * dec_start[:, cols]
        diag = []
        for p in range(g * heads_per_group // 2, (g + 1) * heads_per_group // 2):
            x2 = xdt[:, p * 2 * SSD_HEAD_DIM:(p + 1) * 2 * SSD_HEAD_DIM].astype(BF16)
            outs = []
            for hh in range(2):
                h = lo + 2 * p + hh
                seg = jnp.where(causal, a_cs[:, h:h + 1] - a_cs_t[h:h + 1, :], NEG_BIG)
                m = (cb * jnp.exp(seg)).astype(BF16)
                outs.append(jnp.dot(m, x2, preferred_element_type=F32))
            diag.append(jnp.where(lane < SSD_HEAD_DIM, outs[0], outs[1]))
        y_parts.append(jnp.concatenate(diag, axis=1) + y_off)
        xw = (xdt[:, cols] * dec_end[:, cols]).astype(BF16)
        state_ref[:, cols] = dec_total[:, cols] * st + jnp.dot(b_g.T.astype(BF16), xw, preferred_element_type=F32)
    y = jnp.concatenate(y_parts, axis=1)

    if reverse:
        y = yf_ref[0] + y + dskip_ref[...] * xs
        z = z_ref[0]
        y = y * (z * jax.nn.sigmoid(z))
        o_ref[0] = (y * lax.rsqrt(jnp.mean(y * y, axis=-1, keepdims=True) + NORM_EPS)) * ng_ref[...]
    else:
        o_ref[0] = y


def _ssd_chunk_of_step(j, reverse, n_ctx_chunks, n_chunks):
    if not reverse:
        return j
    return jnp.where(j < n_ctx_chunks, n_ctx_chunks - 1 - j, n_chunks - 1 - (j - n_ctx_chunks))


def ssd_mixer(proj, conv_w, conv_b, a_log, dt_bias, d_skip, norm_g, n_ctx):
    bsz, l, _ = proj.shape
    q = SSD_CHUNK
    n_chunks = l // q
    n_cc = n_ctx // q
    hb = q // CONV_HALO
    pad = lambda v: jnp.concatenate([v.reshape(1, -1), jnp.zeros((1, 128 - v.size), F32)], axis=1)
    dtb = pad(dt_bias)
    acoef = pad(-jnp.exp(a_log))
    cwx, cwb = conv_w[:, :SSD_INNER], conv_w[:, SSD_INNER:]
    cbx, cbb = conv_b[:SSD_INNER].reshape(1, -1), conv_b[SSD_INNER:].reshape(1, -1)
    wbc = cwb.shape[1]

    dsk = jnp.repeat(d_skip, SSD_HEAD_DIM).reshape(1, SSD_INNER)
    ng = norm_g.reshape(1, SSD_INNER)
    full = lambda a: pl.BlockSpec(a.shape, lambda b, jj: (0, 0))

    def call(reverse, inputs, in_specs, out_widths):
        chunk = lambda jj: _ssd_chunk_of_step(jj, reverse, n_cc, n_chunks)
        cur = lambda w, off: pl.BlockSpec((1, q, w), lambda b, jj: (b, chunk(jj), off // w))
        prev = lambda w, off: pl.BlockSpec((1, CONV_HALO, w), lambda b, jj: (b, jnp.maximum(chunk(jj) * hb - 1, 0), off // w))
        nxt = lambda w, off: pl.BlockSpec((1, CONV_HALO, w),
                                          lambda b, jj: (b, jnp.minimum((chunk(jj) + 1) * hb, l // CONV_HALO - 1), off // w))
        return pl.pallas_call(
            functools.partial(_ssd_kernel, reverse=reverse, n_ctx=n_ctx, n_chunks=n_chunks),
            out_shape=tuple(jax.ShapeDtypeStruct((bsz, l, w), F32) for w in out_widths),
            grid=(bsz, n_chunks),
            in_specs=in_specs(cur, prev, nxt),
            out_specs=tuple(cur(w, 0) for w in out_widths),
            scratch_shapes=[pltpu.VMEM((SSD_STATE, SSD_INNER), F32)],
            compiler_params=_cparams(("parallel", "arbitrary")),
            name="ssd_bwd" if reverse else "ssd_fwd",
        )(*inputs)

    y_f, xs_conv, bc_conv = call(
        False, (proj, proj, proj, proj, proj, proj, proj, cwx, cbx, cwb, cbb, dtb, acoef),
        lambda cur, prev, nxt: [cur(SSD_INNER, COL_XS), prev(SSD_INNER, COL_XS), nxt(SSD_INNER, COL_XS),
                                cur(wbc, COL_BC), prev(wbc, COL_BC), nxt(wbc, COL_BC),
                                cur(128, COL_DT), full(cwx), full(cbx), full(cwb), full(cbb), full(dtb), full(acoef)],
        (SSD_INNER, SSD_INNER, wbc))
    (out,) = call(
        True, (xs_conv, bc_conv, proj, dtb, acoef, proj, y_f, dsk, ng),
        lambda cur, prev, nxt: [cur(SSD_INNER, 0), cur(wbc, 0), cur(128, COL_DT), full(dtb), full(acoef),
                                cur(SSD_INNER, COL_Z), cur(SSD_INNER, 0), full(dsk), full(ng)],
        (SSD_INNER,))
    return out


def _pack_w_in(w):
    o = 0
    segs = {}
    for name, width in (('z', SSD_INNER), ('xs', SSD_INNER), ('bc', 512), ('dt', 2 * SSD_HEADS), ('qs', SWA_Q),
                        ('ks', SWA_KV), ('vs', SWA_KV), ('qn', NA_DIM), ('kn', NA_DIM), ('vn', NA_DIM),
                        ('g', N_BRANCH * D_MODEL)):
        segs[name] = w[:, o:o + width]
        o += width
    pad = jnp.zeros((w.shape[0], 128 - 2 * SSD_HEADS), w.dtype)
    order = ('g', 'z', 'xs', 'bc', 'qs', 'qn', 'kn', 'vn', 'ks', 'vs', 'dt')
    return jnp.concatenate([segs[k] for k in order] + [pad], axis=1).astype(BF16)


def kernel(x, c, ctx, c_ctx, norm1_g, norm2_g, w_ada, b_ada, w_in, b_gate, ssd_conv_w, ssd_conv_b, ssd_a_log,
           ssd_dt_bias, ssd_d, ssd_norm_g, swa_sink, na_rpb, w_br_ssd, w_br_swa, w_br_na, w_out, router_w, router_b,
           moe_w1, moe_b1, moe_w2, moe_b2, final_norm_g):
    bsz, s, d = x.shape
    lc = ctx.shape[1]
    depth = w_in.shape[0]
    n_mod_rows = 16
    cond = jnp.concatenate([c, c_ctx[None, :], jnp.zeros((n_mod_rows - bsz - 1, d), F32)], axis=0)
    xs = jnp.concatenate([ctx, x], axis=1)
    for i in range(depth):
        last = i == depth - 1
        mod = ada_modulation(cond, w_ada, b_ada, i).reshape(n_mod_rows, 6, d)
        proj = norm_proj(xs, mod, norm1_g[i], _pack_w_in(w_in[i]), lc)
        o_ssd = ssd_mixer(proj, ssd_conv_w[i], ssd_conv_b[i], ssd_a_log[i], ssd_dt_bias[i], ssd_d[i], ssd_norm_g[i], lc)
        blk_off = lc // ATT_BLOCK if last else 0
        o_swa = swa_attention(proj, swa_sink[i], lc, blk_off)
        o_na = na_attention(proj, na_rpb[i], lc, blk_off)
        xs = merge(xs, proj, o_ssd, o_swa, o_na, b_gate[i].reshape(N_BRANCH, d), w_br_ssd[i].astype(BF16),
                   w_br_swa[i].astype(BF16), w_br_na[i].astype(BF16), w_out[i].astype(BF16), mod, lc,
                   lc if last else 0)
        xs = moe_layer(xs, mod, norm2_g[i], router_w[i], router_b[i], moe_w1, moe_b1, moe_w2, moe_b2, i,
                       0 if last else lc, 0)
    return final_norm(xs, final_norm_g)
```

```python
import functools

import jax
import jax.numpy as jnp
import numpy as np
from jax import lax
from jax.experimental import pallas as pl
from jax.experimental.pallas import tpu as pltpu

F32 = jnp.float32
BF16 = jnp.bfloat16

D_MODEL = 1024
GRID_W = 64
HEAD_DIM = 64
NORM_EPS = 1e-6

SSD_HEADS = 16
SSD_HEAD_DIM = 64
SSD_INNER = SSD_HEADS * SSD_HEAD_DIM
SSD_GROUPS = 2
SSD_STATE = 128
SSD_CHUNK = 128

SWA_Q_HEADS = 8
SWA_KV_HEADS = 2
SWA_WINDOW = 128
SWA_Q = SWA_Q_HEADS * HEAD_DIM
SWA_KV = SWA_KV_HEADS * HEAD_DIM
ROPE_THETA = 10000.0

NA_HEADS = 8
NA_WIN_R = 8
NA_WIN_C = 16
NA_DIM = NA_HEADS * HEAD_DIM

N_BRANCH = 3
N_EXPERTS = 32
TOP_K = 4
SWIGLU_LIMIT = 7.0
SWIGLU_ALPHA = 1.702

COL_GATE = 0
COL_Z = 3 * D_MODEL
COL_XS = COL_Z + SSD_INNER
COL_BC = COL_XS + SSD_INNER
COL_QS = COL_BC + 512
COL_QN = COL_QS + SWA_Q
COL_KN = COL_QN + NA_DIM
COL_VN = COL_KN + NA_DIM
COL_KS = COL_VN + NA_DIM
COL_VS = COL_KS + SWA_KV
COL_DT = COL_VS + SWA_KV
IN_COLS_PACKED = COL_DT + 128

MOE_BLOCK_ROWS = 256
ROW_UNROLL = 8
EXPERT_CHUNKS = 8
VMEM_LIMIT = 56 * 1024 * 1024


def _cparams(sem):
    return pltpu.CompilerParams(dimension_semantics=sem, vmem_limit_bytes=VMEM_LIMIT)


def _ada_kernel(c_ref, w_ref, b_ref, o_ref):
    c = c_ref[...]
    s = (c * jax.nn.sigmoid(c)).astype(BF16)
    o_ref[...] = jnp.dot(s, w_ref[0].astype(BF16), preferred_element_type=F32) + b_ref[0]


def ada_modulation(rows, w_ada, b_ada, layer):
    r, d = rows.shape
    n = w_ada.shape[2]
    tn = 1024
    return pl.pallas_call(
        _ada_kernel,
        out_shape=jax.ShapeDtypeStruct((r, n), F32),
        grid=(n // tn,),
        in_specs=[pl.BlockSpec((r, d), lambda j: (0, 0)),
                  pl.BlockSpec((1, d, tn), lambda j: (layer, 0, j)),
                  pl.BlockSpec((1, 1, tn), lambda j: (layer, 0, j))],
        out_specs=pl.BlockSpec((r, tn), lambda j: (0, j)),
        compiler_params=_cparams(("parallel",)),
        name="ada_modulation",
    )(rows, w_ada, b_ada.reshape(b_ada.shape[0], 1, n))


def _row_mod(mod_l_ref, mod_c_ref, idx, is_ctx):
    return jnp.where(is_ctx, mod_c_ref[0, idx:idx + 1, :], mod_l_ref[0, idx:idx + 1, :])


def _rms_mod(x, g, scale, shift):
    y = x * lax.rsqrt(jnp.mean(x * x, axis=-1, keepdims=True) + NORM_EPS)
    return (y * g) * (1.0 + scale) + shift


def _norm_proj_kernel(x_ref, ml_ref, mc_ref, g_ref, w_ref, o_ref, h_ref, *, tm, n_ctx):
    i = pl.program_id(1)
    j = pl.program_id(2)

    @pl.when(j == 0)
    def _():
        rows = i * tm + lax.broadcasted_iota(jnp.int32, (tm, 1), 0)
        is_ctx = rows < n_ctx
        h = _rms_mod(x_ref[0], g_ref[...], _row_mod(ml_ref, mc_ref, 1, is_ctx), _row_mod(ml_ref, mc_ref, 0, is_ctx))
        h_ref[...] = h.astype(BF16)

    o_ref[0] = jnp.dot(h_ref[...], w_ref[...], preferred_element_type=F32)


def norm_proj(x, mod, g, w_packed, n_ctx):
    bsz, l, d = x.shape
    n = w_packed.shape[1]
    tm, tn = 768, 2688
    return pl.pallas_call(
        functools.partial(_norm_proj_kernel, tm=tm, n_ctx=n_ctx),
        out_shape=jax.ShapeDtypeStruct((bsz, l, n), F32),
        grid=(bsz, l // tm, n // tn),
        in_specs=[pl.BlockSpec((1, tm, d), lambda b, i, j: (b, i, 0)),
                  pl.BlockSpec((1, 6, d), lambda b, i, j: (b, 0, 0)),
                  pl.BlockSpec((1, 6, d), lambda b, i, j: (bsz, 0, 0)),
                  pl.BlockSpec((1, d), lambda b, i, j: (0, 0)),
                  pl.BlockSpec((d, tn), lambda b, i, j: (0, j))],
        out_specs=pl.BlockSpec((1, tm, tn), lambda b, i, j: (b, i, j)),
        scratch_shapes=[pltpu.VMEM((tm, d), BF16)],
        compiler_params=_cparams(("parallel", "parallel", "arbitrary")),
        name="norm_proj",
    )(x, mod, mod, g.reshape(1, d), w_packed)


def _merge_kernel(x_ref, ga_ref, gb_ref, gn_ref, ossd_ref, oswa_ref, ona_ref, bg_ref, wssd_ref, wswa_ref, wna_ref,
                  wout_ref, ml_ref, mc_ref, o_ref, *, tm, n_ctx, row_off):
    i = pl.program_id(1)
    rows = row_off + i * tm + lax.broadcasted_iota(jnp.int32, (tm, 1), 0)
    is_ctx = rows < n_ctx

    def branch(gate_ref, k, o_br_ref, w_ref):
        gate = jax.nn.sigmoid(gate_ref[0] + bg_ref[k:k + 1, :])
        return gate * jnp.dot(o_br_ref[0].astype(BF16), w_ref[...], preferred_element_type=F32)

    m = branch(ga_ref, 0, ossd_ref, wssd_ref) + branch(gb_ref, 1, oswa_ref, wswa_ref) + branch(gn_ref, 2, ona_ref, wna_ref)
    y = jnp.dot(m.astype(BF16), wout_ref[...], preferred_element_type=F32)
    o_ref[0] = x_ref[0] + _row_mod(ml_ref, mc_ref, 2, is_ctx) * y


def merge(x, proj, o_ssd, o_swa, o_na, b_gate, w_ssd, w_swa, w_na, w_out, mod, n_ctx, row_off):
    bsz, l, d = x.shape
    tm = 256
    gate_blk = COL_GATE // d
    off_blk = row_off // tm

    def row(a):
        o = off_blk if a.shape[1] == l else 0
        return pl.BlockSpec((1, tm, a.shape[-1]), lambda b, i: (b, i + o, 0))

    full = lambda a: pl.BlockSpec(a.shape, lambda b, i: (0,) * a.ndim)
    return pl.pallas_call(
        functools.partial(_merge_kernel, tm=tm, n_ctx=n_ctx, row_off=row_off),
        out_shape=jax.ShapeDtypeStruct((bsz, l - row_off, d), F32),
        grid=(bsz, (l - row_off) // tm),
        in_specs=[row(x),
                  pl.BlockSpec((1, tm, d), lambda b, i: (b, i + off_blk, gate_blk)),
                  pl.BlockSpec((1, tm, d), lambda b, i: (b, i + off_blk, gate_blk + 1)),
                  pl.BlockSpec((1, tm, d), lambda b, i: (b, i + off_blk, gate_blk + 2)),
                  row(o_ssd), row(o_swa), row(o_na),
                  full(b_gate), full(w_ssd), full(w_swa), full(w_na), full(w_out),
                  pl.BlockSpec((1, 6, d), lambda b, i: (b, 0, 0)),
                  pl.BlockSpec((1, 6, d), lambda b, i: (bsz, 0, 0))],
        out_specs=pl.BlockSpec((1, tm, d), lambda b, i: (b, i, 0)),
        compiler_params=_cparams(("parallel", "parallel")),
        name="merge",
    )(x, proj, proj, proj, o_ssd, o_swa, o_na, b_gate, w_ssd, w_swa, w_na, w_out, mod, mod)


def _router_kernel(x_ref, ml_ref, mc_ref, g_ref, rw_ref, rb_ref, hid_ref, idx_ref, gate_ref, *, tm, n_ctx, row_off):
    i = pl.program_id(1)
    rows = row_off + i * tm + lax.broadcasted_iota(jnp.int32, (tm, 1), 0)
    is_ctx = rows < n_ctx
    h = _rms_mod(x_ref[0], g_ref[...], _row_mod(ml_ref, mc_ref, 4, is_ctx), _row_mod(ml_ref, mc_ref, 3, is_ctx))
    hid_ref[0] = h
    logits = jnp.dot(h.astype(BF16), rw_ref[...], preferred_element_type=F32) + rb_ref[...]
    lane = lax.broadcasted_iota(jnp.int32, logits.shape, 1)
    vals, idxs = [], []
    for _ in range(TOP_K):
        m = jnp.max(logits, axis=-1, keepdims=True)
        sel = jnp.min(jnp.where(logits == m, lane, N_EXPERTS), axis=-1, keepdims=True)
        vals.append(m)
        idxs.append(sel)
        logits = jnp.where(lane == sel, -jnp.inf, logits)
    ex = [jnp.exp(v - vals[0]) for v in vals]
    denom = ex[0] + ex[1] + ex[2] + ex[3]
    col = lax.broadcasted_iota(jnp.int32, (tm, TOP_K), 1)
    idx_out = jnp.zeros((tm, TOP_K), jnp.int32)
    gate_out = jnp.zeros((tm, TOP_K), F32)
    for k in range(TOP_K):
        idx_out = jnp.where(col == k, idxs[k], idx_out)
        gate_out = jnp.where(col == k, ex[k] / denom, gate_out)
    idx_ref[0] = idx_out
    gate_ref[0] = gate_out


def moe_router(x, mod, g, router_w, router_b, n_ctx, row_off):
    bsz, l, d = x.shape
    tm = 256
    nrow = (l - row_off) // tm
    off_blk = row_off // tm
    return pl.pallas_call(
        functools.partial(_router_kernel, tm=tm, n_ctx=n_ctx, row_off=row_off),
        out_shape=(jax.ShapeDtypeStruct((bsz, l - row_off, d), F32),
                   jax.ShapeDtypeStruct((bsz, l - row_off, TOP_K), jnp.int32),
                   jax.ShapeDtypeStruct((bsz, l - row_off, TOP_K), F32)),
        grid=(bsz, nrow),
        in_specs=[pl.BlockSpec((1, tm, d), lambda b, i: (b, i + off_blk, 0)),
                  pl.BlockSpec((1, 6, d), lambda b, i: (b, 0, 0)),
                  pl.BlockSpec((1, 6, d), lambda b, i: (bsz, 0, 0)),
                  pl.BlockSpec((1, d), lambda b, i: (0, 0)),
                  pl.BlockSpec((d, N_EXPERTS), lambda b, i: (0, 0)),
                  pl.BlockSpec((1, N_EXPERTS), lambda b, i: (0, 0))],
        out_specs=(pl.BlockSpec((1, tm, d), lambda b, i: (b, i, 0)),
                   pl.BlockSpec((1, tm, TOP_K), lambda b, i: (b, i, 0)),
                   pl.BlockSpec((1, tm, TOP_K), lambda b, i: (b, i, 0))),
        compiler_params=_cparams(("parallel", "parallel")),
        name="moe_router",
    )(x, mod, mod, g.reshape(1, d), router_w.astype(BF16), router_b.reshape(1, N_EXPERTS))


def _expert_kernel(be_ref, nu_ref, idx_hbm, hid_hbm, w1_ref, b1_ref, w2_ref, b2_ref, out_hbm,
                   idx_smem, xbuf0, xbuf1, ybuf0, ybuf1, w1b_ref, w2b_ref, idx_sem, in_sem, out_sem, *, n_real_rows):
    i = pl.program_id(0)
    n_used = nu_ref[0]
    xbufs, ybufs = (xbuf0, xbuf1), (ybuf0, ybuf1)
    bm = xbuf0.shape[0]
    n_idx_slots = idx_smem.shape[0]
    de = w2_ref.shape[2]

    def idx_copy(blk):
        sl = blk % n_idx_slots
        return pltpu.make_async_copy(idx_hbm.at[blk], idx_smem.at[sl], idx_sem.at[sl])

    def row_in(blk, par, r):
        tok = idx_smem[blk % n_idx_slots, r]
        return pltpu.make_async_copy(hid_hbm.at[pl.ds(tok, 1)], xbufs[par].at[pl.ds(r, 1)], in_sem.at[par])

    def row_out(blk, par, r):
        dst = idx_smem[blk % n_idx_slots, bm + r]
        return pltpu.make_async_copy(ybufs[par].at[pl.ds(r, 1)], out_hbm.at[pl.ds(dst, 1)], out_sem.at[par])

    def rows_in_wait(par):
        pltpu.make_async_copy(hid_hbm.at[pl.ds(0, bm)], xbufs[par], in_sem.at[par]).wait()

    def rows_out_wait(par):
        pltpu.make_async_copy(ybufs[par], out_hbm.at[pl.ds(0, bm)], out_sem.at[par]).wait()

    def for_rows(copy_of_row):
        def group(gi, carry):
            for u in range(ROW_UNROLL):
                copy_of_row(gi * ROW_UNROLL + u).start(priority=u % 2)
            return carry
        lax.fori_loop(0, bm // ROW_UNROLL, group, 0)

    def experts(par, issue_rows):
        x = xbufs[par][...].astype(BF16)
        n_chunks = EXPERT_CHUNKS
        cw = 2 * de // n_chunks
        rows_per = bm // n_chunks
        gu = []
        for cix in range(n_chunks):
            issue_rows(cix * rows_per, (cix + 1) * rows_per)
            cols = slice(cix * cw, (cix + 1) * cw)
            gu.append(jnp.dot(x, w1b_ref[:, cols], preferred_element_type=F32) + b1_ref[0, 0, :, cols])
        half = n_chunks // 2
        acts = []
        for cix in range(half):
            gate = jnp.minimum(gu[cix], SWIGLU_LIMIT)
            up = jnp.clip(gu[half + cix], -SWIGLU_LIMIT, SWIGLU_LIMIT)
            acts.append(((up + 1.0) * gate * jax.nn.sigmoid(SWIGLU_ALPHA * gate)).astype(BF16))
        act = jnp.concatenate(acts, axis=1)
        ybufs[par][...] = jnp.dot(act, w2b_ref[...], preferred_element_type=F32) + b2_ref[0, 0]

    def step(par):
        other = 1 - par
        if par == 0:
            @pl.when(i == 0)
            def _():
                ybuf1[...] = jnp.zeros(ybuf1.shape, F32)
                fill = pltpu.make_async_copy(ybuf1, out_hbm.at[pl.ds(n_real_rows, bm)], out_sem.at[1])
                fill.start()
                fill.wait()
                idx_copy(0).start()
                idx_copy(0).wait()
                for_rows(lambda r: row_in(0, 0, r))

                @pl.when(n_used > 1)
                def _():
                    idx_copy(1).start()

        @pl.when(jnp.logical_and(i >= 2, i <= n_used))
        def _():
            rows_out_wait(par)

        @pl.when(i + 1 < n_used)
        def _():
            idx_copy(i + 1).wait()

            @pl.when(i + 2 < n_used)
            def _():
                idx_copy(i + 2).start()

        @pl.when(jnp.logical_and(i < n_used, jnp.logical_or(i == 0, be_ref[i] != be_ref[jnp.maximum(i - 1, 0)])))
        def _():
            w1b_ref[...] = w1_ref[0, 0].astype(BF16)
            w2b_ref[...] = w2_ref[0, 0].astype(BF16)

        @pl.when(i < n_used)
        def _():
            rows_in_wait(par)

        steady = jnp.logical_and(i >= 1, i + 1 < n_used)

        @pl.when(steady)
        def _():
            def issue_rows(lo, hi):
                for r in range(lo, hi):
                    row_in(i + 1, other, r).start(priority=r % 2)
                for r in range(lo, hi):
                    row_out(i - 1, other, r).start(priority=r % 2)
            experts(par, issue_rows)

        @pl.when(jnp.logical_not(steady))
        def _():
            @pl.when(i + 1 < n_used)
            def _():
                for_rows(lambda r: row_in(i + 1, other, r))

            @pl.when(jnp.logical_and(i >= 1, i <= n_used))
            def _():
                for_rows(lambda r: row_out(i - 1, other, r))

            @pl.when(i < n_used)
            def _():
                experts(par, lambda lo, hi: None)

            @pl.when(i == n_used)
            def _():
                rows_out_wait(other)

    @pl.when(i % 2 == 0)
    def _():
        step(0)

    @pl.when(i % 2 == 1)
    def _():
        step(1)


def moe_experts(hid, idx, block_e, n_used, w1, b1, w2, b2, layer):
    t, d = hid.shape
    n_blocks = idx.shape[0]
    bm = idx.shape[1] // 2
    _, n_exp, _, two_de = w1.shape
    de = two_de // 2
    any_spec = pl.BlockSpec(memory_space=pl.ANY)
    return pl.pallas_call(
        functools.partial(_expert_kernel, n_real_rows=TOP_K * t),
        out_shape=jax.ShapeDtypeStruct((TOP_K * t + bm, d), F32),
        grid_spec=pltpu.PrefetchScalarGridSpec(
            num_scalar_prefetch=2,
            grid=(n_blocks,),
            in_specs=[any_spec, any_spec,
                      pl.BlockSpec((1, 1, d, two_de), lambda i, be, nu: (layer, be[i], 0, 0)),
                      pl.BlockSpec((1, 1, 1, two_de), lambda i, be, nu: (layer, be[i], 0, 0)),
                      pl.BlockSpec((1, 1, de, d), lambda i, be, nu: (layer, be[i], 0, 0)),
                      pl.BlockSpec((1, 1, 1, d), lambda i, be, nu: (layer, be[i], 0, 0))],
            out_specs=any_spec,
            scratch_shapes=[pltpu.SMEM((4, 2 * bm), jnp.int32),
                            pltpu.VMEM((bm, d), F32), pltpu.VMEM((bm, d), F32),
                            pltpu.VMEM((bm, d), F32), pltpu.VMEM((bm, d), F32),
                            pltpu.VMEM((d, two_de), BF16),
                            pltpu.VMEM((de, d), BF16),
                            pltpu.SemaphoreType.DMA((4,)),
                            pltpu.SemaphoreType.DMA((2,)),
                            pltpu.SemaphoreType.DMA((2,))]),
        compiler_params=_cparams(("arbitrary",)),
        name="moe_experts",
    )(block_e, n_used, idx, hid, w1, b1.reshape(b1.shape[0], n_exp, 1, two_de), w2, b2.reshape(b2.shape[0], n_exp, 1, d))


def _combine_kernel(x_ref, y0_ref, y1_ref, y2_ref, y3_ref, gate_ref, ml_ref, mc_ref, o_ref, *, tm, n_ctx, row_off):
    i = pl.program_id(1)
    rows = row_off + i * tm + lax.broadcasted_iota(jnp.int32, (tm, 1), 0)
    is_ctx = rows < n_ctx
    gates = gate_ref[0]
    y = y0_ref[...] * gates[:, 0:1]
    for k, y_ref in enumerate((y1_ref, y2_ref, y3_ref), start=1):
        y = y + y_ref[...] * gates[:, k:k + 1]
    o_ref[0] = x_ref[0] + _row_mod(ml_ref, mc_ref, 5, is_ctx) * y


def moe_combine(x, ys, gates, mod, n_ctx, row_off):
    bsz, l, d = x.shape
    tm = 256
    nrow = (l - row_off) // tm
    off_blk = row_off // tm
    plane = lambda k: pl.BlockSpec((tm, d), lambda b, i: ((k * bsz + b) * nrow + i, 0))
    return pl.pallas_call(
        functools.partial(_combine_kernel, tm=tm, n_ctx=n_ctx, row_off=row_off),
        out_shape=jax.ShapeDtypeStruct((bsz, l - row_off, d), F32),
        grid=(bsz, nrow),
        in_specs=[pl.BlockSpec((1, tm, d), lambda b, i: (b, i + off_blk, 0)),
                  plane(0), plane(1), plane(2), plane(3),
                  pl.BlockSpec((1, tm, TOP_K), lambda b, i: (b, i, 0)),
                  pl.BlockSpec((1, 6, d), lambda b, i: (b, 0, 0)),
                  pl.BlockSpec((1, 6, d), lambda b, i: (bsz, 0, 0))],
        out_specs=pl.BlockSpec((1, tm, d), lambda b, i: (b, i, 0)),
        compiler_params=_cparams(("parallel", "parallel")),
        name="moe_combine",
    )(x, ys, ys, ys, ys, gates, mod, mod)


def moe_layer(x, mod, g, router_w, router_b, w1, b1, w2, b2, layer, n_ctx, row_off):
    bsz, l, d = x.shape
    hid, top_i, gates = moe_router(x, mod, g, router_w, router_b, n_ctx, row_off)
    t = bsz * (l - row_off)
    bm = MOE_BLOCK_ROWS
    n_blocks = (t * TOP_K + bm - 1) // bm + N_EXPERTS
    flat_e = top_i.reshape(-1)
    order = jnp.argsort(flat_e).astype(jnp.int32)
    counts = jnp.sum((flat_e[:, None] == jnp.arange(N_EXPERTS)[None, :]).astype(jnp.int32), axis=0)
    starts = jnp.cumsum(counts) - counts
    padded = (counts + bm - 1) // bm * bm
    pends = jnp.cumsum(padded)
    pstarts = pends - padded
    blk_first = jnp.arange(n_blocks, dtype=jnp.int32) * bm
    in_expert = jnp.logical_and(blk_first[:, None] >= pstarts[None, :], blk_first[:, None] < pends[None, :])
    pick = lambda v: jnp.sum(jnp.where(in_expert, v[None, :], 0), axis=1)
    block_e = jnp.minimum(jnp.sum((blk_first[:, None] >= pends[None, :]).astype(jnp.int32), axis=1), N_EXPERTS - 1)
    off = blk_first - pick(pstarts)
    n_valid = jnp.clip(pick(counts) - off, 0, bm).astype(jnp.int32)
    n_used = (pends[-1] // bm).astype(jnp.int32).reshape(1)
    r = jnp.arange(bm, dtype=jnp.int32)[None, :]
    valid = r < n_valid[:, None]
    pos = jnp.where(valid, (pick(starts) + off)[:, None] + r, 0)
    src = order[pos]
    tok, kk = src // TOP_K, src % TOP_K
    dst = jnp.where(valid, kk * t + tok, TOP_K * t + r)
    idx = jnp.concatenate([tok, dst], axis=1).astype(jnp.int32)
    ys = moe_experts(hid.reshape(t, d), idx, block_e.astype(jnp.int32), n_used, w1, b1, w2, b2, layer)
    return moe_combine(x, ys, gates, mod, n_ctx, row_off)


def _final_norm_kernel(x_ref, g_ref, o_ref):
    x = x_ref[0]
    o_ref[0] = (x * lax.rsqrt(jnp.mean(x * x, axis=-1, keepdims=True) + NORM_EPS)) * g_ref[...]


def final_norm(x, g):
    bsz, l, d = x.shape
    tm = 512
    return pl.pallas_call(
        _final_norm_kernel,
        out_shape=jax.ShapeDtypeStruct((bsz, l, d), F32),
        grid=(bsz, l // tm),
        in_specs=[pl.BlockSpec((1, tm, d), lambda b, i: (b, i, 0)), pl.BlockSpec((1, d), lambda b, i: (0, 0))],
        out_specs=pl.BlockSpec((1, tm, d), lambda b, i: (b, i, 0)),
        compiler_params=_cparams(("parallel", "parallel")),
        name="final_norm",
    )(x, g.reshape(1, d))


ATT_BLOCK = 128
LOG2_E = 1.4426950408889634
NEG_BIG = -1e30


def _qk(q, k):
    return lax.dot_general(q, k, (((1,), (1,)), ((), ())), preferred_element_type=F32)


def _pair_queries(q2, lane):
    return jnp.concatenate([jnp.where(lane < HEAD_DIM, q2, 0.0), jnp.where(lane < HEAD_DIM, 0.0, q2)], axis=0).astype(BF16)


def _pair_outputs(acc, lane):
    r = acc.shape[0] // 2
    return jnp.where(lane < HEAD_DIM, acc[:r], acc[r:])


def _softmax_pv(scores, values, extra_logit=None):
    m = functools.reduce(jnp.maximum, [jnp.max(s, axis=-1, keepdims=True) for s in scores])
    if extra_logit is not None:
        m = jnp.maximum(m, extra_logit)
    ps = [jnp.exp2(s - m) for s in scores]
    den = functools.reduce(jnp.add, [jnp.sum(p, axis=-1, keepdims=True) for p in ps])
    if extra_logit is not None:
        den = den + jnp.exp2(extra_logit - m)
    acc = functools.reduce(jnp.add, [jnp.dot(p.astype(BF16), v, preferred_element_type=F32) for p, v in zip(ps, values)])
    return acc / den


def _na_kernel(q_ref, k_ref, v_ref, bias_ref, o_ref, *, n_ctx, blk_off, n_dbl):
    blk = pl.program_id(1) + blk_off
    n_ctx_blk = n_ctx // ATT_BLOCK
    s2 = jnp.clip(blk - n_ctx_blk - 2, 0, n_dbl - 5)
    start = pl.multiple_of(n_ctx + s2 * ATT_BLOCK, ATT_BLOCK)
    lane = lax.broadcasted_iota(jnp.int32, (1, 2 * HEAD_DIM), 1)
    scale = HEAD_DIM ** -0.5 * LOG2_E

    def pair_out(p, local):
        cs = slice(p * 2 * HEAD_DIM, (p + 1) * 2 * HEAD_DIM)
        q2 = q_ref[0, :, cs] * scale
        kc = k_ref[0, 0:n_ctx, cs].astype(BF16)
        vc = v_ref[0, 0:n_ctx, cs].astype(BF16)
        if local:
            kl = k_ref[0, pl.ds(start, 5 * ATT_BLOCK), cs].astype(BF16)
            vl = v_ref[0, pl.ds(start, 5 * ATT_BLOCK), cs].astype(BF16)
        qp = _pair_queries(q2, lane)
        scores, values = [_qk(qp, kc)], [vc]
        if local:
            scores.append(_qk(qp, kl) + bias_ref[0, p])
            values.append(vl)
        o_ref[0, :, cs] = _pair_outputs(_softmax_pv(scores, values), lane)

    @pl.when(blk < n_ctx_blk)
    def _():
        for p in range(NA_HEADS // 2):
            pair_out(p, False)

    @pl.when(blk >= n_ctx_blk)
    def _():
        for p in range(NA_HEADS // 2):
            pair_out(p, True)


def _na_bias_tables(rpb, rows):
    n_dbl = rows // 2
    wr = min(NA_WIN_R, rows)
    cidx = np.arange(GRID_W)
    c0 = np.clip(cidx - NA_WIN_C // 2, 0, GRID_W - NA_WIN_C)
    cvalid = (cidx[None, :] >= c0[:, None]) & (cidx[None, :] < c0[:, None] + NA_WIN_C)
    coff = np.clip(cidx[None, :] - cidx[:, None] + (NA_WIN_C - 1), 0, 2 * NA_WIN_C - 2)
    col_onehot = (coff[:, :, None] == np.arange(2 * NA_WIN_C - 1)[None, None, :]) & cvalid[:, :, None]
    pats = (0, 1, 2, n_dbl - 2, n_dbl - 1)
    row_onehot = np.zeros((len(pats), 2, 10, 2 * NA_WIN_R - 1), np.float32)
    for pi, r2 in enumerate(pats):
        s2 = int(np.clip(r2 - 2, 0, n_dbl - 5))
        for qh in range(2):
            qr = 2 * r2 + qh
            r0 = int(np.clip(qr - wr // 2, 0, rows - wr))
            for kk in range(10):
                kr = 2 * s2 + kk
                if r0 <= kr < r0 + wr:
                    row_onehot[pi, qh, kk, kr - qr + NA_WIN_R - 1] = 1.0
    t1 = jnp.einsum('hrc,qkc->hrqk', rpb, jnp.asarray(col_onehot, F32), precision=lax.Precision.HIGHEST)
    bias = jnp.einsum('pajr,hrqk->phaqjk', jnp.asarray(row_onehot), t1, precision=lax.Precision.HIGHEST)
    valid = (row_onehot.sum(-1)[:, None, :, None, :, None] > 0) & cvalid[None, None, None, :, None, :]
    bias = jnp.where(jnp.asarray(valid), bias * LOG2_E, NEG_BIG)
    return bias.reshape(len(pats), rpb.shape[0] // 2, 4 * GRID_W, 10 * GRID_W)


def na_attention(proj, rpb, n_ctx, blk_off):
    bsz, l, _ = proj.shape
    rows = (l - n_ctx) // GRID_W
    n_dbl = rows // 2
    n_ctx_blk = n_ctx // ATT_BLOCK
    bias = _na_bias_tables(rpb, rows)

    def bias_idx(b, i):
        r2 = i + blk_off - n_ctx_blk
        pat = jnp.where(r2 < 2, r2, jnp.where(r2 >= n_dbl - 2, r2 - (n_dbl - 5), 2))
        return (jnp.clip(pat, 0, 4), 0, 0, 0)

    return pl.pallas_call(
        functools.partial(_na_kernel, n_ctx=n_ctx, blk_off=blk_off, n_dbl=n_dbl),
        out_shape=jax.ShapeDtypeStruct((bsz, l - blk_off * ATT_BLOCK, NA_DIM), F32),
        grid=(bsz, l // ATT_BLOCK - blk_off),
        in_specs=[pl.BlockSpec((1, ATT_BLOCK, NA_DIM), lambda b, i: (b, i + blk_off, COL_QN // NA_DIM)),
                  pl.BlockSpec((1, l, NA_DIM), lambda b, i: (b, 0, COL_KN // NA_DIM)),
                  pl.BlockSpec((1, l, NA_DIM), lambda b, i: (b, 0, COL_VN // NA_DIM)),
                  pl.BlockSpec((1, NA_HEADS // 2, 2 * ATT_BLOCK, 5 * ATT_BLOCK), bias_idx)],
        out_specs=pl.BlockSpec((1, ATT_BLOCK, NA_DIM), lambda b, i: (b, i, 0)),
        compiler_params=_cparams(("parallel", "arbitrary")),
        name="na_attention",
    )(proj, proj, proj, bias)


def _swa_kernel(sink_ref, q_ref, k_ref, v_ref, cq_ref, sq_ref, ck_ref, sk_ref, o_ref, *, n_ctx, blk_off, n_blk):
    blk = pl.program_id(1) + blk_off
    n_ctx_blk = n_ctx // ATT_BLOCK
    n = blk - n_ctx_blk
    first = jnp.clip(n - 1, 0, n_blk - 3)
    kstart = pl.multiple_of(first * ATT_BLOCK, ATT_BLOCK)
    lane = lax.broadcasted_iota(jnp.int32, (1, 2 * HEAD_DIM), 1)
    scale = HEAD_DIM ** -0.5 * LOG2_E
    quarter = HEAD_DIM // 4

    def rope(u, cos, sin):
        w = u.shape[-1]
        li = lax.broadcasted_iota(jnp.int32, (1, w), 1)
        swapped = jnp.where(li % (2 * quarter) < quarter, pltpu.roll(u, w - quarter, 1), pltpu.roll(u, quarter, 1))
        return u * cos + swapped * sin

    def dup_head(x, h):
        other = pltpu.roll(x, HEAD_DIM, 1)
        return jnp.where((lane < HEAD_DIM) == (h == 0), x, other)

    def run(local):
        q = q_ref[0] * scale
        kc_all = k_ref[0, 0:n_ctx, :]
        vc_all = v_ref[0, 0:n_ctx, :]
        if local:
            q = rope(q, cq_ref[...], sq_ref[...])
            kl_all = rope(k_ref[0, pl.ds(n_ctx + kstart, 3 * ATT_BLOCK), :],
                          ck_ref[pl.ds(kstart, 3 * ATT_BLOCK), :], sk_ref[pl.ds(kstart, 3 * ATT_BLOCK), :])
            vl_all = v_ref[0, pl.ds(n_ctx + kstart, 3 * ATT_BLOCK), :]
            qrow = lax.broadcasted_iota(jnp.int32, (2 * ATT_BLOCK, 3 * ATT_BLOCK), 0) % ATT_BLOCK
            kpos = kstart + lax.broadcasted_iota(jnp.int32, (2 * ATT_BLOCK, 3 * ATT_BLOCK), 1)
            band = jnp.where(jnp.abs(kpos - (n * ATT_BLOCK + qrow)) <= SWA_WINDOW, 0.0, NEG_BIG)
        group = SWA_Q_HEADS // SWA_KV_HEADS
        first_head = lax.broadcasted_iota(jnp.int32, (2 * ATT_BLOCK, 1), 0) < ATT_BLOCK
        for h in range(SWA_KV_HEADS):
            kc = dup_head(kc_all, h).astype(BF16)
            vc = dup_head(vc_all, h).astype(BF16)
            if local:
                kl = dup_head(kl_all, h).astype(BF16)
                vl = dup_head(vl_all, h).astype(BF16)
            for p in range(h * group // 2, (h + 1) * group // 2):
                cs = slice(p * 2 * HEAD_DIM, (p + 1) * 2 * HEAD_DIM)
                qp = _pair_queries(q[:, cs], lane)
                scores, values = [_qk(qp, kc)], [vc]
                if local:
                    scores.append(_qk(qp, kl) + band)
                    values.append(vl)
                sink = jnp.where(first_head, sink_ref[2 * p], sink_ref[2 * p + 1]) * LOG2_E
                o_ref[0, :, cs] = _pair_outputs(_softmax_pv(scores, values, extra_logit=sink), lane)

    @pl.when(blk < n_ctx_blk)
    def _():
        run(False)

    @pl.when(blk >= n_ctx_blk)
    def _():
        run(True)


def _rope_tables(s):
    pos = np.arange(s)
    quarter = HEAD_DIM // 4
    inv = ROPE_THETA ** (-jnp.arange(0, 2 * quarter, 2, dtype=F32) / (2 * quarter))

    def axis_tables(p):
        ang = jnp.asarray(p, F32)[:, None] * inv[None, :]
        c, sn = jnp.cos(ang), jnp.sin(ang)
        return jnp.concatenate([c, c], axis=1), jnp.concatenate([-sn, sn], axis=1)
    cr, sr = axis_tables(pos // GRID_W)
    cc, sc = axis_tables(pos % GRID_W)
    return jnp.concatenate([cr, cc], axis=1), jnp.concatenate([sr, sc], axis=1)


def swa_attention(proj, sink, n_ctx, blk_off):
    bsz, l, _ = proj.shape
    s = l - n_ctx
    n_blk = s // ATT_BLOCK
    n_ctx_blk = n_ctx // ATT_BLOCK
    cos, sin = _rope_tables(s)
    cq, sq = jnp.tile(cos, (1, SWA_Q_HEADS)), jnp.tile(sin, (1, SWA_Q_HEADS))
    ck, sk = jnp.tile(cos, (1, SWA_KV_HEADS)), jnp.tile(sin, (1, SWA_KV_HEADS))
    qtab = lambda b, i: (jnp.maximum(i + blk_off - n_ctx_blk, 0), 0)
    return pl.pallas_call(
        functools.partial(_swa_kernel, n_ctx=n_ctx, blk_off=blk_off, n_blk=n_blk),
        out_shape=jax.ShapeDtypeStruct((bsz, l - blk_off * ATT_BLOCK, SWA_Q), F32),
        grid=(bsz, l // ATT_BLOCK - blk_off),
        in_specs=[pl.BlockSpec(memory_space=pltpu.SMEM),
                  pl.BlockSpec((1, ATT_BLOCK, SWA_Q), lambda b, i: (b, i + blk_off, COL_QS // SWA_Q)),
                  pl.BlockSpec((1, l, SWA_KV), lambda b, i: (b, 0, COL_KS // SWA_KV)),
                  pl.BlockSpec((1, l, SWA_KV), lambda b, i: (b, 0, COL_VS // SWA_KV)),
                  pl.BlockSpec((ATT_BLOCK, SWA_Q), qtab),
                  pl.BlockSpec((ATT_BLOCK, SWA_Q), qtab),
                  pl.BlockSpec((s, SWA_KV), lambda b, i: (0, 0)),
                  pl.BlockSpec((s, SWA_KV), lambda b, i: (0, 0))],
        out_specs=pl.BlockSpec((1, ATT_BLOCK, SWA_Q), lambda b, i: (b, i, 0)),
        compiler_params=_cparams(("parallel", "arbitrary")),
        name="swa_attention",
    )(sink, proj, proj, proj, cq, sq, ck, sk)


CONV_HALO = 8


def _ssd_kernel(*refs, reverse, n_ctx, n_chunks):
    if reverse:
        xsc_ref, bcc_ref, dt_ref, dtbias_ref, acoef_ref, z_ref, yf_ref, dskip_ref, ng_ref, o_ref, state_ref = refs
    else:
        (xs_c, xs_p, xs_n, bc_c, bc_p, bc_n, dt_ref, cwx_ref, cbx_ref, cwb_ref, cbb_ref, dtbias_ref, acoef_ref,
         o_ref, xsc_ref, bcc_ref, state_ref) = refs
    q = SSD_CHUNK
    n_st = SSD_STATE
    gw = SSD_INNER // SSD_GROUPS
    j = pl.program_id(1)
    c = _ssd_chunk_of_step(j, reverse, n_ctx // q, n_chunks)
    n_cc = n_ctx // q
    seg_first = jnp.logical_or(c == 0, c == n_cc)
    seg_last = jnp.logical_or(c == n_cc - 1, c == n_chunks - 1)

    @pl.when(j == 0)
    def _():
        state_ref[...] = jnp.zeros_like(state_ref)

    def conv_silu(cur_ref, prev_ref, next_ref, w_ref, b_ref):
        half = w_ref.shape[0] // 2
        prev = jnp.where(seg_first, 0.0, prev_ref[0])
        nxt = jnp.where(seg_last, 0.0, next_ref[0])
        ext = jnp.concatenate([prev, cur_ref[0], nxt], axis=0)
        acc = b_ref[...]
        for k in range(w_ref.shape[0]):
            o = CONV_HALO - half + k
            acc = acc + ext[o:o + q] * w_ref[k:k + 1, :]
        return acc * jax.nn.sigmoid(acc)

    if reverse:
        xs, bcv = xsc_ref[0], bcc_ref[0]
    else:
        xs = conv_silu(xs_c, xs_p, xs_n, cwx_ref, cbx_ref)
        bcv = conv_silu(bc_c, bc_p, bc_n, cwb_ref, cbb_ref)
        xsc_ref[0] = xs
        bcc_ref[0] = bcv

    lo = SSD_HEADS if reverse else 0
    dt_raw = dt_ref[0] + dtbias_ref[...]
    dt_all = jnp.maximum(dt_raw, 0.0) + jnp.log(1.0 + jnp.exp(-jnp.abs(dt_raw)))
    a_all = dt_all * acoef_ref[...]
    ri = lax.broadcasted_iota(jnp.int32, (q, q), 0)
    ci = lax.broadcasted_iota(jnp.int32, (q, q), 1)
    causal = (ci >= ri) if reverse else (ci <= ri)
    a_cs = jnp.dot(causal.astype(F32), a_all, precision=lax.Precision.HIGHEST, preferred_element_type=F32)
    total = a_cs[0:1] if reverse else a_cs[q - 1:q]
    a_cs_t = a_cs.T

    lane = lax.broadcasted_iota(jnp.int32, (1, 2 * SSD_HEAD_DIM), 1)

    def expand(v):
        parts = [jnp.where(lane < SSD_HEAD_DIM, v[:, lo + 2 * p:lo + 2 * p + 1], v[:, lo + 2 * p + 1:lo + 2 * p + 2])
                 for p in range(SSD_HEADS // 2)]
        return jnp.concatenate(parts, axis=1)

    xdt = xs * expand(dt_all)
    dec_start = expand(jnp.exp(a_cs))
    dec_end = expand(jnp.exp(total - a_cs))
    dec_total = expand(jnp.exp(total))

    heads_per_group = SSD_HEADS // SSD_GROUPS
    y_parts = []
    for g in range(SSD_GROUPS):
        b_g = bcv[:, g * n_st:(g + 1) * n_st]
        c_g = bcv[:, (SSD_GROUPS + g) * n_st:(SSD_GROUPS + g + 1) * n_st].astype(BF16)
        cb = _qk(c_g, b_g.astype(BF16))
        cols = slice(g * gw, (g + 1) * gw)
        st = state_ref[:, cols]
        y_off = jnp.dot(c_g, st.astype(BF16), preferred_element_type=F32) * dec_start[:, cols]
        diag = []
        for p in range(g * heads_per_group // 2, (g + 1) * heads_per_group // 2):
            x2 = xdt[:, p * 2 * SSD_HEAD_DIM:(p + 1) * 2 * SSD_HEAD_DIM].astype(BF16)
            outs = []
            for hh in range(2):
                h = lo + 2 * p + hh
                seg = jnp.where(causal, a_cs[:, h:h + 1] - a_cs_t[h:h + 1, :], NEG_BIG)
                m = (cb * jnp.exp(seg)).astype(BF16)
                outs.append(jnp.dot(m, x2, preferred_element_type=F32))
            diag.append(jnp.where(lane < SSD_HEAD_DIM, outs[0], outs[1]))
        y_parts.append(jnp.concatenate(diag, axis=1) + y_off)
        xw = (xdt[:, cols] * dec_end[:, cols]).astype(BF16)
        state_ref[:, cols] = dec_total[:, cols] * st + jnp.dot(b_g.T.astype(BF16), xw, preferred_element_type=F32)
    y = jnp.concatenate(y_parts, axis=1)

    if reverse:
        y = yf_ref[0] + y + dskip_ref[...] * xs
        z = z_ref[0]
        y = y * (z * jax.nn.sigmoid(z))
        o_ref[0] = (y * lax.rsqrt(jnp.mean(y * y, axis=-1, keepdims=True) + NORM_EPS)) * ng_ref[...]
    else:
        o_ref[0] = y


def _ssd_chunk_of_step(j, reverse, n_ctx_chunks, n_chunks):
    if not reverse:
        return j
    return jnp.where(j < n_ctx_chunks, n_ctx_chunks - 1 - j, n_chunks - 1 - (j - n_ctx_chunks))


def ssd_mixer(proj, conv_w, conv_b, a_log, dt_bias, d_skip, norm_g, n_ctx):
    bsz, l, _ = proj.shape
    q = SSD_CHUNK
    n_chunks = l // q
    n_cc = n_ctx // q
    hb = q // CONV_HALO
    pad = lambda v: jnp.concatenate([v.reshape(1, -1), jnp.zeros((1, 128 - v.size), F32)], axis=1)
    dtb = pad(dt_bias)
    acoef = pad(-jnp.exp(a_log))
    cwx, cwb = conv_w[:, :SSD_INNER], conv_w[:, SSD_INNER:]
    cbx, cbb = conv_b[:SSD_INNER].reshape(1, -1), conv_b[SSD_INNER:].reshape(1, -1)
    wbc = cwb.shape[1]
    dsk = jnp.repeat(d_skip, SSD_HEAD_DIM).reshape(1, SSD_INNER)
    ng = norm_g.reshape(1, SSD_INNER)
    full = lambda a: pl.BlockSpec(a.shape, lambda b, jj: (0, 0))

    def call(reverse, inputs, in_specs, out_widths):
        chunk = lambda jj: _ssd_chunk_of_step(jj, reverse, n_cc, n_chunks)
        cur = lambda w, off: pl.BlockSpec((1, q, w), lambda b, jj: (b, chunk(jj), off // w))
        prev = lambda w, off: pl.BlockSpec((1, CONV_HALO, w), lambda b, jj: (b, jnp.maximum(chunk(jj) * hb - 1, 0), off // w))
        nxt = lambda w, off: pl.BlockSpec((1, CONV_HALO, w),
                                          lambda b, jj: (b, jnp.minimum((chunk(jj) + 1) * hb, l // CONV_HALO - 1), off // w))
        return pl.pallas_call(
            functools.partial(_ssd_kernel, reverse=reverse, n_ctx=n_ctx, n_chunks=n_chunks),
            out_shape=tuple(jax.ShapeDtypeStruct((bsz, l, w), F32) for w in out_widths),
            grid=(bsz, n_chunks),
            in_specs=in_specs(cur, prev, nxt),
            out_specs=tuple(cur(w, 0) for w in out_widths),
            scratch_shapes=[pltpu.VMEM((SSD_STATE, SSD_INNER), F32)],
            compiler_params=_cparams(("parallel", "arbitrary")),
            name="ssd_bwd" if reverse else "ssd_fwd",
        )(*inputs)

    y_f, xs_conv, bc_conv = call(
        False, (proj, proj, proj, proj, proj, proj, proj, cwx, cbx, cwb, cbb, dtb, acoef),
        lambda cur, prev, nxt: [cur(SSD_INNER, COL_XS), prev(SSD_INNER, COL_XS), nxt(SSD_INNER, COL_XS),
                                cur(wbc, COL_BC), prev(wbc, COL_BC), nxt(wbc, COL_BC),
                                cur(128, COL_DT), full(cwx), full(cbx), full(cwb), full(cbb), full(dtb), full(acoef)],
        (SSD_INNER, SSD_INNER, wbc))
    (out,) = call(
        True, (xs_conv, bc_conv, proj, dtb, acoef, proj, y_f, dsk, ng),
        lambda cur, prev, nxt: [cur(SSD_INNER, 0), cur(wbc, 0), cur(128, COL_DT), full(dtb), full(acoef),
                                cur(SSD_INNER, COL_Z), cur(SSD_INNER, 0), full(dsk), full(ng)],
        (SSD_INNER,))
    return out


def _pack_w_in(w):
    o = 0
    segs = {}
    for name, width in (('z', SSD_INNER), ('xs', SSD_INNER), ('bc', 512), ('dt', 2 * SSD_HEADS), ('qs', SWA_Q),
                        ('ks', SWA_KV), ('vs', SWA_KV), ('qn', NA_DIM), ('kn', NA_DIM), ('vn', NA_DIM),
                        ('g', N_BRANCH * D_MODEL)):
        segs[name] = w[:, o:o + width]
        o += width
    pad = jnp.zeros((w.shape[0], 128 - 2 * SSD_HEADS), w.dtype)
    order = ('g', 'z', 'xs', 'bc', 'qs', 'qn', 'kn', 'vn', 'ks', 'vs', 'dt')
    return jnp.concatenate([segs[k] for k in order] + [pad], axis=1).astype(BF16)


def kernel(x, c, ctx, c_ctx, norm1_g, norm2_g, w_ada, b_ada, w_in, b_gate, ssd_conv_w, ssd_conv_b, ssd_a_log,
           ssd_dt_bias, ssd_d, ssd_norm_g, swa_sink, na_rpb, w_br_ssd, w_br_swa, w_br_na, w_out, router_w, router_b,
           moe_w1, moe_b1, moe_w2, moe_b2, final_norm_g):
    bsz, s, d = x.shape
    lc = ctx.shape[1]
    depth = w_in.shape[0]
    n_mod_rows = 16
    cond = jnp.concatenate([c, c_ctx[None, :], jnp.zeros((n_mod_rows - bsz - 1, d), F32)], axis=0)
    xs = jnp.concatenate([ctx, x], axis=1)
    for i in range(depth):
        last = i == depth - 1
        mod = ada_modulation(cond, w_ada, b_ada, i).reshape(n_mod_rows, 6, d)
        proj = norm_proj(xs, mod, norm1_g[i], _pack_w_in(w_in[i]), lc)
        o_ssd = ssd_mixer(proj, ssd_conv_w[i], ssd_conv_b[i], ssd_a_log[i], ssd_dt_bias[i], ssd_d[i], ssd_norm_g[i], lc)
        blk_off = lc // ATT_BLOCK if last else 0
        o_swa = swa_attention(proj, swa_sink[i], lc, blk_off)
        o_na = na_attention(proj, na_rpb[i], lc, blk_off)
        xs = merge(xs, proj, o_ssd, o_swa, o_na, b_gate[i].reshape(N_BRANCH, d), w_br_ssd[i].astype(BF16),
                   w_br_swa[i].astype(BF16), w_br_na[i].astype(BF16), w_out[i].astype(BF16), mod, lc,
                   lc if last else 0)
        xs = moe_layer(xs, mod, norm2_g[i], router_w[i], router_b[i], moe_w1, moe_b1, moe_w2, moe_b2, i,
                       0 if last else lc, 0)
    return final_norm(xs, final_norm_g)
```

```python
import functools

import jax
import jax.numpy as jnp
import numpy as np
from jax import lax
from jax.experimental import pallas as pl
from jax.experimental.pallas import tpu as pltpu

F32 = jnp.float32
BF16 = jnp.bfloat16

D_MODEL = 1024
GRID_W = 64
HEAD_DIM = 64
NORM_EPS = 1e-6

SSD_HEADS = 16
SSD_HEAD_DIM = 64
SSD_INNER = SSD_HEADS * SSD_HEAD_DIM
SSD_GROUPS = 2
SSD_STATE = 128
SSD_CHUNK = 128

SWA_Q_HEADS = 8
SWA_KV_HEADS = 2
SWA_WINDOW = 128
SWA_Q = SWA_Q_HEADS * HEAD_DIM
SWA_KV = SWA_KV_HEADS * HEAD_DIM
ROPE_THETA = 10000.0

NA_HEADS = 8
NA_WIN_R = 8
NA_WIN_C = 16
NA_DIM = NA_HEADS * HEAD_DIM

N_BRANCH = 3
N_EXPERTS = 32
TOP_K = 4
SWIGLU_LIMIT = 7.0
SWIGLU_ALPHA = 1.702

COL_GATE = 0
COL_Z = 3 * D_MODEL
COL_XS = COL_Z + SSD_INNER
COL_BC = COL_XS + SSD_INNER
COL_QS = COL_BC + 512
COL_QN = COL_QS + SWA_Q
COL_KN = COL_QN + NA_DIM
COL_VN = COL_KN + NA_DIM
COL_KS = COL_VN + NA_DIM
COL_VS = COL_KS + SWA_KV
COL_DT = COL_VS + SWA_KV
IN_COLS_PACKED = COL_DT + 128

MOE_BLOCK_ROWS = 256
ROW_UNROLL = 8
EXPERT_CHUNKS = 8
VMEM_LIMIT = 56 * 1024 * 1024


def _cparams(sem):
    return pltpu.CompilerParams(dimension_semantics=sem, vmem_limit_bytes=VMEM_LIMIT)


def _ada_kernel(c_ref, w_ref, b_ref, o_ref):
    c = c_ref[...]
    s = (c * jax.nn.sigmoid(c)).astype(BF16)
    o_ref[...] = jnp.dot(s, w_ref[0].astype(BF16), preferred_element_type=F32) + b_ref[0]


def ada_modulation(rows, w_ada, b_ada, layer):
    r, d = rows.shape
    n = w_ada.shape[2]
    tn = 1024
    return pl.pallas_call(
        _ada_kernel,
        out_shape=jax.ShapeDtypeStruct((r, n), F32),
        grid=(n // tn,),
        in_specs=[pl.BlockSpec((r, d), lambda j: (0, 0)),
                  pl.BlockSpec((1, d, tn), lambda j: (layer, 0, j)),
                  pl.BlockSpec((1, 1, tn), lambda j: (layer, 0, j))],
        out_specs=pl.BlockSpec((r, tn), lambda j: (0, j)),
        compiler_params=_cparams(("parallel",)),
        name="ada_modulation",
    )(rows, w_ada, b_ada.reshape(b_ada.shape[0], 1, n))


def _row_mod(mod_l_ref, mod_c_ref, idx, is_ctx):
    return jnp.where(is_ctx, mod_c_ref[0, idx:idx + 1, :], mod_l_ref[0, idx:idx + 1, :])


def _rms_mod(x, g, scale, shift):
    y = x * lax.rsqrt(jnp.mean(x * x, axis=-1, keepdims=True) + NORM_EPS)
    return (y * g) * (1.0 + scale) + shift


def _norm_proj_kernel(x_ref, ml_ref, mc_ref, g_ref, w_ref, o_ref, h_ref, *, tm, n_ctx):
    i = pl.program_id(1)
    j = pl.program_id(2)

    @pl.when(j == 0)
    def _():
        rows = i * tm + lax.broadcasted_iota(jnp.int32, (tm, 1), 0)
        is_ctx = rows < n_ctx
        h = _rms_mod(x_ref[0], g_ref[...], _row_mod(ml_ref, mc_ref, 1, is_ctx), _row_mod(ml_ref, mc_ref, 0, is_ctx))
        h_ref[...] = h.astype(BF16)

    o_ref[0] = jnp.dot(h_ref[...], w_ref[...], preferred_element_type=F32)


def norm_proj(x, mod, g, w_packed, n_ctx):
    bsz, l, d = x.shape
    n = w_packed.shape[1]
    tm, tn = 768, 2688
    return pl.pallas_call(
        functools.partial(_norm_proj_kernel, tm=tm, n_ctx=n_ctx),
        out_shape=jax.ShapeDtypeStruct((bsz, l, n), F32),
        grid=(bsz, l // tm, n // tn),
        in_specs=[pl.BlockSpec((1, tm, d), lambda b, i, j: (b, i, 0)),
                  pl.BlockSpec((1, 6, d), lambda b, i, j: (b, 0, 0)),
                  pl.BlockSpec((1, 6, d), lambda b, i, j: (bsz, 0, 0)),
                  pl.BlockSpec((1, d), lambda b, i, j: (0, 0)),
                  pl.BlockSpec((d, tn), lambda b, i, j: (0, j))],
        out_specs=pl.BlockSpec((1, tm, tn), lambda b, i, j: (b, i, j)),
        scratch_shapes=[pltpu.VMEM((tm, d), BF16)],
        compiler_params=_cparams(("parallel", "parallel", "arbitrary")),
        name="norm_proj",
    )(x, mod, mod, g.reshape(1, d), w_packed)


def _merge_kernel(x_ref, ga_ref, gb_ref, gn_ref, ossd_ref, oswa_ref, ona_ref, bg_ref, wssd_ref, wswa_ref, wna_ref,
                  wout_ref, g2_ref, rw_ref, rb_ref, ml_ref, mc_ref, o_ref, hid_ref, idx_ref, gate_ref,
                  *, tm, n_ctx, row_off):
    i = pl.program_id(1)
    rows = row_off + i * tm + lax.broadcasted_iota(jnp.int32, (tm, 1), 0)
    is_ctx = rows < n_ctx

    def branch(gate_ref, k, o_br_ref, w_ref):
        gate = jax.nn.sigmoid(gate_ref[0] + bg_ref[k:k + 1, :])
        return gate * jnp.dot(o_br_ref[0].astype(BF16), w_ref[...], preferred_element_type=F32)

    m = branch(ga_ref, 0, ossd_ref, wssd_ref) + branch(gb_ref, 1, oswa_ref, wswa_ref) + branch(gn_ref, 2, ona_ref, wna_ref)
    y = jnp.dot(m.astype(BF16), wout_ref[...], preferred_element_type=F32)
    x_new = x_ref[0] + _row_mod(ml_ref, mc_ref, 2, is_ctx) * y
    o_ref[0] = x_new
    h = _rms_mod(x_new, g2_ref[...], _row_mod(ml_ref, mc_ref, 4, is_ctx), _row_mod(ml_ref, mc_ref, 3, is_ctx))
    hid_ref[0] = h
    _route_top_k(h, rw_ref, rb_ref, idx_ref, gate_ref)


def merge(x, proj, o_ssd, o_swa, o_na, b_gate, w_ssd, w_swa, w_na, w_out, norm2_g, router_w, router_b, mod, n_ctx,
          row_off):
    bsz, l, d = x.shape
    g2, rw, rb = norm2_g.reshape(1, d), router_w.astype(BF16), router_b.reshape(1, N_EXPERTS)
    tm = 256
    gate_blk = COL_GATE // d
    off_blk = row_off // tm

    def row(a):
        o = off_blk if a.shape[1] == l else 0
        return pl.BlockSpec((1, tm, a.shape[-1]), lambda b, i: (b, i + o, 0))

    full = lambda a: pl.BlockSpec(a.shape, lambda b, i: (0,) * a.ndim)
    out_rows = lambda w, dt: jax.ShapeDtypeStruct((bsz, l - row_off, w), dt)
    out_blk = lambda w: pl.BlockSpec((1, tm, w), lambda b, i: (b, i, 0))
    return pl.pallas_call(
        functools.partial(_merge_kernel, tm=tm, n_ctx=n_ctx, row_off=row_off),
        out_shape=(out_rows(d, F32), out_rows(d, F32), out_rows(TOP_K, jnp.int32), out_rows(TOP_K, F32)),
        grid=(bsz, (l - row_off) // tm),
        in_specs=[row(x),
                  pl.BlockSpec((1, tm, d), lambda b, i: (b, i + off_blk, gate_blk)),
                  pl.BlockSpec((1, tm, d), lambda b, i: (b, i + off_blk, gate_blk + 1)),
                  pl.BlockSpec((1, tm, d), lambda b, i: (b, i + off_blk, gate_blk + 2)),
                  row(o_ssd), row(o_swa), row(o_na),
                  full(b_gate), full(w_ssd), full(w_swa), full(w_na), full(w_out), full(g2), full(rw), full(rb),
                  pl.BlockSpec((1, 6, d), lambda b, i: (b, 0, 0)),
                  pl.BlockSpec((1, 6, d), lambda b, i: (bsz, 0, 0))],
        out_specs=(out_blk(d), out_blk(d), out_blk(TOP_K), out_blk(TOP_K)),
        compiler_params=_cparams(("parallel", "parallel")),
        name="merge",
    )(x, proj, proj, proj, o_ssd, o_swa, o_na, b_gate, w_ssd, w_swa, w_na, w_out, g2, rw, rb, mod, mod)


def _route_top_k(h, rw_ref, rb_ref, idx_ref, gate_ref):
    tm = h.shape[0]
    logits = jnp.dot(h.astype(BF16), rw_ref[...], preferred_element_type=F32) + rb_ref[...]
    lane = lax.broadcasted_iota(jnp.int32, logits.shape, 1)
    vals, idxs = [], []
    for _ in range(TOP_K):
        m = jnp.max(logits, axis=-1, keepdims=True)
        sel = jnp.min(jnp.where(logits == m, lane, N_EXPERTS), axis=-1, keepdims=True)
        vals.append(m)
        idxs.append(sel)
        logits = jnp.where(lane == sel, -jnp.inf, logits)
    ex = [jnp.exp(v - vals[0]) for v in vals]
    denom = ex[0] + ex[1] + ex[2] + ex[3]
    col = lax.broadcasted_iota(jnp.int32, (tm, TOP_K), 1)
    idx_out = jnp.zeros((tm, TOP_K), jnp.int32)
    gate_out = jnp.zeros((tm, TOP_K), F32)
    for k in range(TOP_K):
        idx_out = jnp.where(col == k, idxs[k], idx_out)
        gate_out = jnp.where(col == k, ex[k] / denom, gate_out)
    idx_ref[0] = idx_out
    gate_ref[0] = gate_out


def _expert_kernel(be_ref, nu_ref, idx_hbm, hid_hbm, w1_ref, b1_ref, w2_ref, b2_ref, out_hbm,
                   idx_smem, xbuf0, xbuf1, ybuf0, ybuf1, w1b_ref, w2b_ref, idx_sem, in_sem, out_sem, *, n_real_rows):
    i = pl.program_id(0)
    n_used = nu_ref[0]
    xbufs, ybufs = (xbuf0, xbuf1), (ybuf0, ybuf1)
    bm = xbuf0.shape[0]
    n_idx_slots = idx_smem.shape[0]
    de = w2_ref.shape[2]

    def idx_copy(blk):
        sl = blk % n_idx_slots
        return pltpu.make_async_copy(idx_hbm.at[blk], idx_smem.at[sl], idx_sem.at[sl])

    def row_in(blk, par, r):
        tok = idx_smem[blk % n_idx_slots, r]
        return pltpu.make_async_copy(hid_hbm.at[pl.ds(tok, 1)], xbufs[par].at[pl.ds(r, 1)], in_sem.at[par])

    def row_out(blk, par, r):
        dst = idx_smem[blk % n_idx_slots, bm + r]
        return pltpu.make_async_copy(ybufs[par].at[pl.ds(r, 1)], out_hbm.at[pl.ds(dst, 1)], out_sem.at[par])

    def rows_in_wait(par):
        pltpu.make_async_copy(hid_hbm.at[pl.ds(0, bm)], xbufs[par], in_sem.at[par]).wait()

    def rows_out_wait(par):
        pltpu.make_async_copy(ybufs[par], out_hbm.at[pl.ds(0, bm)], out_sem.at[par]).wait()

    def for_rows(copy_of_row):
        def group(gi, carry):
            for u in range(ROW_UNROLL):
                copy_of_row(gi * ROW_UNROLL + u).start(priority=u % 2)
            return carry
        lax.fori_loop(0, bm // ROW_UNROLL, group, 0)

    def experts(par, issue_rows):
        x = xbufs[par][...].astype(BF16)
        n_chunks = EXPERT_CHUNKS
        cw = 2 * de // n_chunks
        rows_per = bm // n_chunks
        gu = []
        for cix in range(n_chunks):
            issue_rows(cix * rows_per, (cix + 1) * rows_per)
            cols = slice(cix * cw, (cix + 1) * cw)
            gu.append(jnp.dot(x, w1b_ref[:, cols], preferred_element_type=F32) + b1_ref[0, 0, :, cols])
        half = n_chunks // 2
        acts = []
        for cix in range(half):
            gate = jnp.minimum(gu[cix], SWIGLU_LIMIT)
            up = jnp.clip(gu[half + cix], -SWIGLU_LIMIT, SWIGLU_LIMIT)
            acts.append(((up + 1.0) * gate * jax.nn.sigmoid(SWIGLU_ALPHA * gate)).astype(BF16))
        act = jnp.concatenate(acts, axis=1)
        ybufs[par][...] = jnp.dot(act, w2b_ref[...], preferred_element_type=F32) + b2_ref[0, 0]

    def step(par):
        other = 1 - par
        if par == 0:
            @pl.when(i == 0)
            def _():
                ybuf1[...] = jnp.zeros(ybuf1.shape, F32)
                fill = pltpu.make_async_copy(ybuf1, out_hbm.at[pl.ds(n_real_rows, bm)], out_sem.at[1])
                fill.start()
                fill.wait()
                idx_copy(0).start()
                idx_copy(0).wait()
                for_rows(lambda r: row_in(0, 0, r))

                @pl.when(n_used > 1)
                def _():
                    idx_copy(1).start()

        @pl.when(jnp.logical_and(i >= 2, i <= n_used))
        def _():
            rows_out_wait(par)

        @pl.when(i + 1 < n_used)
        def _():
            idx_copy(i + 1).wait()

            @pl.when(i + 2 < n_used)
            def _():
                idx_copy(i + 2).start()

        @pl.when(jnp.logical_and(i < n_used, jnp.logical_or(i == 0, be_ref[i] != be_ref[jnp.maximum(i - 1, 0)])))
        def _():
            w1b_ref[...] = w1_ref[0, 0].astype(BF16)
            w2b_ref[...] = w2_ref[0, 0].astype(BF16)

        @pl.when(i < n_used)
        def _():
            rows_in_wait(par)

        steady = jnp.logical_and(i >= 1, i + 1 < n_used)

        @pl.when(steady)
        def _():
            def issue_rows(lo, hi):
                for r in range(lo, hi):
                    row_in(i + 1, other, r).start(priority=r % 2)
                for r in range(lo, hi):
                    row_out(i - 1, other, r).start(priority=r % 2)
            experts(par, issue_rows)

        @pl.when(jnp.logical_not(steady))
        def _():
            @pl.when(i + 1 < n_used)
            def _():
                for_rows(lambda r: row_in(i + 1, other, r))

            @pl.when(jnp.logical_and(i >= 1, i <= n_used))
            def _():
                for_rows(lambda r: row_out(i - 1, other, r))

            @pl.when(i < n_used)
            def _():
                experts(par, lambda lo, hi: None)

            @pl.when(i == n_used)
            def _():
                rows_out_wait(other)

    @pl.when(i % 2 == 0)
    def _():
        step(0)

    @pl.when(i % 2 == 1)
    def _():
        step(1)


def moe_experts(hid, idx, block_e, n_used, w1, b1, w2, b2, layer):
    t, d = hid.shape
    n_blocks = idx.shape[0]
    bm = idx.shape[1] // 2
    _, n_exp, _, two_de = w1.shape
    de = two_de // 2
    any_spec = pl.BlockSpec(memory_space=pl.ANY)
    return pl.pallas_call(
        functools.partial(_expert_kernel, n_real_rows=TOP_K * t),
        out_shape=jax.ShapeDtypeStruct((TOP_K * t + bm, d), F32),
        grid_spec=pltpu.PrefetchScalarGridSpec(
            num_scalar_prefetch=2,
            grid=(n_blocks,),
            in_specs=[any_spec, any_spec,
                      pl.BlockSpec((1, 1, d, two_de), lambda i, be, nu: (layer, be[i], 0, 0)),
                      pl.BlockSpec((1, 1, 1, two_de), lambda i, be, nu: (layer, be[i], 0, 0)),
                      pl.BlockSpec((1, 1, de, d), lambda i, be, nu: (layer, be[i], 0, 0)),
                      pl.BlockSpec((1, 1, 1, d), lambda i, be, nu: (layer, be[i], 0, 0))],
            out_specs=any_spec,
            scratch_shapes=[pltpu.SMEM((4, 2 * bm), jnp.int32),
                            pltpu.VMEM((bm, d), F32), pltpu.VMEM((bm, d), F32),
                            pltpu.VMEM((bm, d), F32), pltpu.VMEM((bm, d), F32),
                            pltpu.VMEM((d, two_de), BF16),
                            pltpu.VMEM((de, d), BF16),
                            pltpu.SemaphoreType.DMA((4,)),
                            pltpu.SemaphoreType.DMA((2,)),
                            pltpu.SemaphoreType.DMA((2,))]),
        compiler_params=_cparams(("arbitrary",)),
        name="moe_experts",
    )(block_e, n_used, idx, hid, w1, b1.reshape(b1.shape[0], n_exp, 1, two_de), w2, b2.reshape(b2.shape[0], n_exp, 1, d))


def _combine_kernel(x_ref, y0_ref, y1_ref, y2_ref, y3_ref, gate_ref, ml_ref, mc_ref, o_ref, *, tm, n_ctx):
    i = pl.program_id(1)
    rows = i * tm + lax.broadcasted_iota(jnp.int32, (tm, 1), 0)
    is_ctx = rows < n_ctx
    gates = gate_ref[0]
    y = y0_ref[...] * gates[:, 0:1]
    for k, y_ref in enumerate((y1_ref, y2_ref, y3_ref), start=1):
        y = y + y_ref[...] * gates[:, k:k + 1]
    o_ref[0] = x_ref[0] + _row_mod(ml_ref, mc_ref, 5, is_ctx) * y


def moe_combine(x, ys, gates, mod, n_ctx):
    bsz, l, d = x.shape
    tm = 256
    nrow = l // tm
    plane = lambda k: pl.BlockSpec((tm, d), lambda b, i: ((k * bsz + b) * nrow + i, 0))
    return pl.pallas_call(
        functools.partial(_combine_kernel, tm=tm, n_ctx=n_ctx),
        out_shape=jax.ShapeDtypeStruct((bsz, l, d), F32),
        grid=(bsz, nrow),
        in_specs=[pl.BlockSpec((1, tm, d), lambda b, i: (b, i, 0)),
                  plane(0), plane(1), plane(2), plane(3),
                  pl.BlockSpec((1, tm, TOP_K), lambda b, i: (b, i, 0)),
                  pl.BlockSpec((1, 6, d), lambda b, i: (b, 0, 0)),
                  pl.BlockSpec((1, 6, d), lambda b, i: (bsz, 0, 0))],
        out_specs=pl.BlockSpec((1, tm, d), lambda b, i: (b, i, 0)),
        compiler_params=_cparams(("parallel", "parallel")),
        name="moe_combine",
    )(x, ys, ys, ys, ys, gates, mod, mod)


def moe_layer(x, hid, top_i, gates, mod, w1, b1, w2, b2, layer, n_ctx):
    bsz, l, d = x.shape
    t = bsz * l
    bm = MOE_BLOCK_ROWS
    n_blocks = (t * TOP_K + bm - 1) // bm + N_EXPERTS
    flat_e = top_i.reshape(-1)
    entry_bits = (t * TOP_K - 1).bit_length()
    packed = jnp.sort(flat_e * (1 << entry_bits) + jnp.arange(t * TOP_K, dtype=jnp.int32))
    order = packed & ((1 << entry_bits) - 1)
    counts = jnp.sum((flat_e[:, None] == jnp.arange(N_EXPERTS)[None, :]).astype(jnp.int32), axis=0)
    starts = jnp.cumsum(counts) - counts
    padded = (counts + bm - 1) // bm * bm
    pends = jnp.cumsum(padded)
    pstarts = pends - padded
    blk_first = jnp.arange(n_blocks, dtype=jnp.int32) * bm
    in_expert = jnp.logical_and(blk_first[:, None] >= pstarts[None, :], blk_first[:, None] < pends[None, :])
    pick = lambda v: jnp.sum(jnp.where(in_expert, v[None, :], 0), axis=1)
    block_e = jnp.minimum(jnp.sum((blk_first[:, None] >= pends[None, :]).astype(jnp.int32), axis=1), N_EXPERTS - 1)
    off = blk_first - pick(pstarts)
    n_valid = jnp.clip(pick(counts) - off, 0, bm).astype(jnp.int32)
    n_used = (pends[-1] // bm).astype(jnp.int32).reshape(1)
    r = jnp.arange(bm, dtype=jnp.int32)[None, :]
    valid = r < n_valid[:, None]
    pos = jnp.where(valid, (pick(starts) + off)[:, None] + r, 0)
    src = order[pos]
    tok, kk = src // TOP_K, src % TOP_K
    dst = jnp.where(valid, kk * t + tok, TOP_K * t + r)
    idx = jnp.concatenate([tok, dst], axis=1).astype(jnp.int32)
    ys = moe_experts(hid.reshape(t, d), idx, block_e.astype(jnp.int32), n_used, w1, b1, w2, b2, layer)
    return moe_combine(x, ys, gates, mod, n_ctx)


def _final_norm_kernel(x_ref, g_ref, o_ref):
    x = x_ref[0]
    o_ref[0] = (x * lax.rsqrt(jnp.mean(x * x, axis=-1, keepdims=True) + NORM_EPS)) * g_ref[...]


def final_norm(x, g):
    bsz, l, d = x.shape
    tm = 512
    return pl.pallas_call(
        _final_norm_kernel,
        out_shape=jax.ShapeDtypeStruct((bsz, l, d), F32),
        grid=(bsz, l // tm),
        in_specs=[pl.BlockSpec((1, tm, d), lambda b, i: (b, i, 0)), pl.BlockSpec((1, d), lambda b, i: (0, 0))],
        out_specs=pl.BlockSpec((1, tm, d), lambda b, i: (b, i, 0)),
        compiler_params=_cparams(("parallel", "parallel")),
        name="final_norm",
    )(x, g.reshape(1, d))


ATT_BLOCK = 128
LOG2_E = 1.4426950408889634
NEG_BIG = -1e30


def _qk(q, k):
    return lax.dot_general(q, k, (((1,), (1,)), ((), ())), preferred_element_type=F32)


def _pair_queries(q2, lane):
    return jnp.concatenate([jnp.where(lane < HEAD_DIM, q2, 0.0), jnp.where(lane < HEAD_DIM, 0.0, q2)], axis=0).astype(BF16)


def _pair_outputs(acc, lane):
    r = acc.shape[0] // 2
    return jnp.where(lane < HEAD_DIM, acc[:r], acc[r:])


def _softmax_pv(scores, values, extra_logit=None):
    m = functools.reduce(jnp.maximum, [jnp.max(s, axis=-1, keepdims=True) for s in scores])
    if extra_logit is not None:
        m = jnp.maximum(m, extra_logit)
    ps = [jnp.exp2(s - m) for s in scores]
    den = functools.reduce(jnp.add, [jnp.sum(p, axis=-1, keepdims=True) for p in ps])
    if extra_logit is not None:
        den = den + jnp.exp2(extra_logit - m)
    acc = functools.reduce(jnp.add, [jnp.dot(p.astype(BF16), v, preferred_element_type=F32) for p, v in zip(ps, values)])
    return acc / den


def _na_kernel(q_ref, k_ref, v_ref, bias_ref, o_ref, *, n_ctx, blk_off, n_dbl):
    blk = pl.program_id(1) + blk_off
    n_ctx_blk = n_ctx // ATT_BLOCK
    s2 = jnp.clip(blk - n_ctx_blk - 2, 0, n_dbl - 5)
    start = pl.multiple_of(n_ctx + s2 * ATT_BLOCK, ATT_BLOCK)
    lane = lax.broadcasted_iota(jnp.int32, (1, 2 * HEAD_DIM), 1)
    scale = HEAD_DIM ** -0.5 * LOG2_E

    def pair_out(p, local):
        cs = slice(p * 2 * HEAD_DIM, (p + 1) * 2 * HEAD_DIM)
        q2 = q_ref[0, :, cs] * scale
        kc = k_ref[0, 0:n_ctx, cs].astype(BF16)
        vc = v_ref[0, 0:n_ctx, cs].astype(BF16)
        if local:
            kl = k_ref[0, pl.ds(start, 5 * ATT_BLOCK), cs].astype(BF16)
            vl = v_ref[0, pl.ds(start, 5 * ATT_BLOCK), cs].astype(BF16)
        qp = _pair_queries(q2, lane)
        scores, values = [_qk(qp, kc)], [vc]
        if local:
            scores.append(_qk(qp, kl) + bias_ref[0, p])
            values.append(vl)
        o_ref[0, :, cs] = _pair_outputs(_softmax_pv(scores, values), lane)

    @pl.when(blk < n_ctx_blk)
    def _():
        for p in range(NA_HEADS // 2):
            pair_out(p, False)

    @pl.when(blk >= n_ctx_blk)
    def _():
        for p in range(NA_HEADS // 2):
            pair_out(p, True)


def _na_bias_tables(rpb, rows):
    n_dbl = rows // 2
    wr = min(NA_WIN_R, rows)
    cidx = np.arange(GRID_W)
    c0 = np.clip(cidx - NA_WIN_C // 2, 0, GRID_W - NA_WIN_C)
    cvalid = (cidx[None, :] >= c0[:, None]) & (cidx[None, :] < c0[:, None] + NA_WIN_C)
    coff = np.clip(cidx[None, :] - cidx[:, None] + (NA_WIN_C - 1), 0, 2 * NA_WIN_C - 2)
    col_onehot = (coff[:, :, None] == np.arange(2 * NA_WIN_C - 1)[None, None, :]) & cvalid[:, :, None]
    pats = (0, 1, 2, n_dbl - 2, n_dbl - 1)
    row_onehot = np.zeros((len(pats), 2, 10, 2 * NA_WIN_R - 1), np.float32)
    for pi, r2 in enumerate(pats):
        s2 = int(np.clip(r2 - 2, 0, n_dbl - 5))
        for qh in range(2):
            qr = 2 * r2 + qh
            r0 = int(np.clip(qr - wr // 2, 0, rows - wr))
            for kk in range(10):
                kr = 2 * s2 + kk
                if r0 <= kr < r0 + wr:
                    row_onehot[pi, qh, kk, kr - qr + NA_WIN_R - 1] = 1.0
    t1 = jnp.einsum('hrc,qkc->hrqk', rpb, jnp.asarray(col_onehot, F32), precision=lax.Precision.HIGHEST)
    bias = jnp.einsum('pajr,hrqk->phaqjk', jnp.asarray(row_onehot), t1, precision=lax.Precision.HIGHEST)
    valid = (row_onehot.sum(-1)[:, None, :, None, :, None] > 0) & cvalid[None, None, None, :, None, :]
    bias = jnp.where(jnp.asarray(valid), bias * LOG2_E, NEG_BIG)
    return bias.reshape(len(pats), rpb.shape[0] // 2, 4 * GRID_W, 10 * GRID_W)


def na_attention(proj, rpb, n_ctx, blk_off):
    bsz, l, _ = proj.shape
    rows = (l - n_ctx) // GRID_W
    n_dbl = rows // 2
    n_ctx_blk = n_ctx // ATT_BLOCK
    bias = _na_bias_tables(rpb, rows)

    def bias_idx(b, i):
        r2 = i + blk_off - n_ctx_blk
        pat = jnp.where(r2 < 2, r2, jnp.where(r2 >= n_dbl - 2, r2 - (n_dbl - 5), 2))
        return (jnp.clip(pat, 0, 4), 0, 0, 0)

    return pl.pallas_call(
        functools.partial(_na_kernel, n_ctx=n_ctx, blk_off=blk_off, n_dbl=n_dbl),
        out_shape=jax.ShapeDtypeStruct((bsz, l - blk_off * ATT_BLOCK, NA_DIM), F32),
        grid=(bsz, l // ATT_BLOCK - blk_off),
        in_specs=[pl.BlockSpec((1, ATT_BLOCK, NA_DIM), lambda b, i: (b, i + blk_off, COL_QN // NA_DIM)),
                  pl.BlockSpec((1, l, NA_DIM), lambda b, i: (b, 0, COL_KN // NA_DIM)),
                  pl.BlockSpec((1, l, NA_DIM), lambda b, i: (b, 0, COL_VN // NA_DIM)),
                  pl.BlockSpec((1, NA_HEADS // 2, 2 * ATT_BLOCK, 5 * ATT_BLOCK), bias_idx)],
        out_specs=pl.BlockSpec((1, ATT_BLOCK, NA_DIM), lambda b, i: (b, i, 0)),
        compiler_params=_cparams(("parallel", "arbitrary")),
        name="na_attention",
    )(proj, proj, proj, bias)


def _swa_kernel(sink_ref, q_ref, k_ref, v_ref, cq_ref, sq_ref, ck_ref, sk_ref, o_ref, *, n_ctx, blk_off, n_blk):
    blk = pl.program_id(1) + blk_off
    n_ctx_blk = n_ctx // ATT_BLOCK
    n = blk - n_ctx_blk
    first = jnp.clip(n - 1, 0, n_blk - 3)
    kstart = pl.multiple_of(first * ATT_BLOCK, ATT_BLOCK)
    lane = lax.broadcasted_iota(jnp.int32, (1, 2 * HEAD_DIM), 1)
    scale = HEAD_DIM ** -0.5 * LOG2_E
    quarter = HEAD_DIM // 4

    def rope(u, cos, sin):
        w = u.shape[-1]
        li = lax.broadcasted_iota(jnp.int32, (1, w), 1)
        swapped = jnp.where(li % (2 * quarter) < quarter, pltpu.roll(u, w - quarter, 1), pltpu.roll(u, quarter, 1))
        return u * cos + swapped * sin

    def dup_head(x, h):
        other = pltpu.roll(x, HEAD_DIM, 1)
        return jnp.where((lane < HEAD_DIM) == (h == 0), x, other)

    def run(local):
        q = q_ref[0] * scale
        kc_all = k_ref[0, 0:n_ctx, :]
        vc_all = v_ref[0, 0:n_ctx, :]
        if local:
            q = rope(q, cq_ref[...], sq_ref[...])
            kl_all = rope(k_ref[0, pl.ds(n_ctx + kstart, 3 * ATT_BLOCK), :],
                          ck_ref[pl.ds(kstart, 3 * ATT_BLOCK), :], sk_ref[pl.ds(kstart, 3 * ATT_BLOCK), :])
            vl_all = v_ref[0, pl.ds(n_ctx + kstart, 3 * ATT_BLOCK), :]
            qrow = lax.broadcasted_iota(jnp.int32, (2 * ATT_BLOCK, 3 * ATT_BLOCK), 0) % ATT_BLOCK
            kpos = kstart + lax.broadcasted_iota(jnp.int32, (2 * ATT_BLOCK, 3 * ATT_BLOCK), 1)
            band = jnp.where(jnp.abs(kpos - (n * ATT_BLOCK + qrow)) <= SWA_WINDOW, 0.0, NEG_BIG)
        group = SWA_Q_HEADS // SWA_KV_HEADS
        first_head = lax.broadcasted_iota(jnp.int32, (2 * ATT_BLOCK, 1), 0) < ATT_BLOCK
        for h in range(SWA_KV_HEADS):
            kc = dup_head(kc_all, h).astype(BF16)
            vc = dup_head(vc_all, h).astype(BF16)
            if local:
                kl = dup_head(kl_all, h).astype(BF16)
                vl = dup_head(vl_all, h).astype(BF16)
            for p in range(h * group // 2, (h + 1) * group // 2):
                cs = slice(p * 2 * HEAD_DIM, (p + 1) * 2 * HEAD_DIM)
                qp = _pair_queries(q[:, cs], lane)
                scores, values = [_qk(qp, kc)], [vc]
                if local:
                    scores.append(_qk(qp, kl) + band)
                    values.append(vl)
                sink = jnp.where(first_head, sink_ref[2 * p], sink_ref[2 * p + 1]) * LOG2_E
                o_ref[0, :, cs] = _pair_outputs(_softmax_pv(scores, values, extra_logit=sink), lane)

    @pl.when(blk < n_ctx_blk)
    def _():
        run(False)

    @pl.when(blk >= n_ctx_blk)
    def _():
        run(True)


def _rope_tables(s):
    pos = np.arange(s)
    quarter = HEAD_DIM // 4
    inv = ROPE_THETA ** (-jnp.arange(0, 2 * quarter, 2, dtype=F32) / (2 * quarter))

    def axis_tables(p):
        ang = jnp.asarray(p, F32)[:, None] * inv[None, :]
        c, sn = jnp.cos(ang), jnp.sin(ang)
        return jnp.concatenate([c, c], axis=1), jnp.concatenate([-sn, sn], axis=1)
    cr, sr = axis_tables(pos // GRID_W)
    cc, sc = axis_tables(pos % GRID_W)
    return jnp.concatenate([cr, cc], axis=1), jnp.concatenate([sr, sc], axis=1)


def swa_attention(proj, sink, n_ctx, blk_off):
    bsz, l, _ = proj.shape
    s = l - n_ctx
    n_blk = s // ATT_BLOCK
    n_ctx_blk = n_ctx // ATT_BLOCK
    cos, sin = _rope_tables(s)
    cq, sq = jnp.tile(cos, (1, SWA_Q_HEADS)), jnp.tile(sin, (1, SWA_Q_HEADS))
    ck, sk = jnp.tile(cos, (1, SWA_KV_HEADS)), jnp.tile(sin, (1, SWA_KV_HEADS))
    qtab = lambda b, i: (jnp.maximum(i + blk_off - n_ctx_blk, 0), 0)
    return pl.pallas_call(
        functools.partial(_swa_kernel, n_ctx=n_ctx, blk_off=blk_off, n_blk=n_blk),
        out_shape=jax.ShapeDtypeStruct((bsz, l - blk_off * ATT_BLOCK, SWA_Q), F32),
        grid=(bsz, l // ATT_BLOCK - blk_off),
        in_specs=[pl.BlockSpec(memory_space=pltpu.SMEM),
                  pl.BlockSpec((1, ATT_BLOCK, SWA_Q), lambda b, i: (b, i + blk_off, COL_QS // SWA_Q)),
                  pl.BlockSpec((1, l, SWA_KV), lambda b, i: (b, 0, COL_KS // SWA_KV)),
                  pl.BlockSpec((1, l, SWA_KV), lambda b, i: (b, 0, COL_VS // SWA_KV)),
                  pl.BlockSpec((ATT_BLOCK, SWA_Q), qtab),
                  pl.BlockSpec((ATT_BLOCK, SWA_Q), qtab),
                  pl.BlockSpec((s, SWA_KV), lambda b, i: (0, 0)),
                  pl.BlockSpec((s, SWA_KV), lambda b, i: (0, 0))],
        out_specs=pl.BlockSpec((1, ATT_BLOCK, SWA_Q), lambda b, i: (b, i, 0)),
        compiler_params=_cparams(("parallel", "arbitrary")),
        name="swa_attention",
    )(sink, proj, proj, proj, cq, sq, ck, sk)


CONV_HALO = 8


def _ssd_kernel(*refs, reverse, n_ctx, n_chunks):
    if reverse:
        xsc_ref, bcc_ref, dt_ref, dtbias_ref, acoef_ref, z_ref, yf_ref, dskip_ref, ng_ref, o_ref, state_ref = refs
    else:
        (xs_c, xs_p, xs_n, bc_c, bc_p, bc_n, dt_ref, cwx_ref, cbx_ref, cwb_ref, cbb_ref, dtbias_ref, acoef_ref,
         o_ref, xsc_ref, bcc_ref, state_ref) = refs
    q = SSD_CHUNK
    n_st = SSD_STATE
    gw = SSD_INNER // SSD_GROUPS
    j = pl.program_id(1)
    c = _ssd_chunk_of_step(j, reverse, n_ctx // q, n_chunks)
    n_cc = n_ctx // q
    seg_first = jnp.logical_or(c == 0, c == n_cc)
    seg_last = jnp.logical_or(c == n_cc - 1, c == n_chunks - 1)

    @pl.when(j == 0)
    def _():
        state_ref[...] = jnp.zeros_like(state_ref)

    def conv_silu(cur_ref, prev_ref, next_ref, w_ref, b_ref):
        half = w_ref.shape[0] // 2
        prev = jnp.where(seg_first, 0.0, prev_ref[0])
        nxt = jnp.where(seg_last, 0.0, next_ref[0])
        ext = jnp.concatenate([prev, cur_ref[0], nxt], axis=0)
        acc = b_ref[...]
        for k in range(w_ref.shape[0]):
            o = CONV_HALO - half + k
            acc = acc + ext[o:o + q] * w_ref[k:k + 1, :]
        return acc * jax.nn.sigmoid(acc)

    if reverse:
        xs, bcv = xsc_ref[0], bcc_ref[0]
    else:
        xs = conv_silu(xs_c, xs_p, xs_n, cwx_ref, cbx_ref)
        bcv = conv_silu(bc_c, bc_p, bc_n, cwb_ref, cbb_ref)
        xsc_ref[0] = xs
        bcc_ref[0] = bcv

    lo = SSD_HEADS if reverse else 0
    dt_raw = dt_ref[0] + dtbias_ref[...]
    dt_all = jnp.maximum(dt_raw, 0.0) + jnp.log(1.0 + jnp.exp(-jnp.abs(dt_raw)))
    a_all = dt_all * acoef_ref[...]
    ri = lax.broadcasted_iota(jnp.int32, (q, q), 0)
    ci = lax.broadcasted_iota(jnp.int32, (q, q), 1)
    causal = (ci >= ri) if reverse else (ci <= ri)
    a_cs = jnp.dot(causal.astype(F32), a_all, precision=lax.Precision.HIGHEST, preferred_element_type=F32)
    total = a_cs[0:1] if reverse else a_cs[q - 1:q]
    a_cs_t = a_cs.T

    lane = lax.broadcasted_iota(jnp.int32, (1, 2 * SSD_HEAD_DIM), 1)

    def expand(v):
        parts = [jnp.where(lane < SSD_HEAD_DIM, v[:, lo + 2 * p:lo + 2 * p + 1], v[:, lo + 2 * p + 1:lo + 2 * p + 2])
                 for p in range(SSD_HEADS // 2)]
        return jnp.concatenate(parts, axis=1)

    xdt = xs * expand(dt_all)
    dec_start = expand(jnp.exp(a_cs))
    dec_end = expand(jnp.exp(total - a_cs))
    dec_total = expand(jnp.exp(total))

    heads_per_group = SSD_HEADS // SSD_GROUPS
    y_parts = []
    for g in range(SSD_GROUPS):
        b_g = bcv[:, g * n_st:(g + 1) * n_st]
        c_g = bcv[:, (SSD_GROUPS + g) * n_st:(SSD_GROUPS + g + 1) * n_st].astype(BF16)
        cb = _qk(c_g, b_g.astype(BF16))
        cols = slice(g * gw, (g + 1) * gw)
        st = state_ref[:, cols]
        y_off = jnp.dot(c_g, st.astype(BF16), preferred_element_type=F32) * dec_start[:, cols]
        diag = []
        for p in range(g * heads_per_group // 2, (g + 1) * heads_per_group // 2):
            x2 = xdt[:, p * 2 * SSD_HEAD_DIM:(p + 1) * 2 * SSD_HEAD_DIM].astype(BF16)
            outs = []
            for hh in range(2):
                h = lo + 2 * p + hh
                seg = jnp.where(causal, a_cs[:, h:h + 1] - a_cs_t[h:h + 1, :], NEG_BIG)
                m = (cb * jnp.exp(seg)).astype(BF16)
                outs.append(jnp.dot(m, x2, preferred_element_type=F32))
            diag.append(jnp.where(lane < SSD_HEAD_DIM, outs[0], outs[1]))
        y_parts.append(jnp.concatenate(diag, axis=1) + y_off)
        xw = (xdt[:, cols] * dec_end[:, cols]).astype(BF16)
        state_ref[:, cols] = dec_total[:, cols] * st + jnp.dot(b_g.T.astype(BF16), xw, preferred_element_type=F32)
    y = jnp.concatenate(y_parts, axis=1)

    if reverse:
        y = yf_ref[0] + y + dskip_ref[...] * xs
        z = z_ref[0]
        y = y * (z * jax.nn.sigmoid(z))
        o_ref[0] = (y * lax.rsqrt(jnp.mean(y * y, axis=-1, keepdims=True) + NORM_EPS)) * ng_ref[...]
    else:
        o_ref[0] = y


def _ssd_chunk_of_step(j, reverse, n_ctx_chunks, n_chunks):
    if not reverse:
        return j
    return jnp.where(j < n_ctx_chunks, n_ctx_chunks - 1 - j, n_chunks - 1 - (j - n_ctx_chunks))


def ssd_mixer(proj, conv_w, conv_b, a_log, dt_bias, d_skip, norm_g, n_ctx):
    bsz, l, _ = proj.shape
    q = SSD_CHUNK
    n_chunks = l // q
    n_cc = n_ctx // q
    hb = q // CONV_HALO
    pad = lambda v: jnp.concatenate([v.reshape(1, -1), jnp.zeros((1, 128 - v.size), F32)], axis=1)
    dtb = pad(dt_bias)
    acoef = pad(-jnp.exp(a_log))
    cwx, cwb = conv_w[:, :SSD_INNER], conv_w[:, SSD_INNER:]
    cbx, cbb = conv_b[:SSD_INNER].reshape(1, -1), conv_b[SSD_INNER:].reshape(1, -1)
    wbc = cwb.shape[1]
    dsk = jnp.repeat(d_skip, SSD_HEAD_DIM).reshape(1, SSD_INNER)
    ng = norm_g.reshape(1, SSD_INNER)
    full = lambda a: pl.BlockSpec(a.shape, lambda b, jj: (0, 0))

    def call(reverse, inputs, in_specs, out_widths):
        chunk = lambda jj: _ssd_chunk_of_step(jj, reverse, n_cc, n_chunks)
        cur = lambda w, off: pl.BlockSpec((1, q, w), lambda b, jj: (b, chunk(jj), off // w))
        prev = lambda w, off: pl.BlockSpec((1, CONV_HALO, w), lambda b, jj: (b, jnp.maximum(chunk(jj) * hb - 1, 0), off // w))
        nxt = lambda w, off: pl.BlockSpec((1, CONV_HALO, w),
                                          lambda b, jj: (b, jnp.minimum((chunk(jj) + 1) * hb, l // CONV_HALO - 1), off // w))
        return pl.pallas_call(
            functools.partial(_ssd_kernel, reverse=reverse, n_ctx=n_ctx, n_chunks=n_chunks),
            out_shape=tuple(jax.ShapeDtypeStruct((bsz, l, w), F32) for w in out_widths),
            grid=(bsz, n_chunks),
            in_specs=in_specs(cur, prev, nxt),
            out_specs=tuple(cur(w, 0) for w in out_widths),
            scratch_shapes=[pltpu.VMEM((SSD_STATE, SSD_INNER), F32)],
            compiler_params=_cparams(("parallel", "arbitrary")),
            name="ssd_bwd" if reverse else "ssd_fwd",
        )(*inputs)

    y_f, xs_conv, bc_conv = call(
        False, (proj, proj, proj, proj, proj, proj, proj, cwx, cbx, cwb, cbb, dtb, acoef),
        lambda cur, prev, nxt: [cur(SSD_INNER, COL_XS), prev(SSD_INNER, COL_XS), nxt(SSD_INNER, COL_XS),
                                cur(wbc, COL_BC), prev(wbc, COL_BC), nxt(wbc, COL_BC),
                                cur(128, COL_DT), full(cwx), full(cbx), full(cwb), full(cbb), full(dtb), full(acoef)],
        (SSD_INNER, SSD_INNER, wbc))
    (out,) = call(
        True, (xs_conv, bc_conv, proj, dtb, acoef, proj, y_f, dsk, ng),
        lambda cur, prev, nxt: [cur(SSD_INNER, 0), cur(wbc, 0), cur(128, COL_DT), full(dtb), full(acoef),
                                cur(SSD_INNER, COL_Z), cur(SSD_INNER, 0), full(dsk), full(ng)],
        (SSD_INNER,))
    return out


def _pack_w_in(w):
    o = 0
    segs = {}
    for name, width in (('z', SSD_INNER), ('xs', SSD_INNER), ('bc', 512), ('dt', 2 * SSD_HEADS), ('qs', SWA_Q),
                        ('ks', SWA_KV), ('vs', SWA_KV), ('qn', NA_DIM), ('kn', NA_DIM), ('vn', NA_DIM),
                        ('g', N_BRANCH * D_MODEL)):
        segs[name] = w[:, o:o + width]
        o += width
    pad = jnp.zeros((w.shape[0], 128 - 2 * SSD_HEADS), w.dtype)
    order = ('g', 'z', 'xs', 'bc', 'qs', 'qn', 'kn', 'vn', 'ks', 'vs', 'dt')
    return jnp.concatenate([segs[k] for k in order] + [pad], axis=1).astype(BF16)


def kernel(x, c, ctx, c_ctx, norm1_g, norm2_g, w_ada, b_ada, w_in, b_gate, ssd_conv_w, ssd_conv_b, ssd_a_log,
           ssd_dt_bias, ssd_d, ssd_norm_g, swa_sink, na_rpb, w_br_ssd, w_br_swa, w_br_na, w_out, router_w, router_b,
           moe_w1, moe_b1, moe_w2, moe_b2, final_norm_g):
    bsz, s, d = x.shape
    lc = ctx.shape[1]
    depth = w_in.shape[0]
    n_mod_rows = 16
    cond = jnp.concatenate([c, c_ctx[None, :], jnp.zeros((n_mod_rows - bsz - 1, d), F32)], axis=0)
    xs = jnp.concatenate([ctx, x], axis=1)
    for i in range(depth):
        last = i == depth - 1
        mod = ada_modulation(cond, w_ada, b_ada, i).reshape(n_mod_rows, 6, d)
        proj = norm_proj(xs, mod, norm1_g[i], _pack_w_in(w_in[i]), lc)
        o_ssd = ssd_mixer(proj, ssd_conv_w[i], ssd_conv_b[i], ssd_a_log[i], ssd_dt_bias[i], ssd_d[i], ssd_norm_g[i], lc)
        blk_off = lc // ATT_BLOCK if last else 0
        o_swa = swa_attention(proj, swa_sink[i], lc, blk_off)
        o_na = na_attention(proj, na_rpb[i], lc, blk_off)
        xs, hid, top_i, gates = merge(xs, proj, o_ssd, o_swa, o_na, b_gate[i].reshape(N_BRANCH, d),
                                      w_br_ssd[i].astype(BF16), w_br_swa[i].astype(BF16), w_br_na[i].astype(BF16),
                                      w_out[i].astype(BF16), norm2_g[i], router_w[i], router_b[i], mod, lc,
                                      lc if last else 0)
        xs = moe_layer(xs, hid, top_i, gates, mod, moe_w1, moe_b1, moe_w2, moe_b2, i, 0 if last else lc)
    return final_norm(xs, final_norm_g)
```

```python
import functools

import jax
import jax.numpy as jnp
import numpy as np
from jax import lax
from jax.experimental import pallas as pl
from jax.experimental.pallas import tpu as pltpu

F32 = jnp.float32
BF16 = jnp.bfloat16

D_MODEL = 1024
GRID_W = 64
HEAD_DIM = 64
NORM_EPS = 1e-6

SSD_HEADS = 16
SSD_HEAD_DIM = 64
SSD_INNER = SSD_HEADS * SSD_HEAD_DIM
SSD_GROUPS = 2
SSD_STATE = 128
SSD_CHUNK = 128

SWA_Q_HEADS = 8
SWA_KV_HEADS = 2
SWA_WINDOW = 128
SWA_Q = SWA_Q_HEADS * HEAD_DIM
SWA_KV = SWA_KV_HEADS * HEAD_DIM
ROPE_THETA = 10000.0

NA_HEADS = 8
NA_WIN_R = 8
NA_WIN_C = 16
NA_DIM = NA_HEADS * HEAD_DIM

N_BRANCH = 3
N_EXPERTS = 32
TOP_K = 4
SWIGLU_LIMIT = 7.0
SWIGLU_ALPHA = 1.702

COL_GATE = 0
COL_Z = 3 * D_MODEL
COL_XS = COL_Z + SSD_INNER
COL_BC = COL_XS + SSD_INNER
COL_QS = COL_BC + 512
COL_QN = COL_QS + SWA_Q
COL_KN = COL_QN + NA_DIM
COL_VN = COL_KN + NA_DIM
COL_KS = COL_VN + NA_DIM
COL_VS = COL_KS + SWA_KV
COL_DT = COL_VS + SWA_KV
IN_COLS_PACKED = COL_DT + 128

MOE_BLOCK_ROWS = 256
ROW_UNROLL = 8
EXPERT_CHUNKS = 8
VMEM_LIMIT = 56 * 1024 * 1024


def _cparams(sem):
    return pltpu.CompilerParams(dimension_semantics=sem, vmem_limit_bytes=VMEM_LIMIT)


def _ada_kernel(c_ref, w_ref, b_ref, o_ref):
    c = c_ref[...]
    s = (c * jax.nn.sigmoid(c)).astype(BF16)
    o_ref[...] = jnp.dot(s, w_ref[0].astype(BF16), preferred_element_type=F32) + b_ref[0]


def ada_modulation(rows, w_ada, b_ada, layer):
    r, d = rows.shape
    n = w_ada.shape[2]
    tn = 1024
    return pl.pallas_call(
        _ada_kernel,
        out_shape=jax.ShapeDtypeStruct((r, n), F32),
        grid=(n // tn,),
        in_specs=[pl.BlockSpec((r, d), lambda j: (0, 0)),
                  pl.BlockSpec((1, d, tn), lambda j: (layer, 0, j)),
                  pl.BlockSpec((1, 1, tn), lambda j: (layer, 0, j))],
        out_specs=pl.BlockSpec((r, tn), lambda j: (0, j)),
        compiler_params=_cparams(("parallel",)),
        name="ada_modulation",
    )(rows, w_ada, b_ada.reshape(b_ada.shape[0], 1, n))


def _row_mod(mod_l_ref, mod_c_ref, idx, is_ctx):
    return jnp.where(is_ctx, mod_c_ref[0, idx:idx + 1, :], mod_l_ref[0, idx:idx + 1, :])


def _rms_mod(x, g, scale, shift):
    y = x * lax.rsqrt(jnp.mean(x * x, axis=-1, keepdims=True) + NORM_EPS)
    return (y * g) * (1.0 + scale) + shift


def _norm_proj_kernel(x_ref, ml_ref, mc_ref, g_ref, w_ref, o_ref, h_ref, *, tm, n_ctx):
    i = pl.program_id(1)
    j = pl.program_id(2)

    @pl.when(j == 0)
    def _():
        rows = i * tm + lax.broadcasted_iota(jnp.int32, (tm, 1), 0)
        is_ctx = rows < n_ctx
        h = _rms_mod(x_ref[0], g_ref[...], _row_mod(ml_ref, mc_ref, 1, is_ctx), _row_mod(ml_ref, mc_ref, 0, is_ctx))
        h_ref[...] = h.astype(BF16)

    o_ref[0] = jnp.dot(h_ref[...], w_ref[...], preferred_element_type=F32)


def norm_proj(x, mod, g, w_packed, n_ctx):
    bsz, l, d = x.shape
    n = w_packed.shape[1]
    tm, tn = 768, 2688
    return pl.pallas_call(
        functools.partial(_norm_proj_kernel, tm=tm, n_ctx=n_ctx),
        out_shape=jax.ShapeDtypeStruct((bsz, l, n), F32),
        grid=(bsz, l // tm, n // tn),
        in_specs=[pl.BlockSpec((1, tm, d), lambda b, i, j: (b, i, 0)),
                  pl.BlockSpec((1, 6, d), lambda b, i, j: (b, 0, 0)),
                  pl.BlockSpec((1, 6, d), lambda b, i, j: (bsz, 0, 0)),
                  pl.BlockSpec((1, d), lambda b, i, j: (0, 0)),
                  pl.BlockSpec((d, tn), lambda b, i, j: (0, j))],
        out_specs=pl.BlockSpec((1, tm, tn), lambda b, i, j: (b, i, j)),
        scratch_shapes=[pltpu.VMEM((tm, d), BF16)],
        compiler_params=_cparams(("parallel", "parallel", "arbitrary")),
        name="norm_proj",
    )(x, mod, mod, g.reshape(1, d), w_packed)


def _merge_kernel(x_ref, ga_ref, gb_ref, gn_ref, ossd_ref, oswa_ref, ona_ref, bg_ref, wssd_ref, wswa_ref, wna_ref,
                  wout_ref, g2_ref, rw_ref, rb_ref, ml_ref, mc_ref, o_ref, hid_ref, idx_ref, gate_ref,
                  *, tm, n_ctx, row_off):
    i = pl.program_id(1)
    rows = row_off + i * tm + lax.broadcasted_iota(jnp.int32, (tm, 1), 0)
    is_ctx = rows < n_ctx

    def branch(gate_ref, k, o_br_ref, w_ref):
        gate = jax.nn.sigmoid(gate_ref[0] + bg_ref[k:k + 1, :])
        return gate * jnp.dot(o_br_ref[0].astype(BF16), w_ref[...], preferred_element_type=F32)

    m = branch(ga_ref, 0, ossd_ref, wssd_ref) + branch(gb_ref, 1, oswa_ref, wswa_ref) + branch(gn_ref, 2, ona_ref, wna_ref)
    y = jnp.dot(m.astype(BF16), wout_ref[...], preferred_element_type=F32)
    x_new = x_ref[0] + _row_mod(ml_ref, mc_ref, 2, is_ctx) * y
    o_ref[0] = x_new
    h = _rms_mod(x_new, g2_ref[...], _row_mod(ml_ref, mc_ref, 4, is_ctx), _row_mod(ml_ref, mc_ref, 3, is_ctx))
    hid_ref[0] = h
    _route_top_k(h, rw_ref, rb_ref, idx_ref, gate_ref)


def merge(x, proj, o_ssd, o_swa, o_na, b_gate, w_ssd, w_swa, w_na, w_out, norm2_g, router_w, router_b, mod, n_ctx,
          row_off):
    bsz, l, d = x.shape
    g2, rw, rb = norm2_g.reshape(1, d), router_w.astype(BF16), router_b.reshape(1, N_EXPERTS)
    tm = 256
    gate_blk = COL_GATE // d
    off_blk = row_off // tm

    def row(a):
        o = off_blk if a.shape[1] == l else 0
        return pl.BlockSpec((1, tm, a.shape[-1]), lambda b, i: (b, i + o, 0))

    full = lambda a: pl.BlockSpec(a.shape, lambda b, i: (0,) * a.ndim)
    out_rows = lambda w, dt: jax.ShapeDtypeStruct((bsz, l - row_off, w), dt)
    out_blk = lambda w: pl.BlockSpec((1, tm, w), lambda b, i: (b, i, 0))
    return pl.pallas_call(
        functools.partial(_merge_kernel, tm=tm, n_ctx=n_ctx, row_off=row_off),
        out_shape=(out_rows(d, F32), out_rows(d, F32), out_rows(TOP_K, jnp.int32), out_rows(TOP_K, F32)),
        grid=(bsz, (l - row_off) // tm),
        in_specs=[row(x),
                  pl.BlockSpec((1, tm, d), lambda b, i: (b, i + off_blk, gate_blk)),
                  pl.BlockSpec((1, tm, d), lambda b, i: (b, i + off_blk, gate_blk + 1)),
                  pl.BlockSpec((1, tm, d), lambda b, i: (b, i + off_blk, gate_blk + 2)),
                  row(o_ssd), row(o_swa), row(o_na),
                  full(b_gate), full(w_ssd), full(w_swa), full(w_na), full(w_out), full(g2), full(rw), full(rb),
                  pl.BlockSpec((1, 6, d), lambda b, i: (b, 0, 0)),
                  pl.BlockSpec((1, 6, d), lambda b, i: (bsz, 0, 0))],
        out_specs=(out_blk(d), out_blk(d), out_blk(TOP_K), out_blk(TOP_K)),
        compiler_params=_cparams(("parallel", "parallel")),
        name="merge",
    )(x, proj, proj, proj, o_ssd, o_swa, o_na, b_gate, w_ssd, w_swa, w_na, w_out, g2, rw, rb, mod, mod)


def _route_top_k(h, rw_ref, rb_ref, idx_ref, gate_ref):
    tm = h.shape[0]
    logits = jnp.dot(h.astype(BF16), rw_ref[...], preferred_element_type=F32) + rb_ref[...]
    lane = lax.broadcasted_iota(jnp.int32, logits.shape, 1)
    vals, idxs = [], []
    for _ in range(TOP_K):
        m = jnp.max(logits, axis=-1, keepdims=True)
        sel = jnp.min(jnp.where(logits == m, lane, N_EXPERTS), axis=-1, keepdims=True)
        vals.append(m)
        idxs.append(sel)
        logits = jnp.where(lane == sel, -jnp.inf, logits)
    ex = [jnp.exp(v - vals[0]) for v in vals]
    denom = ex[0] + ex[1] + ex[2] + ex[3]
    col = lax.broadcasted_iota(jnp.int32, (tm, TOP_K), 1)
    idx_out = jnp.zeros((tm, TOP_K), jnp.int32)
    gate_out = jnp.zeros((tm, TOP_K), F32)
    for k in range(TOP_K):
        idx_out = jnp.where(col == k, idxs[k], idx_out)
        gate_out = jnp.where(col == k, ex[k] / denom, gate_out)
    idx_ref[0] = idx_out
    gate_ref[0] = gate_out


def _expert_kernel(be_ref, nu_ref, idx_hbm, hid_hbm, w1_ref, b1_ref, w2_ref, b2_ref, out_hbm,
                   idx_smem, xbuf0, xbuf1, ybuf0, ybuf1, w1b_ref, w2b_ref, idx_sem, in_sem, out_sem, *, n_real_rows):
    i = pl.program_id(0)
    n_used = nu_ref[0]
    xbufs, ybufs = (xbuf0, xbuf1), (ybuf0, ybuf1)
    bm = xbuf0.shape[0]
    n_idx_slots = idx_smem.shape[0]
    de = w2_ref.shape[2]

    def idx_copy(blk):
        sl = blk % n_idx_slots
        return pltpu.make_async_copy(idx_hbm.at[blk], idx_smem.at[sl], idx_sem.at[sl])

    def row_in(blk, par, r):
        tok = idx_smem[blk % n_idx_slots, r]
        return pltpu.make_async_copy(hid_hbm.at[pl.ds(tok, 1)], xbufs[par].at[pl.ds(r, 1)], in_sem.at[par])

    def row_out(blk, par, r):
        dst = idx_smem[blk % n_idx_slots, bm + r]
        return pltpu.make_async_copy(ybufs[par].at[pl.ds(r, 1)], out_hbm.at[pl.ds(dst, 1)], out_sem.at[par])

    def rows_in_wait(par):
        pltpu.make_async_copy(hid_hbm.at[pl.ds(0, bm)], xbufs[par], in_sem.at[par]).wait()

    def rows_out_wait(par):
        pltpu.make_async_copy(ybufs[par], out_hbm.at[pl.ds(0, bm)], out_sem.at[par]).wait()

    def for_rows(copy_of_row):
        def group(gi, carry):
            for u in range(ROW_UNROLL):
                copy_of_row(gi * ROW_UNROLL + u).start(priority=u % 2)
            return carry
        lax.fori_loop(0, bm // ROW_UNROLL, group, 0)

    def experts(par, issue_rows):
        x = xbufs[par][...].astype(BF16)
        n_chunks = EXPERT_CHUNKS
        cw = 2 * de // n_chunks
        rows_per = bm // n_chunks
        gu = []
        for cix in range(n_chunks):
            cols = slice(cix * cw, (cix + 1) * cw)
            gu.append(jnp.dot(x, w1b_ref[:, cols], preferred_element_type=F32) + b1_ref[0, 0, :, cols])
            issue_rows(cix * rows_per, (cix + 1) * rows_per)
        half = n_chunks // 2
        acts = []
        for cix in range(half):
            gate = jnp.minimum(gu[cix], SWIGLU_LIMIT)
            up = jnp.clip(gu[half + cix], -SWIGLU_LIMIT, SWIGLU_LIMIT)
            acts.append(((up + 1.0) * gate * jax.nn.sigmoid(SWIGLU_ALPHA * gate)).astype(BF16))
        act = jnp.concatenate(acts, axis=1)
        ybufs[par][...] = jnp.dot(act, w2b_ref[...], preferred_element_type=F32) + b2_ref[0, 0]

    def step(par):
        other = 1 - par
        if par == 0:
            @pl.when(i == 0)
            def _():
                ybuf1[...] = jnp.zeros(ybuf1.shape, F32)
                fill = pltpu.make_async_copy(ybuf1, out_hbm.at[pl.ds(n_real_rows, bm)], out_sem.at[1])
                fill.start()
                fill.wait()
                idx_copy(0).start()
                idx_copy(0).wait()
                for_rows(lambda r: row_in(0, 0, r))

                @pl.when(n_used > 1)
                def _():
                    idx_copy(1).start()

        @pl.when(jnp.logical_and(i >= 2, i <= n_used))
        def _():
            rows_out_wait(par)

        @pl.when(i + 1 < n_used)
        def _():
            idx_copy(i + 1).wait()

            @pl.when(i + 2 < n_used)
            def _():
                idx_copy(i + 2).start()

        @pl.when(jnp.logical_and(i < n_used, jnp.logical_or(i == 0, be_ref[i] != be_ref[jnp.maximum(i - 1, 0)])))
        def _():
            w1b_ref[...] = w1_ref[0, 0].astype(BF16)
            w2b_ref[...] = w2_ref[0, 0].astype(BF16)

        @pl.when(i < n_used)
        def _():
            rows_in_wait(par)

        steady = jnp.logical_and(i >= 1, i + 1 < n_used)

        @pl.when(steady)
        def _():
            def issue_rows(lo, hi):
                for r in range(lo, hi):
                    row_in(i + 1, other, r).start(priority=r % 2)
                for r in range(lo, hi):
                    row_out(i - 1, other, r).start(priority=r % 2)
            experts(par, issue_rows)

        @pl.when(jnp.logical_not(steady))
        def _():
            @pl.when(i + 1 < n_used)
            def _():
                for_rows(lambda r: row_in(i + 1, other, r))

            @pl.when(jnp.logical_and(i >= 1, i <= n_used))
            def _():
                for_rows(lambda r: row_out(i - 1, other, r))

            @pl.when(i < n_used)
            def _():
                experts(par, lambda lo, hi: None)

            @pl.when(i == n_used)
            def _():
                rows_out_wait(other)

    @pl.when(i % 2 == 0)
    def _():
        step(0)

    @pl.when(i % 2 == 1)
    def _():
        step(1)


def moe_experts(hid, idx, block_e, n_used, w1, b1, w2, b2, layer):
    t, d = hid.shape
    n_blocks = idx.shape[0]
    bm = idx.shape[1] // 2
    _, n_exp, _, two_de = w1.shape
    de = two_de // 2
    any_spec = pl.BlockSpec(memory_space=pl.ANY)
    return pl.pallas_call(
        functools.partial(_expert_kernel, n_real_rows=TOP_K * t),
        out_shape=jax.ShapeDtypeStruct((TOP_K * t + bm, d), F32),
        grid_spec=pltpu.PrefetchScalarGridSpec(
            num_scalar_prefetch=2,
            grid=(n_blocks,),
            in_specs=[any_spec, any_spec,
                      pl.BlockSpec((1, 1, d, two_de), lambda i, be, nu: (layer, be[i], 0, 0)),
                      pl.BlockSpec((1, 1, 1, two_de), lambda i, be, nu: (layer, be[i], 0, 0)),
                      pl.BlockSpec((1, 1, de, d), lambda i, be, nu: (layer, be[i], 0, 0)),
                      pl.BlockSpec((1, 1, 1, d), lambda i, be, nu: (layer, be[i], 0, 0))],
            out_specs=any_spec,
            scratch_shapes=[pltpu.SMEM((4, 2 * bm), jnp.int32),
                            pltpu.VMEM((bm, d), F32), pltpu.VMEM((bm, d), F32),
                            pltpu.VMEM((bm, d), F32), pltpu.VMEM((bm, d), F32),
                            pltpu.VMEM((d, two_de), BF16),
                            pltpu.VMEM((de, d), BF16),
                            pltpu.SemaphoreType.DMA((4,)),
                            pltpu.SemaphoreType.DMA((2,)),
                            pltpu.SemaphoreType.DMA((2,))]),
        compiler_params=_cparams(("arbitrary",)),
        name="moe_experts",
    )(block_e, n_used, idx, hid, w1, b1.reshape(b1.shape[0], n_exp, 1, two_de), w2, b2.reshape(b2.shape[0], n_exp, 1, d))


def _combine_kernel(x_ref, y0_ref, y1_ref, y2_ref, y3_ref, gate_ref, ml_ref, mc_ref, o_ref, *, tm, n_ctx):
    i = pl.program_id(1)
    rows = i * tm + lax.broadcasted_iota(jnp.int32, (tm, 1), 0)
    is_ctx = rows < n_ctx
    gates = gate_ref[0]
    y = y0_ref[...] * gates[:, 0:1]
    for k, y_ref in enumerate((y1_ref, y2_ref, y3_ref), start=1):
        y = y + y_ref[...] * gates[:, k:k + 1]
    o_ref[0] = x_ref[0] + _row_mod(ml_ref, mc_ref, 5, is_ctx) * y


def moe_combine(x, ys, gates, mod, n_ctx):
    bsz, l, d = x.shape
    tm = 256
    nrow = l // tm
    plane = lambda k: pl.BlockSpec((tm, d), lambda b, i: ((k * bsz + b) * nrow + i, 0))
    return pl.pallas_call(
        functools.partial(_combine_kernel, tm=tm, n_ctx=n_ctx),
        out_shape=jax.ShapeDtypeStruct((bsz, l, d), F32),
        grid=(bsz, nrow),
        in_specs=[pl.BlockSpec((1, tm, d), lambda b, i: (b, i, 0)),
                  plane(0), plane(1), plane(2), plane(3),
                  pl.BlockSpec((1, tm, TOP_K), lambda b, i: (b, i, 0)),
                  pl.BlockSpec((1, 6, d), lambda b, i: (b, 0, 0)),
                  pl.BlockSpec((1, 6, d), lambda b, i: (bsz, 0, 0))],
        out_specs=pl.BlockSpec((1, tm, d), lambda b, i: (b, i, 0)),
        compiler_params=_cparams(("parallel", "parallel")),
        name="moe_combine",
    )(x, ys, ys, ys, ys, gates, mod, mod)


def moe_layer(x, hid, top_i, gates, mod, w1, b1, w2, b2, layer, n_ctx):
    bsz, l, d = x.shape
    t = bsz * l
    bm = MOE_BLOCK_ROWS
    n_blocks = (t * TOP_K + bm - 1) // bm + N_EXPERTS
    flat_e = top_i.reshape(-1)
    entry_bits = (t * TOP_K - 1).bit_length()
    packed = jnp.sort(flat_e * (1 << entry_bits) + jnp.arange(t * TOP_K, dtype=jnp.int32))
    order = packed & ((1 << entry_bits) - 1)
    counts = jnp.sum((flat_e[:, None] == jnp.arange(N_EXPERTS)[None, :]).astype(jnp.int32), axis=0)
    starts = jnp.cumsum(counts) - counts
    padded = (counts + bm - 1) // bm * bm
    pends = jnp.cumsum(padded)
    pstarts = pends - padded
    blk_first = jnp.arange(n_blocks, dtype=jnp.int32) * bm
    in_expert = jnp.logical_and(blk_first[:, None] >= pstarts[None, :], blk_first[:, None] < pends[None, :])
    pick = lambda v: jnp.sum(jnp.where(in_expert, v[None, :], 0), axis=1)
    block_e = jnp.minimum(jnp.sum((blk_first[:, None] >= pends[None, :]).astype(jnp.int32), axis=1), N_EXPERTS - 1)
    off = blk_first - pick(pstarts)
    n_valid = jnp.clip(pick(counts) - off, 0, bm).astype(jnp.int32)
    n_used = (pends[-1] // bm).astype(jnp.int32).reshape(1)
    r = jnp.arange(bm, dtype=jnp.int32)[None, :]
    valid = r < n_valid[:, None]
    pos = jnp.where(valid, (pick(starts) + off)[:, None] + r, 0)
    src = order[pos]
    tok, kk = src // TOP_K, src % TOP_K
    dst = jnp.where(valid, kk * t + tok, TOP_K * t + r)
    idx = jnp.concatenate([tok, dst], axis=1).astype(jnp.int32)
    ys = moe_experts(hid.reshape(t, d), idx, block_e.astype(jnp.int32), n_used, w1, b1, w2, b2, layer)
    return moe_combine(x, ys, gates, mod, n_ctx)


def _final_norm_kernel(x_ref, g_ref, o_ref):
    x = x_ref[0]
    o_ref[0] = (x * lax.rsqrt(jnp.mean(x * x, axis=-1, keepdims=True) + NORM_EPS)) * g_ref[...]


def final_norm(x, g):
    bsz, l, d = x.shape
    tm = 512
    return pl.pallas_call(
        _final_norm_kernel,
        out_shape=jax.ShapeDtypeStruct((bsz, l, d), F32),
        grid=(bsz, l // tm),
        in_specs=[pl.BlockSpec((1, tm, d), lambda b, i: (b, i, 0)), pl.BlockSpec((1, d), lambda b, i: (0, 0))],
        out_specs=pl.BlockSpec((1, tm, d), lambda b, i: (b, i, 0)),
        compiler_params=_cparams(("parallel", "parallel")),
        name="final_norm",
    )(x, g.reshape(1, d))


ATT_BLOCK = 128
LOG2_E = 1.4426950408889634
NEG_BIG = -1e30


def _qk(q, k):
    return lax.dot_general(q, k, (((1,), (1,)), ((), ())), preferred_element_type=F32)


def _pair_queries(q2, lane):
    return jnp.concatenate([jnp.where(lane < HEAD_DIM, q2, 0.0), jnp.where(lane < HEAD_DIM, 0.0, q2)], axis=0).astype(BF16)


def _pair_outputs(acc, lane):
    r = acc.shape[0] // 2
    return jnp.where(lane < HEAD_DIM, acc[:r], acc[r:])


def _softmax_pv(scores, values, extra_logit=None):
    m = functools.reduce(jnp.maximum, [jnp.max(s, axis=-1, keepdims=True) for s in scores])
    if extra_logit is not None:
        m = jnp.maximum(m, extra_logit)
    ps = [jnp.exp2(s - m) for s in scores]
    den = functools.reduce(jnp.add, [jnp.sum(p, axis=-1, keepdims=True) for p in ps])
    if extra_logit is not None:
        den = den + jnp.exp2(extra_logit - m)
    acc = functools.reduce(jnp.add, [jnp.dot(p.astype(BF16), v, preferred_element_type=F32) for p, v in zip(ps, values)])
    return acc / den


def _na_kernel(q_ref, k_ref, v_ref, bias_ref, o_ref, *, n_ctx, blk_off, n_dbl):
    blk = pl.program_id(1) + blk_off
    n_ctx_blk = n_ctx // ATT_BLOCK
    s2 = jnp.clip(blk - n_ctx_blk - 2, 0, n_dbl - 5)
    start = pl.multiple_of(n_ctx + s2 * ATT_BLOCK, ATT_BLOCK)
    lane = lax.broadcasted_iota(jnp.int32, (1, 2 * HEAD_DIM), 1)
    scale = HEAD_DIM ** -0.5 * LOG2_E

    def pair_out(p, local):
        cs = slice(p * 2 * HEAD_DIM, (p + 1) * 2 * HEAD_DIM)
        q2 = q_ref[0, :, cs] * scale
        kc = k_ref[0, 0:n_ctx, cs].astype(BF16)
        vc = v_ref[0, 0:n_ctx, cs].astype(BF16)
        if local:
            kl = k_ref[0, pl.ds(start, 5 * ATT_BLOCK), cs].astype(BF16)
            vl = v_ref[0, pl.ds(start, 5 * ATT_BLOCK), cs].astype(BF16)
        qp = _pair_queries(q2, lane)
        scores, values = [_qk(qp, kc)], [vc]
        if local:
            scores.append(_qk(qp, kl) + bias_ref[0, p])
            values.append(vl)
        o_ref[0, :, cs] = _pair_outputs(_softmax_pv(scores, values), lane)

    @pl.when(blk < n_ctx_blk)
    def _():
        for p in range(NA_HEADS // 2):
            pair_out(p, False)

    @pl.when(blk >= n_ctx_blk)
    def _():
        for p in range(NA_HEADS // 2):
            pair_out(p, True)


def _na_bias_tables(rpb, rows):
    n_dbl = rows // 2
    wr = min(NA_WIN_R, rows)
    cidx = np.arange(GRID_W)
    c0 = np.clip(cidx - NA_WIN_C // 2, 0, GRID_W - NA_WIN_C)
    cvalid = (cidx[None, :] >= c0[:, None]) & (cidx[None, :] < c0[:, None] + NA_WIN_C)
    coff = np.clip(cidx[None, :] - cidx[:, None] + (NA_WIN_C - 1), 0, 2 * NA_WIN_C - 2)
    col_onehot = (coff[:, :, None] == np.arange(2 * NA_WIN_C - 1)[None, None, :]) & cvalid[:, :, None]
    pats = (0, 1, 2, n_dbl - 2, n_dbl - 1)
    row_onehot = np.zeros((len(pats), 2, 10, 2 * NA_WIN_R - 1), np.float32)
    for pi, r2 in enumerate(pats):
        s2 = int(np.clip(r2 - 2, 0, n_dbl - 5))
        for qh in range(2):
            qr = 2 * r2 + qh
            r0 = int(np.clip(qr - wr // 2, 0, rows - wr))
            for kk in range(10):
                kr = 2 * s2 + kk
                if r0 <= kr < r0 + wr:
                    row_onehot[pi, qh, kk, kr - qr + NA_WIN_R - 1] = 1.0
    t1 = jnp.einsum('hrc,qkc->hrqk', rpb, jnp.asarray(col_onehot, F32), precision=lax.Precision.HIGHEST)
    bias = jnp.einsum('pajr,hrqk->phaqjk', jnp.asarray(row_onehot), t1, precision=lax.Precision.HIGHEST)
    valid = (row_onehot.sum(-1)[:, None, :, None, :, None] > 0) & cvalid[None, None, None, :, None, :]
    bias = jnp.where(jnp.asarray(valid), bias * LOG2_E, NEG_BIG)
    return bias.reshape(len(pats), rpb.shape[0] // 2, 4 * GRID_W, 10 * GRID_W)


def na_attention(proj, rpb, n_ctx, blk_off):
    bsz, l, _ = proj.shape
    rows = (l - n_ctx) // GRID_W
    n_dbl = rows // 2
    n_ctx_blk = n_ctx // ATT_BLOCK
    bias = _na_bias_tables(rpb, rows)

    def bias_idx(b, i):
        r2 = i + blk_off - n_ctx_blk
        pat = jnp.where(r2 < 2, r2, jnp.where(r2 >= n_dbl - 2, r2 - (n_dbl - 5), 2))
        return (jnp.clip(pat, 0, 4), 0, 0, 0)

    return pl.pallas_call(
        functools.partial(_na_kernel, n_ctx=n_ctx, blk_off=blk_off, n_dbl=n_dbl),
        out_shape=jax.ShapeDtypeStruct((bsz, l - blk_off * ATT_BLOCK, NA_DIM), F32),
        grid=(bsz, l // ATT_BLOCK - blk_off),
        in_specs=[pl.BlockSpec((1, ATT_BLOCK, NA_DIM), lambda b, i: (b, i + blk_off, COL_QN // NA_DIM)),
                  pl.BlockSpec((1, l, NA_DIM), lambda b, i: (b, 0, COL_KN // NA_DIM)),
                  pl.BlockSpec((1, l, NA_DIM), lambda b, i: (b, 0, COL_VN // NA_DIM)),
                  pl.BlockSpec((1, NA_HEADS // 2, 2 * ATT_BLOCK, 5 * ATT_BLOCK), bias_idx)],
        out_specs=pl.BlockSpec((1, ATT_BLOCK, NA_DIM), lambda b, i: (b, i, 0)),
        compiler_params=_cparams(("parallel", "arbitrary")),
        name="na_attention",
    )(proj, proj, proj, bias)


def _swa_kernel(sink_ref, q_ref, k_ref, v_ref, cq_ref, sq_ref, ck_ref, sk_ref, o_ref, *, n_ctx, blk_off, n_blk):
    blk = pl.program_id(1) + blk_off
    n_ctx_blk = n_ctx // ATT_BLOCK
    n = blk - n_ctx_blk
    first = jnp.clip(n - 1, 0, n_blk - 3)
    kstart = pl.multiple_of(first * ATT_BLOCK, ATT_BLOCK)
    lane = lax.broadcasted_iota(jnp.int32, (1, 2 * HEAD_DIM), 1)
    scale = HEAD_DIM ** -0.5 * LOG2_E
    quarter = HEAD_DIM // 4

    def rope(u, cos, sin):
        w = u.shape[-1]
        li = lax.broadcasted_iota(jnp.int32, (1, w), 1)
        swapped = jnp.where(li % (2 * quarter) < quarter, pltpu.roll(u, w - quarter, 1), pltpu.roll(u, quarter, 1))
        return u * cos + swapped * sin

    def dup_head(x, h):
        other = pltpu.roll(x, HEAD_DIM, 1)
        return jnp.where((lane < HEAD_DIM) == (h == 0), x, other)

    def run(local):
        q = q_ref[0] * scale
        kc_all = k_ref[0, 0:n_ctx, :]
        vc_all = v_ref[0, 0:n_ctx, :]
        if local:
            q = rope(q, cq_ref[...], sq_ref[...])
            kl_all = rope(k_ref[0, pl.ds(n_ctx + kstart, 3 * ATT_BLOCK), :],
                          ck_ref[pl.ds(kstart, 3 * ATT_BLOCK), :], sk_ref[pl.ds(kstart, 3 * ATT_BLOCK), :])
            vl_all = v_ref[0, pl.ds(n_ctx + kstart, 3 * ATT_BLOCK), :]
            qrow = lax.broadcasted_iota(jnp.int32, (2 * ATT_BLOCK, 3 * ATT_BLOCK), 0) % ATT_BLOCK
            kpos = kstart + lax.broadcasted_iota(jnp.int32, (2 * ATT_BLOCK, 3 * ATT_BLOCK), 1)
            band = jnp.where(jnp.abs(kpos - (n * ATT_BLOCK + qrow)) <= SWA_WINDOW, 0.0, NEG_BIG)
        group = SWA_Q_HEADS // SWA_KV_HEADS
        first_head = lax.broadcasted_iota(jnp.int32, (2 * ATT_BLOCK, 1), 0) < ATT_BLOCK
        for h in range(SWA_KV_HEADS):
            kc = dup_head(kc_all, h).astype(BF16)
            vc = dup_head(vc_all, h).astype(BF16)
            if local:
                kl = dup_head(kl_all, h).astype(BF16)
                vl = dup_head(vl_all, h).astype(BF16)
            for p in range(h * group // 2, (h + 1) * group // 2):
                cs = slice(p * 2 * HEAD_DIM, (p + 1) * 2 * HEAD_DIM)
                qp = _pair_queries(q[:, cs], lane)
                scores, values = [_qk(qp, kc)], [vc]
                if local:
                    scores.append(_qk(qp, kl) + band)
                    values.append(vl)
                sink = jnp.where(first_head, sink_ref[2 * p], sink_ref[2 * p + 1]) * LOG2_E
                o_ref[0, :, cs] = _pair_outputs(_softmax_pv(scores, values, extra_logit=sink), lane)

    @pl.when(blk < n_ctx_blk)
    def _():
        run(False)

    @pl.when(blk >= n_ctx_blk)
    def _():
        run(True)


def _rope_tables(s):
    pos = np.arange(s)
    quarter = HEAD_DIM // 4
    inv = ROPE_THETA ** (-jnp.arange(0, 2 * quarter, 2, dtype=F32) / (2 * quarter))

    def axis_tables(p):
        ang = jnp.asarray(p, F32)[:, None] * inv[None, :]
        c, sn = jnp.cos(ang), jnp.sin(ang)
        return jnp.concatenate([c, c], axis=1), jnp.concatenate([-sn, sn], axis=1)
    cr, sr = axis_tables(pos // GRID_W)
    cc, sc = axis_tables(pos % GRID_W)
    return jnp.concatenate([cr, cc], axis=1), jnp.concatenate([sr, sc], axis=1)


def swa_attention(proj, sink, n_ctx, blk_off):
    bsz, l, _ = proj.shape
    s = l - n_ctx
    n_blk = s // ATT_BLOCK
    n_ctx_blk = n_ctx // ATT_BLOCK
    cos, sin = _rope_tables(s)
    cq, sq = jnp.tile(cos, (1, SWA_Q_HEADS)), jnp.tile(sin, (1, SWA_Q_HEADS))
    ck, sk = jnp.tile(cos, (1, SWA_KV_HEADS)), jnp.tile(sin, (1, SWA_KV_HEADS))
    qtab = lambda b, i: (jnp.maximum(i + blk_off - n_ctx_blk, 0), 0)
    return pl.pallas_call(
        functools.partial(_swa_kernel, n_ctx=n_ctx, blk_off=blk_off, n_blk=n_blk),
        out_shape=jax.ShapeDtypeStruct((bsz, l - blk_off * ATT_BLOCK, SWA_Q), F32),
        grid=(bsz, l // ATT_BLOCK - blk_off),
        in_specs=[pl.BlockSpec(memory_space=pltpu.SMEM),
                  pl.BlockSpec((1, ATT_BLOCK, SWA_Q), lambda b, i: (b, i + blk_off, COL_QS // SWA_Q)),
                  pl.BlockSpec((1, l, SWA_KV), lambda b, i: (b, 0, COL_KS // SWA_KV)),
                  pl.BlockSpec((1, l, SWA_KV), lambda b, i: (b, 0, COL_VS // SWA_KV)),
                  pl.BlockSpec((ATT_BLOCK, SWA_Q), qtab),
                  pl.BlockSpec((ATT_BLOCK, SWA_Q), qtab),
                  pl.BlockSpec((s, SWA_KV), lambda b, i: (0, 0)),
                  pl.BlockSpec((s, SWA_KV), lambda b, i: (0, 0))],
        out_specs=pl.BlockSpec((1, ATT_BLOCK, SWA_Q), lambda b, i: (b, i, 0)),
        compiler_params=_cparams(("parallel", "arbitrary")),
        name="swa_attention",
    )(sink, proj, proj, proj, cq, sq, ck, sk)


CONV_HALO = 8


def _ssd_kernel(*refs, reverse, n_ctx, n_chunks):
    if reverse:
        xsc_ref, bcc_ref, dt_ref, dtbias_ref, acoef_ref, z_ref, yf_ref, dskip_ref, ng_ref, o_ref, state_ref = refs
    else:
        (xs_c, xs_p, xs_n, bc_c, bc_p, bc_n, dt_ref, cwx_ref, cbx_ref, cwb_ref, cbb_ref, dtbias_ref, acoef_ref,
         o_ref, xsc_ref, bcc_ref, state_ref) = refs
    q = SSD_CHUNK
    n_st = SSD_STATE
    gw = SSD_INNER // SSD_GROUPS
    j = pl.program_id(1)
    c = _ssd_chunk_of_step(j, reverse, n_ctx // q, n_chunks)
    n_cc = n_ctx // q
    seg_first = jnp.logical_or(c == 0, c == n_cc)
    seg_last = jnp.logical_or(c == n_cc - 1, c == n_chunks - 1)

    @pl.when(j == 0)
    def _():
        state_ref[...] = jnp.zeros_like(state_ref)

    def conv_silu(cur_ref, prev_ref, next_ref, w_ref, b_ref):
        half = w_ref.shape[0] // 2
        prev = jnp.where(seg_first, 0.0, prev_ref[0])
        nxt = jnp.where(seg_last, 0.0, next_ref[0])
        ext = jnp.concatenate([prev, cur_ref[0], nxt], axis=0)
        acc = b_ref[...]
        for k in range(w_ref.shape[0]):
            o = CONV_HALO - half + k
            acc = acc + ext[o:o + q] * w_ref[k:k + 1, :]
        return acc * jax.nn.sigmoid(acc)

    if reverse:
        xs, bcv = xsc_ref[0], bcc_ref[0]
    else:
        xs = conv_silu(xs_c, xs_p, xs_n, cwx_ref, cbx_ref)
        bcv = conv_silu(bc_c, bc_p, bc_n, cwb_ref, cbb_ref)
        xsc_ref[0] = xs
        bcc_ref[0] = bcv

    lo = SSD_HEADS if reverse else 0
    dt_raw = dt_ref[0] + dtbias_ref[...]
    dt_all = jnp.maximum(dt_raw, 0.0) + jnp.log(1.0 + jnp.exp(-jnp.abs(dt_raw)))
    a_all = dt_all * acoef_ref[...]
    ri = lax.broadcasted_iota(jnp.int32, (q, q), 0)
    ci = lax.broadcasted_iota(jnp.int32, (q, q), 1)
    causal = (ci >= ri) if reverse else (ci <= ri)
    a_cs = jnp.dot(causal.astype(F32), a_all, precision=lax.Precision.HIGHEST, preferred_element_type=F32)
    total = a_cs[0:1] if reverse else a_cs[q - 1:q]
    a_cs_t = a_cs.T

    lane = lax.broadcasted_iota(jnp.int32, (1, 2 * SSD_HEAD_DIM), 1)

    def expand(v):
        parts = [jnp.where(lane < SSD_HEAD_DIM, v[:, lo + 2 * p:lo + 2 * p + 1], v[:, lo + 2 * p + 1:lo + 2 * p + 2])
                 for p in range(SSD_HEADS // 2)]
        return jnp.concatenate(parts, axis=1)

    xdt = xs * expand(dt_all)
    dec_start = expand(jnp.exp(a_cs))
    dec_end = expand(jnp.exp(total - a_cs))
    dec_total = expand(jnp.exp(total))

    heads_per_group = SSD_HEADS // SSD_GROUPS
    y_parts = []
    for g in range(SSD_GROUPS):
        b_g = bcv[:, g * n_st:(g + 1) * n_st]
        c_g = bcv[:, (SSD_GROUPS + g) * n_st:(SSD_GROUPS + g + 1) * n_st].astype(BF16)
        cb = _qk(c_g, b_g.astype(BF16))
        cols = slice(g * gw, (g + 1) * gw)
        st = state_ref[:, cols]
        y_off = jnp.dot(c_g, st.astype(BF16), preferred_element_type=F32) * dec_start[:, cols]
        diag = []
        for p in range(g * heads_per_group // 2, (g + 1) * heads_per_group // 2):
            x2 = xdt[:, p * 2 * SSD_HEAD_DIM:(p + 1) * 2 * SSD_HEAD_DIM].astype(BF16)
            outs = []
            for hh in range(2):
                h = lo + 2 * p + hh
                seg = jnp.where(causal, a_cs[:, h:h + 1] - a_cs_t[h:h + 1, :], NEG_BIG)
                m = (cb * jnp.exp(seg)).astype(BF16)
                outs.append(jnp.dot(m, x2, preferred_element_type=F32))
            diag.append(jnp.where(lane < SSD_HEAD_DIM, outs[0], outs[1]))
        y_parts.append(jnp.concatenate(diag, axis=1) + y_off)
        xw = (xdt[:, cols] * dec_end[:, cols]).astype(BF16)
        state_ref[:, cols] = dec_total[:, cols] * st + jnp.dot(b_g.T.astype(BF16), xw, preferred_element_type=F32)
    y = jnp.concatenate(y_parts, axis=1)

    if reverse:
        y = yf_ref[0] + y + dskip_ref[...] * xs
        z = z_ref[0]
        y = y * (z * jax.nn.sigmoid(z))
        o_ref[0] = (y * lax.rsqrt(jnp.mean(y * y, axis=-1, keepdims=True) + NORM_EPS)) * ng_ref[...]
    else:
        o_ref[0] = y


def _ssd_chunk_of_step(j, reverse, n_ctx_chunks, n_chunks):
    if not reverse:
        return j
    return jnp.where(j < n_ctx_chunks, n_ctx_chunks - 1 - j, n_chunks - 1 - (j - n_ctx_chunks))


def ssd_mixer(proj, conv_w, conv_b, a_log, dt_bias, d_skip, norm_g, n_ctx):
    bsz, l, _ = proj.shape
    q = SSD_CHUNK
    n_chunks = l // q
    n_cc = n_ctx // q
    hb = q // CONV_HALO
    pad = lambda v: jnp.concatenate([v.reshape(1, -1), jnp.zeros((1, 128 - v.size), F32)], axis=1)
    dtb = pad(dt_bias)
    acoef = pad(-jnp.exp(a_log))
    cwx, cwb = conv_w[:, :SSD_INNER], conv_w[:, SSD_INNER:]
    cbx, cbb = conv_b[:SSD_INNER].reshape(1, -1), conv_b[SSD_INNER:].reshape(1, -1)
    wbc = cwb.shape[1]
    dsk = jnp.repeat(d_skip, SSD_HEAD_DIM).reshape(1, SSD_INNER)
    ng = norm_g.reshape(1, SSD_INNER)
    full = lambda a: pl.BlockSpec(a.shape, lambda b, jj: (0, 0))

    def call(reverse, inputs, in_specs, out_widths):
        chunk = lambda jj: _ssd_chunk_of_step(jj, reverse, n_cc, n_chunks)
        cur = lambda w, off: pl.BlockSpec((1, q, w), lambda b, jj: (b, chunk(jj), off // w))
        prev = lambda w, off: pl.BlockSpec((1, CONV_HALO, w), lambda b, jj: (b, jnp.maximum(chunk(jj) * hb - 1, 0), off // w))
        nxt = lambda w, off: pl.BlockSpec((1, CONV_HALO, w),
                                          lambda b, jj: (b, jnp.minimum((chunk(jj) + 1) * hb, l // CONV_HALO - 1), off // w))
        return pl.pallas_call(
            functools.partial(_ssd_kernel, reverse=reverse, n_ctx=n_ctx, n_chunks=n_chunks),
            out_shape=tuple(jax.ShapeDtypeStruct((bsz, l, w), F32) for w in out_widths),
            grid=(bsz, n_chunks),
            in_specs=in_specs(cur, prev, nxt),
            out_specs=tuple(cur(w, 0) for w in out_widths),
            scratch_shapes=[pltpu.VMEM((SSD_STATE, SSD_INNER), F32)],
            compiler_params=_cparams(("parallel", "arbitrary")),
            name="ssd_bwd" if reverse else "ssd_fwd",
        )(*inputs)

    y_f, xs_conv, bc_conv = call(
        False, (proj, proj, proj, proj, proj, proj, proj, cwx, cbx, cwb, cbb, dtb, acoef),
        lambda cur, prev, nxt: [cur(SSD_INNER, COL_XS), prev(SSD_INNER, COL_XS), nxt(SSD_INNER, COL_XS),
                                cur(wbc, COL_BC), prev(wbc, COL_BC), nxt(wbc, COL_BC),
                                cur(128, COL_DT), full(cwx), full(cbx), full(cwb), full(cbb), full(dtb), full(acoef)],
        (SSD_INNER, SSD_INNER, wbc))
    (out,) = call(
        True, (xs_conv, bc_conv, proj, dtb, acoef, proj, y_f, dsk, ng),
        lambda cur, prev, nxt: [cur(SSD_INNER, 0), cur(wbc, 0), cur(128, COL_DT), full(dtb), full(acoef),
                                cur(SSD_INNER, COL_Z), cur(SSD_INNER, 0), full(dsk), full(ng)],
        (SSD_INNER,))
    return out


def _pack_w_in(w):
    o = 0
    segs = {}
    for name, width in (('z', SSD_INNER), ('xs', SSD_INNER), ('bc', 512), ('dt', 2 * SSD_HEADS), ('qs', SWA_Q),
                        ('ks', SWA_KV), ('vs', SWA_KV), ('qn', NA_DIM), ('kn', NA_DIM), ('vn', NA_DIM),
                        ('g', N_BRANCH * D_MODEL)):
        segs[name] = w[:, o:o + width]
        o += width
    pad = jnp.zeros((w.shape[0], 128 - 2 * SSD_HEADS), w.dtype)
    order = ('g', 'z', 'xs', 'bc', 'qs', 'qn', 'kn', 'vn', 'ks', 'vs', 'dt')
    return jnp.concatenate([segs[k] for k in order] + [pad], axis=1).astype(BF16)


def kernel(x, c, ctx, c_ctx, norm1_g, norm2_g, w_ada, b_ada, w_in, b_gate, ssd_conv_w, ssd_conv_b, ssd_a_log,
           ssd_dt_bias, ssd_d, ssd_norm_g, swa_sink, na_rpb, w_br_ssd, w_br_swa, w_br_na, w_out, router_w, router_b,
           moe_w1, moe_b1, moe_w2, moe_b2, final_norm_g):
    bsz, s, d = x.shape
    lc = ctx.shape[1]
    depth = w_in.shape[0]
    n_mod_rows = 16
    cond = jnp.concatenate([c, c_ctx[None, :], jnp.zeros((n_mod_rows - bsz - 1, d), F32)], axis=0)
    xs = jnp.concatenate([ctx, x], axis=1)
    for i in range(depth):
        last = i == depth - 1
        mod = ada_modulation(cond, w_ada, b_ada, i).reshape(n_mod_rows, 6, d)
        proj = norm_proj(xs, mod, norm1_g[i], _pack_w_in(w_in[i]), lc)
        o_ssd = ssd_mixer(proj, ssd_conv_w[i], ssd_conv_b[i], ssd_a_log[i], ssd_dt_bias[i], ssd_d[i], ssd_norm_g[i], lc)
        blk_off = lc // ATT_BLOCK if last else 0
        o_swa = swa_attention(proj, swa_sink[i], lc, blk_off)
        o_na = na_attention(proj, na_rpb[i], lc, blk_off)
        xs, hid, top_i, gates = merge(xs, proj, o_ssd, o_swa, o_na, b_gate[i].reshape(N_BRANCH, d),
                                      w_br_ssd[i].astype(BF16), w_br_swa[i].astype(BF16), w_br_na[i].astype(BF16),
                                      w_out[i].astype(BF16), norm2_g[i], router_w[i], router_b[i], mod, lc,
                                      lc if last else 0)
        xs = moe_layer(xs, hid, top_i, gates, mod, moe_w1, moe_b1, moe_w2, moe_b2, i, 0 if last else lc)
    return final_norm(xs, final_norm_g)
```
